```python
import math
import jax, jax.numpy as jnp
from jax import lax
import numpy as np

D_MODEL = 1024
BATCH = 8
SEQ = 2048
DEPTH = 1

GLA_HEADS = 4
GLA_DK = D_MODEL // 8
GLA_DV = D_MODEL // 4
GLA_GATE_RANK = 16
GLA_GATE_TAU = 16.0
GLA_CHUNK = 64
GDN_HEADS = 8
GDN_DK = D_MODEL // 8
GDN_DV = D_MODEL // 8
GDN_CONV = 5
GDN_CHUNK = 64
N_GROUPS = 4
EXPERTS_PER_GROUP = 8
N_EXPERTS = N_GROUPS * EXPERTS_PER_GROUP
TOP_K = 2
D_EXPERT = D_MODEL // 4
MOE_BLOCK = 128
EPS = 1e-6

GLA_QK = GLA_HEADS * GLA_DK
GLA_V = GLA_HEADS * GLA_DV
GDN_QK = GDN_HEADS * GDN_DK
GDN_V = GDN_HEADS * GDN_DV
IN_SPLITS = (
    GLA_QK, GLA_QK, GLA_V, GLA_V,
    GLA_GATE_RANK, GLA_GATE_RANK,
    GDN_QK, GDN_QK, GDN_V, GDN_V,
    4 * GDN_HEADS,
    D_MODEL, D_MODEL,
)
D_IN = sum(IN_SPLITS)

kernel_name = 'hybrid_gla_gdn_hmoe_encoder'


def rms_norm(x, w):
    xf = x.astype(jnp.float32)
    y = xf * lax.rsqrt(jnp.mean(xf * xf, axis=-1, keepdims=True) + EPS)
    return (y * w.astype(jnp.float32)).astype(x.dtype)


def l2_norm(t):
    return t * lax.rsqrt(jnp.sum(t * t, axis=-1, keepdims=True) + EPS)


def to_heads(t, n_heads):
    b, l, _ = t.shape
    return t.reshape(b, l, n_heads, -1).transpose(0, 2, 1, 3)


def from_heads(t):
    b, h, l, d = t.shape
    return t.transpose(0, 2, 1, 3).reshape(b, l, h * d)


def rev(t):
    return jnp.flip(t, axis=2)


def gla_chunked(q, k, v, log_a):
    b_, h_, l_, dk = q.shape
    dv = v.shape[-1]
    c = GLA_CHUNK
    n = l_ // c
    q, k, log_a = (t.reshape(b_, h_, n, c, dk) for t in (q, k, log_a))
    v = v.reshape(b_, h_, n, c, dv)
    cum = jnp.cumsum(log_a, axis=3)
    cum_last = cum[:, :, :, -1:, :]
    q_dec = q * jnp.exp(cum)
    k_inv = k * jnp.exp(-cum)
    causal = jnp.tril(jnp.ones((c, c), dtype=bool))
    scores = jnp.where(causal, jnp.einsum('bhntd,bhnsd->bhnts', q_dec, k_inv), 0.0)
    o_intra = jnp.einsum('bhnts,bhnsv->bhntv', scores, v)
    k_tail = k * jnp.exp(cum_last - cum)
    chunk_kv = jnp.einsum('bhnsd,bhnsv->bhndv', k_tail, v)
    chunk_decay = jnp.exp(cum_last[:, :, :, 0, :])

    def step(state, inp):
        kv, dec = inp
        return dec[..., None] * state + kv, state

    _, states = lax.scan(step, jnp.zeros((b_, h_, dk, dv), q.dtype),
                         (jnp.moveaxis(chunk_kv, 2, 0), jnp.moveaxis(chunk_decay, 2, 0)))
    states = jnp.moveaxis(states, 0, 2)
    o_inter = jnp.einsum('bhntd,bhndv->bhntv', q_dec, states)
    return (o_intra + o_inter).reshape(b_, h_, l_, dv)


def gdn_chunked(q, k, v, g, beta):
    b_, h_, l_, dk = q.shape
    dv = v.shape[-1]
    c = GDN_CHUNK
    n = l_ // c
    q, k = (t.reshape(b_, h_, n, c, dk) for t in (q, k))
    v = v.reshape(b_, h_, n, c, dv)
    g = jnp.cumsum(g.reshape(b_, h_, n, c), axis=-1)
    beta = beta.reshape(b_, h_, n, c)
    tri = jnp.tril(jnp.ones((c, c), dtype=bool))
    strict = jnp.tril(jnp.ones((c, c), dtype=bool), -1)
    diff = g[..., :, None] - g[..., None, :]
    decay = jnp.where(tri, jnp.exp(jnp.where(tri, diff, 0.0)), 0.0)
    k_beta = k * beta[..., None]
    v_beta = v * beta[..., None]
    lower = jnp.where(strict, jnp.einsum('bhntd,bhnsd->bhnts', k_beta, k) * decay, 0.0)
    unit_lower = lower + jnp.eye(c, dtype=q.dtype)
    u = lax.linalg.triangular_solve(unit_lower, v_beta, left_side=True, lower=True,
                                    unit_diagonal=True)
    w = lax.linalg.triangular_solve(unit_lower, k_beta * jnp.exp(g)[..., None],
                                    left_side=True, lower=True, unit_diagonal=True)
    attn = jnp.where(tri, jnp.einsum('bhntd,bhnsd->bhnts', q, k) * decay, 0.0)
    g_last = g[..., -1]
    k_tail = k * jnp.exp(g_last[..., None] - g)[..., None]
    q_dec = q * jnp.exp(g)[..., None]
    chunk_decay = jnp.exp(g_last)

    def step(state, inp):
        q_c, k_c, u_c, w_c, a_c, dec_c = inp
        v_new = u_c - jnp.einsum('bhtd,bhdv->bhtv', w_c, state)
        o = (jnp.einsum('bhtd,bhdv->bhtv', q_c, state)
             + jnp.einsum('bhts,bhsv->bhtv', a_c, v_new))
        state = dec_c[..., None, None] * state + jnp.einsum('bhsd,bhsv->bhdv', k_c, v_new)
        return state, o

    xs = tuple(jnp.moveaxis(t, 2, 0) for t in (q_dec, k_tail, u, w, attn, chunk_decay))
    _, o = lax.scan(step, jnp.zeros((b_, h_, dk, dv), q.dtype), xs)
    return jnp.moveaxis(o, 0, 2).reshape(b_, h_, l_, dv)


def centred_depthwise_conv(x, w):
    kw, ch = w.shape
    pad = kw // 2
    return lax.conv_general_dilated(
        x, w[:, None, :].astype(x.dtype), window_strides=(1,), padding=[(pad, pad)],
        dimension_numbers=('NWC', 'WIO', 'NWC'), feature_group_count=ch)


def token_mixer(h, w_in, gla_w2_f, gla_b_f, gla_w2_b, gla_b_b, gla_norm_w, conv_w,
                a_log_f, dt_bias_f, a_log_b, dt_bias_b, gdn_norm_w, w_out):
    f32 = jnp.float32
    points = [int(p) for p in np.cumsum(IN_SPLITS)[:-1]]
    proj = h @ w_in
    (gq, gk, gv, gr, glr_f, glr_b, dq, dk, dv, dz, dab, m_a, m_b) = jnp.split(proj, points, axis=-1)

    q = to_heads(gq.astype(f32), GLA_HEADS) * (GLA_DK ** -0.5)
    k = to_heads(gk.astype(f32), GLA_HEADS)
    v = to_heads(gv.astype(f32), GLA_HEADS)
    la_f = to_heads(jax.nn.log_sigmoid((glr_f @ gla_w2_f + gla_b_f).astype(f32)) / GLA_GATE_TAU, GLA_HEADS)
    la_b = to_heads(jax.nn.log_sigmoid((glr_b @ gla_w2_b + gla_b_b).astype(f32)) / GLA_GATE_TAU, GLA_HEADS)
    o = gla_chunked(q, k, v, la_f) + rev(gla_chunked(rev(q), rev(k), rev(v), rev(la_b)))
    o_a = from_heads(rms_norm(o, gla_norm_w)) * jax.nn.silu(gr.astype(f32))

    qkv = jax.nn.silu(centred_depthwise_conv(jnp.concatenate([dq, dk, dv], axis=-1), conv_w))
    cq, ck, cv = jnp.split(qkv.astype(f32), [GDN_QK, 2 * GDN_QK], axis=-1)
    q = l2_norm(to_heads(cq, GDN_HEADS)) * (GDN_DK ** -0.5)
    k = l2_norm(to_heads(ck, GDN_HEADS))
    v = to_heads(cv, GDN_HEADS)
    a_f, a_b, b_f, b_b = (t.transpose(0, 2, 1) for t in jnp.split(dab.astype(f32), 4, axis=-1))
    g_f = -jnp.exp(a_log_f.astype(f32))[None, :, None] * jax.nn.softplus(a_f + dt_bias_f.astype(f32)[None, :, None])
    g_b = -jnp.exp(a_log_b.astype(f32))[None, :, None] * jax.nn.softplus(a_b + dt_bias_b.astype(f32)[None, :, None])
    beta_f = jax.nn.sigmoid(b_f)
    beta_b = jax.nn.sigmoid(b_b)
    o = (gdn_chunked(q, k, v, g_f, beta_f)
         + rev(gdn_chunked(rev(q), rev(k), rev(v), rev(g_b), rev(beta_b))))
    o_b = from_heads(rms_norm(o, gdn_norm_w)) * jax.nn.silu(dz.astype(f32))

    mixed = jax.nn.sigmoid(m_a.astype(f32)) * o_a + jax.nn.sigmoid(m_b.astype(f32)) * o_b
    return mixed.astype(h.dtype) @ w_out


def hierarchical_moe(h, w_group, w_router, w_gate, w_up, w_down):
    f32 = jnp.float32
    b_, l_, d = h.shape
    n_tok = b_ * l_
    t = h.reshape(n_tok, d)
    group_logits = (t @ w_group).astype(f32)
    group_prob = jax.nn.softmax(group_logits, axis=-1)
    group_idx = jnp.argmax(group_logits, axis=-1).astype(jnp.int32)
    group_w = jnp.take_along_axis(group_prob, group_idx[:, None], axis=-1)
    exp_logits = (t @ w_router).astype(f32).reshape(n_tok, N_GROUPS, EXPERTS_PER_GROUP)
    exp_logits = jnp.take_along_axis(exp_logits, group_idx[:, None, None], axis=1)[:, 0]
    top_p, top_i = lax.top_k(jax.nn.softmax(exp_logits, axis=-1), TOP_K)
    weights = group_w * top_p / jnp.sum(top_p, axis=-1, keepdims=True)
    expert_id = group_idx[:, None] * EXPERTS_PER_GROUP + top_i.astype(jnp.int32)

    n_assign = n_tok * TOP_K
    flat_e = expert_id.reshape(n_assign)
    flat_tok = jnp.repeat(jnp.arange(n_tok, dtype=jnp.int32), TOP_K)
    flat_w = weights.reshape(n_assign)
    order = jnp.argsort(flat_e)
    sorted_e = flat_e[order]
    counts = jnp.bincount(flat_e, length=N_EXPERTS)
    padded = (counts + MOE_BLOCK - 1) // MOE_BLOCK * MOE_BLOCK
    start = jnp.cumsum(counts) - counts
    ends = jnp.cumsum(padded)
    pstart = ends - padded
    dest = pstart[sorted_e] + jnp.arange(n_assign, dtype=jnp.int32) - start[sorted_e]
    n_blocks = -(-(n_assign + N_EXPERTS * (MOE_BLOCK - 1)) // MOE_BLOCK)
    n_rows = n_blocks * MOE_BLOCK
    row_tok = jnp.full((n_rows,), n_tok, jnp.int32).at[dest].set(flat_tok[order])
    row_w = jnp.zeros((n_rows,), f32).at[dest].set(flat_w[order])
    block_expert = jnp.minimum(
        jnp.searchsorted(ends, jnp.arange(n_blocks, dtype=jnp.int32) * MOE_BLOCK, side='right'),
        N_EXPERTS - 1)
    t_pad = jnp.concatenate([t, jnp.zeros((1, d), t.dtype)], axis=0)
    xb = t_pad[row_tok].reshape(n_blocks, MOE_BLOCK, d)

    def expert_block(args):
        xs, e = args
        hid = jax.nn.silu(xs @ w_gate[e]) * (xs @ w_up[e])
        return hid @ w_down[e]

    yb = lax.map(expert_block, (xb, block_expert)).reshape(n_rows, d)
    out = jnp.zeros((n_tok + 1, d), f32).at[row_tok].add(yb.astype(f32) * row_w[:, None])[:n_tok]
    return out.astype(h.dtype).reshape(b_, l_, d)


def setup_inputs(seed: int = 0) -> dict:
    key = jax.random.key(seed)
    ks = jax.random.split(key, 24)
    f32 = jnp.float32
    nl = DEPTH

    def nrm(k, shape, fan_in):
        return jax.random.normal(k, shape, f32) * (fan_in ** -0.5)

    def gain(k, shape):
        return 1.0 + 0.02 * jax.random.normal(k, shape, f32)

    def dt_bias(k):
        dt = jnp.exp(jax.random.uniform(k, (nl, GDN_HEADS), f32, math.log(1e-3), math.log(1e-1)))
        return dt + jnp.log(-jnp.expm1(-dt))

    def a_log(k):
        return jnp.log(jax.random.uniform(k, (nl, GDN_HEADS), f32, 1.0, 16.0))

    return {
        'x': jax.random.normal(ks[0], (BATCH, SEQ, D_MODEL), f32),
        'norm1_w': gain(ks[1], (nl, D_MODEL)),
        'w_in': nrm(ks[2], (nl, D_MODEL, D_IN), D_MODEL),
        'gla_gate_w2_fwd': nrm(ks[3], (nl, GLA_GATE_RANK, GLA_QK), GLA_GATE_RANK),
        'gla_gate_b_fwd': 0.1 * jax.random.normal(ks[4], (nl, GLA_QK), f32),
        'gla_gate_w2_bwd': nrm(ks[5], (nl, GLA_GATE_RANK, GLA_QK), GLA_GATE_RANK),
        'gla_gate_b_bwd': 0.1 * jax.random.normal(ks[6], (nl, GLA_QK), f32),
        'gla_norm_w': gain(ks[7], (nl, GLA_DV)),
        'gdn_conv_w': nrm(ks[8], (nl, GDN_CONV, 2 * GDN_QK + GDN_V), GDN_CONV),
        'gdn_a_log_fwd': a_log(ks[9]),
        'gdn_dt_bias_fwd': dt_bias(ks[10]),
        'gdn_a_log_bwd': a_log(ks[11]),
        'gdn_dt_bias_bwd': dt_bias(ks[12]),
        'gdn_norm_w': gain(ks[13], (nl, GDN_DV)),
        'w_out': nrm(ks[14], (nl, D_MODEL, D_MODEL), D_MODEL),
        'norm2_w': gain(ks[15], (nl, D_MODEL)),
        'moe_w_group': nrm(ks[16], (nl, D_MODEL, N_GROUPS), D_MODEL),
        'moe_w_router': nrm(ks[17], (nl, D_MODEL, N_EXPERTS), D_MODEL),
        'moe_w_gate': nrm(ks[18], (nl, N_EXPERTS, D_MODEL, D_EXPERT), D_MODEL),
        'moe_w_up': nrm(ks[19], (nl, N_EXPERTS, D_MODEL, D_EXPERT), D_MODEL),
        'moe_w_down': nrm(ks[20], (nl, N_EXPERTS, D_EXPERT, D_MODEL), D_EXPERT),
        'norm_f_w': gain(ks[21], (D_MODEL,)),
    }


def reference(x, norm1_w, w_in, gla_gate_w2_fwd, gla_gate_b_fwd, gla_gate_w2_bwd, gla_gate_b_bwd,
              gla_norm_w, gdn_conv_w, gdn_a_log_fwd, gdn_dt_bias_fwd, gdn_a_log_bwd, gdn_dt_bias_bwd,
              gdn_norm_w, w_out, norm2_w, moe_w_group, moe_w_router, moe_w_gate, moe_w_up,
              moe_w_down, norm_f_w):
    for i in range(DEPTH):
        h = rms_norm(x, norm1_w[i])
        x = x + token_mixer(h, w_in[i], gla_gate_w2_fwd[i], gla_gate_b_fwd[i], gla_gate_w2_bwd[i],
                            gla_gate_b_bwd[i], gla_norm_w[i], gdn_conv_w[i], gdn_a_log_fwd[i],
                            gdn_dt_bias_fwd[i], gdn_a_log_bwd[i], gdn_dt_bias_bwd[i], gdn_norm_w[i],
                            w_out[i])
        h = rms_norm(x, norm2_w[i])
        x = x + hierarchical_moe(h, moe_w_group[i], moe_w_router[i], moe_w_gate[i], moe_w_up[i],
                                 moe_w_down[i])
    return rms_norm(x, norm_f_w)
```

```python
import functools

import jax
import jax.numpy as jnp
from jax import lax
from jax.experimental import pallas as pl
from jax.experimental.pallas import tpu as pltpu

F32 = jnp.float32
BF16 = jnp.bfloat16

D_MODEL = 1024
GLA_HEADS = 4
GLA_DK = 128
GLA_DV = 256
GLA_GATE_RANK = 16
GLA_GATE_TAU = 16.0
GLA_CHUNK = 64
GDN_HEADS = 8
GDN_DK = 128
GDN_DV = 128
GDN_CONV = 5
GDN_CHUNK = 64
N_GROUPS = 4
EXPERTS_PER_GROUP = 8
N_EXPERTS = N_GROUPS * EXPERTS_PER_GROUP
D_EXPERT = 256
EPS = 1e-6

LANES = 128
SUBLANES = 8
VMEM_LIMIT = 48 * 1024 * 1024

COL_GQ, COL_GK, COL_GV, COL_GR = 0, 512, 1024, 2048
COL_DQ, COL_DK, COL_DV, COL_DZ = 3072, 4096, 5120, 6144
COL_MA, COL_MB = 7168, 8192
D_MAIN = 9216
SMALL_AF, SMALL_AB, SMALL_BF, SMALL_BB = 32, 40, 48, 56
ROUTE_GROUP_LANE = 32

MOE_BLOCK = 128
INPROJ_TM, INPROJ_TN = 1024, 1024
OUTPROJ_TM = 512
SCATTER_T = 512
COMBINE_T = 256
CONV_ROWS = 256
NEG_INF = float("-inf")


def _dot(a, b):
    return jnp.dot(a, b, preferred_element_type=F32)


def _dot_nt(a, b):
    return lax.dot_general(a, b, (((1,), (1,)), ((), ())), preferred_element_type=F32)


def _dot_tn(a, b):
    return lax.dot_general(a, b, (((0,), (0,)), ((), ())), preferred_element_type=F32)


def _split2(x):
    hi = x.astype(BF16)
    lo = (x - hi.astype(F32)).astype(BF16)
    return hi, lo


def _split3(x):
    hi = x.astype(BF16)
    r = x - hi.astype(F32)
    mid = r.astype(BF16)
    lo = (r - mid.astype(F32)).astype(BF16)
    return hi, mid, lo


def _dot_exact_rhs(x, m_bf16):
    hi, mid, lo = _split3(x)
    return _dot(hi, m_bf16) + _dot(mid, m_bf16) + _dot(lo, m_bf16)


def _dot_exact_lhs(m_bf16, x):
    hi, mid, lo = _split3(x)
    return _dot(m_bf16, hi) + _dot(m_bf16, mid) + _dot(m_bf16, lo)


def _dot3(a, b):
    ah, al = _split2(a)
    bh, bl = _split2(b)
    return _dot(ah, bh) + _dot(al, bh) + _dot(ah, bl)


def _sigmoid(x):
    return 1.0 / (1.0 + jnp.exp(-x))


def _silu(x):
    return x * _sigmoid(x)


def _softplus(x):
    return jnp.maximum(x, 0.0) + jnp.log(1.0 + jnp.exp(-jnp.abs(x)))


def _log_sigmoid(x):
    return jnp.minimum(x, 0.0) - jnp.log(1.0 + jnp.exp(-jnp.abs(x)))


def _iota2(shape, dim):
    return lax.broadcasted_iota(jnp.int32, shape, dim)


def _inproj_kernel(x_ref, nw_ref, w_ref, wsh_ref, wsl_ref, main_ref, small_ref, h_scr):
    @pl.when(pl.program_id(1) == 0)
    def _():
        x = x_ref[...]
        h = x * lax.rsqrt(jnp.mean(x * x, axis=-1, keepdims=True) + EPS) * nw_ref[...]
        hh, hl = _split2(h)
        h_scr[...] = hh
        small_ref[...] = _dot(hh, wsh_ref[...]) + _dot(hl, wsh_ref[...]) + _dot(hh, wsl_ref[...])

    main_ref[...] = _dot(h_scr[...], w_ref[...]).astype(BF16)


def _inproj(x2, norm_w, w_main, ws_hi, ws_lo):
    n = x2.shape[0]
    tm, tn = INPROJ_TM, INPROJ_TN
    return pl.pallas_call(
        _inproj_kernel,
        grid=(n // tm, D_MAIN // tn),
        in_specs=[
            pl.BlockSpec((tm, D_MODEL), lambda i, j: (i, 0)),
            pl.BlockSpec((1, D_MODEL), lambda i, j: (0, 0)),
            pl.BlockSpec((D_MODEL, tn), lambda i, j: (0, j)),
            pl.BlockSpec((D_MODEL, LANES), lambda i, j: (0, 0)),
            pl.BlockSpec((D_MODEL, LANES), lambda i, j: (0, 0)),
        ],
        out_specs=[
            pl.BlockSpec((tm, tn), lambda i, j: (i, j)),
            pl.BlockSpec((tm, LANES), lambda i, j: (i, 0)),
        ],
        out_shape=[
            jax.ShapeDtypeStruct((n, D_MAIN), BF16),
            jax.ShapeDtypeStruct((n, LANES), F32),
        ],
        scratch_shapes=[pltpu.VMEM((tm, D_MODEL), BF16)],
        compiler_params=pltpu.CompilerParams(
            dimension_semantics=("arbitrary", "arbitrary"), vmem_limit_bytes=VMEM_LIMIT),
        name="inproj",
    )(x2, norm_w, w_main, ws_hi, ws_lo)


def _gla_kernel(q_ref, k_ref, v_ref, gr_ref, ma_ref, small_ref, w2f_ref, w2b_ref, bf_ref, bb_ref,
                nw_ref, out_ref, laf_scr, lab_scr, o_scr, stf_scr, stb_scr, *, seq, chunk):
    c = chunk
    n = seq // c
    scale = GLA_DK ** -0.5

    sm = small_ref[...]
    laf_scr[...] = _log_sigmoid(_dot3(sm, w2f_ref[...]) + bf_ref[...]) * (1.0 / GLA_GATE_TAU)
    lab_scr[...] = _log_sigmoid(_dot3(sm, w2b_ref[...]) + bb_ref[...]) * (1.0 / GLA_GATE_TAU)
    stf_scr[...] = jnp.zeros_like(stf_scr)
    stb_scr[...] = jnp.zeros_like(stb_scr)

    row = _iota2((c, c), 0)
    col = _iota2((c, c), 1)
    low = row >= col
    upp = row <= col
    low_m = jnp.where(low, 1.0, 0.0).astype(BF16)
    upp_m = jnp.where(upp, 1.0, 0.0).astype(BF16)

    def chunk_out(r0, la_scr, st_scr, csum_m, mask, tot_row):
        qc = q_ref[pl.ds(r0, c), :].astype(F32) * scale
        kc = k_ref[pl.ds(r0, c), :].astype(F32)
        vc = v_ref[pl.ds(r0, c), :]
        cum = _dot_exact_lhs(csum_m, la_scr[pl.ds(r0, c), :])
        tot = cum[tot_row:tot_row + 1, :]
        q_dec = (qc * jnp.exp(cum)).astype(BF16)
        k_inv = (kc * jnp.exp(-cum)).astype(BF16)
        k_tail = (kc * jnp.exp(tot - cum)).astype(BF16)
        s = jnp.where(mask, _dot_nt(q_dec, k_inv), 0.0).astype(BF16)
        st = st_scr[...]
        o = _dot(s, vc) + _dot_nt(q_dec, st.astype(BF16))
        st_scr[...] = jnp.exp(tot) * st + _dot_tn(vc, k_tail)
        return o

    def finish(r0, o):
        y = o * lax.rsqrt(jnp.mean(o * o, axis=-1, keepdims=True) + EPS) * nw_ref[...]
        y = y * _silu(gr_ref[pl.ds(r0, c), :].astype(F32))
        y = y * _sigmoid(ma_ref[pl.ds(r0, c), :].astype(F32))
        out_ref[pl.ds(r0, c), :] = y.astype(BF16)

    def first_half(i, carry):
        rf = pl.multiple_of(i * c, c)
        rb = pl.multiple_of((n - 1 - i) * c, c)
        o_scr[pl.ds(rf, c), :] = chunk_out(rf, laf_scr, stf_scr, low_m, low, c - 1)
        o_scr[pl.ds(rb, c), :] = chunk_out(rb, lab_scr, stb_scr, upp_m, upp, 0)
        return carry

    def second_half(i, carry):
        rf = pl.multiple_of(i * c, c)
        rb = pl.multiple_of((n - 1 - i) * c, c)
        of = chunk_out(rf, laf_scr, stf_scr, low_m, low, c - 1)
        ob = chunk_out(rb, lab_scr, stb_scr, upp_m, upp, 0)
        finish(rf, o_scr[pl.ds(rf, c), :] + of)
        finish(rb, o_scr[pl.ds(rb, c), :] + ob)
        return carry

    lax.fori_loop(0, n // 2, first_half, 0)
    lax.fori_loop(n // 2, n, second_half, 0)


def _gla(main, small, w2f_pad, w2b_pad, b_f, b_b, norm_w, batch, seq):
    n = batch * seq
    h = GLA_HEADS
    kern = functools.partial(_gla_kernel, seq=seq, chunk=GLA_CHUNK)
    qk_blk = lambda off: pl.BlockSpec((seq, GLA_DK), lambda b, hh, off=off: (b, off // GLA_DK + hh))
    v_blk = lambda off: pl.BlockSpec((seq, GLA_DV), lambda b, hh, off=off: (b, off // GLA_DV + hh))
    return pl.pallas_call(
        kern,
        grid=(batch, h),
        in_specs=[
            qk_blk(COL_GQ), qk_blk(COL_GK), v_blk(COL_GV), v_blk(COL_GR), v_blk(COL_MA),
            pl.BlockSpec((seq, LANES), lambda b, hh: (b, 0)),
            pl.BlockSpec((LANES, GLA_DK), lambda b, hh: (0, hh)),
            pl.BlockSpec((LANES, GLA_DK), lambda b, hh: (0, hh)),
            pl.BlockSpec((1, GLA_DK), lambda b, hh: (0, hh)),
            pl.BlockSpec((1, GLA_DK), lambda b, hh: (0, hh)),
            pl.BlockSpec((1, GLA_DV), lambda b, hh: (0, 0)),
        ],
        out_specs=pl.BlockSpec((seq, GLA_DV), lambda b, hh: (b, hh)),
        out_shape=jax.ShapeDtypeStruct((n, D_MODEL), BF16),
        scratch_shapes=[
            pltpu.VMEM((seq, GLA_DK), F32), pltpu.VMEM((seq, GLA_DK), F32),
            pltpu.VMEM((seq, GLA_DV), F32),
            pltpu.VMEM((GLA_DV, GLA_DK), F32), pltpu.VMEM((GLA_DV, GLA_DK), F32),
        ],
        compiler_params=pltpu.CompilerParams(
            dimension_semantics=("arbitrary", "arbitrary"), vmem_limit_bytes=VMEM_LIMIT),
        name="gla",
    )(main, main, main, main, main, small, w2f_pad, w2b_pad, b_f, b_b, norm_w)


def _tri_inverse(a, eye, chunk):
    t = eye - a
    p = a
    steps = chunk.bit_length() - 2
    for _ in range(steps):
        p = _dot3(p, p)
        t = t + _dot3(t, p)
    return t


def _gdn_kernel(gate_ref, q_ref, k_ref, v_ref, z_ref, mb_ref, small_ref, cwq_ref, cwk_ref, cwv_ref,
                nw_ref, out_ref, pad_scr, qs_scr, ks_scr, vs_scr, gf_scr, gb_scr, btf_scr, btb_scr,
                o_scr, sf_scr, sb_scr, *, seq, chunk):
    c = chunk
    n = seq // c
    hh = pl.program_id(1)
    scale = GDN_DK ** -0.5

    zeros8 = jnp.zeros((SUBLANES, LANES), F32)
    pad_scr[0:SUBLANES, :] = zeros8
    pad_scr[seq + SUBLANES:seq + 2 * SUBLANES, :] = zeros8
    half = GDN_CONV // 2

    def conv_into(src_ref, cw_ref, dst_ref, normalise, mult):
        pad_scr[SUBLANES:seq + SUBLANES, :] = src_ref[...].astype(F32)
        w = cw_ref[...]

        def body(i, carry):
            r0 = pl.multiple_of(i * CONV_ROWS, CONV_ROWS)
            win = pad_scr[pl.ds(r0, CONV_ROWS + 2 * SUBLANES), :]
            acc = jnp.zeros((CONV_ROWS, LANES), F32)
            for j in range(GDN_CONV):
                o = SUBLANES - half + j
                acc = acc + win[o:o + CONV_ROWS, :] * w[j:j + 1, :]
            y = _silu(acc)
            if normalise:
                y = y * lax.rsqrt(jnp.sum(y * y, axis=-1, keepdims=True) + EPS) * mult
            dst_ref[pl.ds(r0, CONV_ROWS), :] = y.astype(BF16)
            return carry

        lax.fori_loop(0, seq // CONV_ROWS, body, 0)

    conv_into(q_ref, cwq_ref, qs_scr, True, scale)
    conv_into(k_ref, cwk_ref, ks_scr, True, 1.0)
    conv_into(v_ref, cwv_ref, vs_scr, False, 1.0)

    sm = small_ref[...]
    sel_row = _iota2((LANES, LANES), 0)

    def lane_broadcast(lane):
        sel = jnp.where(sel_row == lane, 1.0, 0.0).astype(BF16)
        return _dot_exact_rhs(sm, sel)

    def log_decay(a_log, dt_bias, lane):
        rate = jnp.exp(jnp.full((1, LANES), a_log, F32))
        return -rate * _softplus(lane_broadcast(lane) + dt_bias)

    gf_scr[...] = log_decay(gate_ref[0, hh], gate_ref[1, hh], SMALL_AF + hh)
    gb_scr[...] = log_decay(gate_ref[2, hh], gate_ref[3, hh], SMALL_AB + hh)
    btf_scr[...] = _sigmoid(lane_broadcast(SMALL_BF + hh))
    btb_scr[...] = _sigmoid(lane_broadcast(SMALL_BB + hh))
    sf_scr[...] = jnp.zeros_like(sf_scr)
    sb_scr[...] = jnp.zeros_like(sb_scr)

    row = _iota2((c, c), 0)
    col = _iota2((c, c), 1)
    eye = jnp.where(row == col, 1.0, 0.0).astype(F32)
    low, slow = row >= col, row > col
    upp, supp = row <= col, row < col
    low_m = jnp.where(low, 1.0, 0.0).astype(BF16)
    upp_m = jnp.where(upp, 1.0, 0.0).astype(BF16)

    def chunk_out(r0, g_scr, bt_scr, s_scr, csum_m, incl, strict, tot_row):
        qc = qs_scr[pl.ds(r0, c), :]
        kc = ks_scr[pl.ds(r0, c), :]
        vc = vs_scr[pl.ds(r0, c), :]
        kf = kc.astype(F32)
        gl = g_scr[pl.ds(r0, c), :]
        bt = bt_scr[pl.ds(r0, c), :]
        gc = _dot_exact_lhs(csum_m, gl)
        tot = gc[tot_row:tot_row + 1, :]
        diff = _dot_exact_lhs(csum_m, jnp.where(strict, gl[:, :c], 0.0))
        decay = jnp.where(incl, jnp.exp(diff), 0.0)
        kb = kf * bt
        a = jnp.where(strict, _dot_nt(kb.astype(BF16), kc) * decay, 0.0)
        t_inv = _tri_inverse(a, eye, c)
        egc = jnp.exp(gc)
        u = _dot3(t_inv, vc.astype(F32) * bt)
        w = _dot3(t_inv, kb * egc)
        attn = jnp.where(incl, _dot_nt(qc, kc) * decay, 0.0).astype(BF16)
        q_dec = (qc.astype(F32) * egc).astype(BF16)
        k_tail = (kf * jnp.exp(tot - gc)).astype(BF16)
        s = s_scr[...]
        s_b = s.astype(BF16)
        v_new = u - _dot(w.astype(BF16), s_b)
        v_new_b = v_new.astype(BF16)
        o = _dot(q_dec, s_b) + _dot(attn, v_new_b)
        s_scr[...] = jnp.exp(tot) * s + _dot_tn(k_tail, v_new_b)
        return o

    def finish(r0, o):
        y = o * lax.rsqrt(jnp.mean(o * o, axis=-1, keepdims=True) + EPS) * nw_ref[...]
        y = y * _silu(z_ref[pl.ds(r0, c), :].astype(F32))
        y = y * _sigmoid(mb_ref[pl.ds(r0, c), :].astype(F32))
        out_ref[pl.ds(r0, c), :] = y.astype(BF16)

    def fwd(r0):
        return chunk_out(r0, gf_scr, btf_scr, sf_scr, low_m, low, slow, c - 1)

    def bwd(r0):
        return chunk_out(r0, gb_scr, btb_scr, sb_scr, upp_m, upp, supp, 0)

    def first_half(i, carry):
        rf = pl.multiple_of(i * c, c)
        rb = pl.multiple_of((n - 1 - i) * c, c)
        o_scr[pl.ds(rf, c), :] = fwd(rf)
        o_scr[pl.ds(rb, c), :] = bwd(rb)
        return carry

    def second_half(i, carry):
        rf = pl.multiple_of(i * c, c)
        rb = pl.multiple_of((n - 1 - i) * c, c)
        of = fwd(rf)
        ob = bwd(rb)
        finish(rf, o_scr[pl.ds(rf, c), :] + of)
        finish(rb, o_scr[pl.ds(rb, c), :] + ob)
        return carry

    lax.fori_loop(0, n // 2, first_half, 0)
    lax.fori_loop(n // 2, n, second_half, 0)


def _gdn(main, small, gates, conv_w, norm_w, batch, seq):
    n = batch * seq
    kern = functools.partial(_gdn_kernel, seq=seq, chunk=GDN_CHUNK)
    blk = lambda off: pl.BlockSpec((seq, LANES), lambda b, hh, off=off: (b, off // LANES + hh))
    cw = lambda part: pl.BlockSpec((GDN_CONV, LANES), lambda b, hh, part=part: (0, part * GDN_HEADS + hh))
    seq_f32 = lambda: pltpu.VMEM((seq, LANES), F32)
    seq_bf16 = lambda: pltpu.VMEM((seq, LANES), BF16)
    return pl.pallas_call(
        kern,
        grid=(batch, GDN_HEADS),
        in_specs=[
            pl.BlockSpec(memory_space=pltpu.SMEM),
            blk(COL_DQ), blk(COL_DK), blk(COL_DV), blk(COL_DZ), blk(COL_MB),
            pl.BlockSpec((seq, LANES), lambda b, hh: (b, 0)),
            cw(0), cw(1), cw(2),
            pl.BlockSpec((1, GDN_DV), lambda b, hh: (0, 0)),
        ],
        out_specs=pl.BlockSpec((seq, GDN_DV), lambda b, hh: (b, hh)),
        out_shape=jax.ShapeDtypeStruct((n, D_MODEL), BF16),
        scratch_shapes=[
            pltpu.VMEM((seq + 2 * SUBLANES, LANES), F32),
            seq_bf16(), seq_bf16(), seq_bf16(),
            seq_f32(), seq_f32(), seq_f32(), seq_f32(),
            seq_f32(),
            pltpu.VMEM((GDN_DK, GDN_DV), F32), pltpu.VMEM((GDN_DK, GDN_DV), F32),
        ],
        compiler_params=pltpu.CompilerParams(
            dimension_semantics=("arbitrary", "arbitrary"), vmem_limit_bytes=VMEM_LIMIT),
        name="gdn",
    )(gates, main, main, main, main, main, small, conv_w, conv_w, conv_w, norm_w)


def _outproj_kernel(ga_ref, gb_ref, x_ref, wo_ref, nw_ref, wrh_ref, wrl_ref,
                    x1_ref, h2_ref, ri_ref, rw_ref, cnt_ref, carry_scr, *, tm):
    @pl.when(pl.program_id(0) == 0)
    def _():
        carry_scr[...] = jnp.zeros_like(carry_scr)

    mixed = (ga_ref[...].astype(F32) + gb_ref[...].astype(F32)).astype(BF16)
    x1 = x_ref[...] + _dot(mixed, wo_ref[...])
    x1_ref[...] = x1
    h2 = x1 * lax.rsqrt(jnp.mean(x1 * x1, axis=-1, keepdims=True) + EPS) * nw_ref[...]
    h2_ref[...] = h2
    hh, hl = _split2(h2)
    lg = _dot(hh, wrh_ref[...]) + _dot(hl, wrh_ref[...]) + _dot(hh, wrl_ref[...])

    lane_i = _iota2((tm, LANES), 1)
    lane = lane_i.astype(F32)

    def first_argmax(vals):
        m = jnp.max(vals, axis=-1, keepdims=True)
        idx = jnp.min(jnp.where(vals == m, lane, float(LANES)), axis=-1, keepdims=True)
        return m, idx

    is_g = (lane_i >= ROUTE_GROUP_LANE) & (lane_i < ROUTE_GROUP_LANE + N_GROUPS)
    gmax, glane = first_argmax(jnp.where(is_g, lg, NEG_INF))
    gidx = glane - float(ROUTE_GROUP_LANE)
    gsum = jnp.sum(jnp.where(is_g, jnp.exp(lg - gmax), 0.0), axis=-1, keepdims=True)
    group_w = 1.0 / gsum
    lane_group = jnp.right_shift(lane_i, EXPERTS_PER_GROUP.bit_length() - 1).astype(F32)
    in_grp = (lane_i < N_EXPERTS) & (lane_group == gidx)
    el = jnp.where(in_grp, lg, NEG_INF)
    m1, e0 = first_argmax(el)
    m2, e1 = first_argmax(jnp.where(lane == e0, NEG_INF, el))
    r = jnp.exp(m2 - m1)
    w0 = group_w / (1.0 + r)
    w1 = group_w * r / (1.0 + r)

    pick0 = lane == e0
    pick1 = lane == e1
    onehot = jnp.where(pick0 | pick1, 1.0, 0.0)
    trow = _iota2((tm, tm), 0)
    tcol = _iota2((tm, tm), 1)
    before = jnp.where(trow > tcol, 1.0, 0.0).astype(BF16)
    cnt = _dot(before, onehot.astype(BF16)) + carry_scr[0:1, :]
    rank0 = jnp.sum(jnp.where(pick0, cnt, 0.0), axis=-1, keepdims=True)
    rank1 = jnp.sum(jnp.where(pick1, cnt, 0.0), axis=-1, keepdims=True)
    total = carry_scr[0:1, :] + jnp.sum(onehot, axis=0, keepdims=True)
    carry_scr[...] = jnp.broadcast_to(total, carry_scr.shape)
    cnt_ref[...] = jnp.broadcast_to(total, cnt_ref.shape).astype(jnp.int32)

    ri = jnp.where(lane_i == 0, e0, jnp.where(lane_i == 1, e1, 0.0))
    ri = jnp.where(lane_i == 2, rank0, jnp.where(lane_i == 3, rank1, ri))
    ri_ref[...] = ri.astype(jnp.int32)
    rw_ref[...] = jnp.where(lane_i == 0, w0, jnp.where(lane_i == 1, w1, 0.0))


def _outproj(ga, gb, x2, w_out, norm_w, wr_hi, wr_lo):
    n = x2.shape[0]
    tm = OUTPROJ_TM
    kern = functools.partial(_outproj_kernel, tm=tm)
    row_blk = lambda w: pl.BlockSpec((tm, w), lambda i: (i, 0))
    const = lambda shape: pl.BlockSpec(shape, lambda i: (0, 0))
    return pl.pallas_call(
        kern,
        grid=(n // tm,),
        in_specs=[
            row_blk(D_MODEL), row_blk(D_MODEL), row_blk(D_MODEL),
            const((D_MODEL, D_MODEL)), const((1, D_MODEL)),
            const((D_MODEL, LANES)), const((D_MODEL, LANES)),
        ],
        out_specs=[row_blk(D_MODEL), row_blk(D_MODEL), row_blk(LANES), row_blk(LANES),
                   const((SUBLANES, LANES))],
        out_shape=[
            jax.ShapeDtypeStruct((n, D_MODEL), F32),
            jax.ShapeDtypeStruct((n, D_MODEL), F32),
            jax.ShapeDtypeStruct((n, LANES), jnp.int32),
            jax.ShapeDtypeStruct((n, LANES), F32),
            jax.ShapeDtypeStruct((SUBLANES, LANES), jnp.int32),
        ],
        scratch_shapes=[pltpu.VMEM((SUBLANES, LANES), F32)],
        compiler_params=pltpu.CompilerParams(
            dimension_semantics=("arbitrary",), vmem_limit_bytes=VMEM_LIMIT),
        name="outproj",
    )(ga, gb, x2, w_out, norm_w, wr_hi, wr_lo)


def _row_copy(src_ref, src_row, dst_ref, dst_row, sem):
    return pltpu.make_async_copy(src_ref.at[pl.ds(src_row, 1)], dst_ref.at[pl.ds(dst_row, 1)], sem)


def _scatter_kernel(seg_ref, dest_ref, h2_ref, zero_ref, xs_ref, sem, zsem, *, tile):
    i = pl.program_id(0)
    base = i * tile

    def issue(t, carry):
        _row_copy(h2_ref, base + t, xs_ref, dest_ref[0, 0, t], sem).start()
        _row_copy(h2_ref, base + t, xs_ref, dest_ref[0, 1, t], sem).start()
        return carry

    lax.fori_loop(0, tile, issue, 0)

    @pl.when(i == 0)
    def _():
        def per_expert(e, carry):
            lo, hi = seg_ref[0, e], seg_ref[1, e]

            def start(r, c2):
                _row_copy(zero_ref, 0, xs_ref, r, zsem).start()
                return c2

            def wait(r, c2):
                _row_copy(zero_ref, 0, xs_ref, r, zsem).wait()
                return c2

            lax.fori_loop(lo, hi, start, 0)
            lax.fori_loop(lo, hi, wait, 0)
            return carry

        lax.fori_loop(0, N_EXPERTS + 1, per_expert, 0)

    def drain(t, carry):
        _row_copy(h2_ref, base, xs_ref, 0, sem).wait()
        _row_copy(h2_ref, base, xs_ref, 0, sem).wait()
        return carry

    lax.fori_loop(0, tile, drain, 0)


def _scatter(seg, dest3, h2, n_rows):
    n = h2.shape[0]
    tile = SCATTER_T
    kern = functools.partial(_scatter_kernel, tile=tile)
    zero_row = jnp.zeros((1, D_MODEL), F32)
    return pl.pallas_call(
        kern,
        grid=(n // tile,),
        in_specs=[
            pl.BlockSpec(memory_space=pltpu.SMEM),
            pl.BlockSpec((1, 2, tile), lambda i: (i, 0, 0), memory_space=pltpu.SMEM),
            pl.BlockSpec(memory_space=pl.ANY),
            pl.BlockSpec(memory_space=pl.ANY),
        ],
        out_specs=pl.BlockSpec(memory_space=pl.ANY),
        out_shape=jax.ShapeDtypeStruct((n_rows, D_MODEL), F32),
        scratch_shapes=[pltpu.SemaphoreType.DMA, pltpu.SemaphoreType.DMA],
        compiler_params=pltpu.CompilerParams(dimension_semantics=("arbitrary",)),
        name="scatter",
    )(seg, dest3, h2, zero_row)


def _expert_kernel(be_ref, nv_ref, xs_ref, wg_ref, wu_ref, wd_ref, y_ref):
    i = pl.program_id(0)

    @pl.when(i < nv_ref[0])
    def _():
        x = xs_ref[...].astype(BF16)
        g = _dot(x, wg_ref[0])
        u = _dot(x, wu_ref[0])
        hid = (_silu(g) * u).astype(BF16)
        y_ref[...] = _dot(hid, wd_ref[0])

    @pl.when(i >= nv_ref[0])
    def _():
        y_ref[...] = jnp.zeros_like(y_ref)


def _experts(block_expert, n_valid, xs, w_gate, w_up, w_down):
    n_rows = xs.shape[0]
    blk = MOE_BLOCK
    grid_spec = pltpu.PrefetchScalarGridSpec(
        num_scalar_prefetch=2,
        grid=(n_rows // blk,),
        in_specs=[
            pl.BlockSpec((blk, D_MODEL), lambda i, be, nv: (jnp.minimum(i, nv[0] - 1), 0)),
            pl.BlockSpec((1, D_MODEL, D_EXPERT), lambda i, be, nv: (be[i], 0, 0)),
            pl.BlockSpec((1, D_MODEL, D_EXPERT), lambda i, be, nv: (be[i], 0, 0)),
            pl.BlockSpec((1, D_EXPERT, D_MODEL), lambda i, be, nv: (be[i], 0, 0)),
        ],
        out_specs=pl.BlockSpec((blk, D_MODEL), lambda i, be, nv: (i, 0)),
    )
    return pl.pallas_call(
        _expert_kernel,
        grid_spec=grid_spec,
        out_shape=jax.ShapeDtypeStruct((n_rows, D_MODEL), F32),
        compiler_params=pltpu.CompilerParams(
            dimension_semantics=("arbitrary",), vmem_limit_bytes=VMEM_LIMIT),
        name="experts",
    )(block_expert, n_valid, xs, w_gate, w_up, w_down)


def _combine_kernel(dest_ref, x1_ref, rw_ref, nw_ref, y_ref, out_ref, ya_scr, yb_scr, sem, *, tile):
    def issue(t, carry):
        _row_copy(y_ref, dest_ref[0, 0, t], ya_scr, t, sem).start()
        _row_copy(y_ref, dest_ref[0, 1, t], yb_scr, t, sem).start()
        return carry

    def drain(t, carry):
        _row_copy(y_ref, 0, ya_scr, 0, sem).wait()
        _row_copy(y_ref, 0, yb_scr, 0, sem).wait()
        return carry

    lax.fori_loop(0, tile, issue, 0)
    lax.fori_loop(0, tile, drain, 0)

    rw = rw_ref[...]
    moe = rw[:, 0:1] * ya_scr[...] + rw[:, 1:2] * yb_scr[...]
    x2 = x1_ref[...] + moe
    out_ref[...] = x2 * lax.rsqrt(jnp.mean(x2 * x2, axis=-1, keepdims=True) + EPS) * nw_ref[...]


def _combine(dest3, x1, rw, norm_w, y):
    n = x1.shape[0]
    tile = COMBINE_T
    kern = functools.partial(_combine_kernel, tile=tile)
    return pl.pallas_call(
        kern,
        grid=(n // tile,),
        in_specs=[
            pl.BlockSpec((1, 2, tile), lambda i: (i, 0, 0), memory_space=pltpu.SMEM),
            pl.BlockSpec((tile, D_MODEL), lambda i: (i, 0)),
            pl.BlockSpec((tile, LANES), lambda i: (i, 0)),
            pl.BlockSpec((1, D_MODEL), lambda i: (0, 0)),
            pl.BlockSpec(memory_space=pl.ANY),
        ],
        out_specs=pl.BlockSpec((tile, D_MODEL), lambda i: (i, 0)),
        out_shape=jax.ShapeDtypeStruct((n, D_MODEL), F32),
        scratch_shapes=[
            pltpu.VMEM((tile, D_MODEL), F32), pltpu.VMEM((tile, D_MODEL), F32),
            pltpu.SemaphoreType.DMA,
        ],
        compiler_params=pltpu.CompilerParams(
            dimension_semantics=("arbitrary",), vmem_limit_bytes=VMEM_LIMIT),
        name="combine",
    )(dest3, x1, rw, norm_w, y)


def _pad_cols(w, width):
    return jnp.pad(w, ((0, 0), (0, width - w.shape[1])))


def _token_mixer_and_moe(x, norm1_w, w_in, w2_f, b_f, w2_b, b_b, gla_norm_w, conv_w, a_log_f, dt_bias_f,
                         a_log_b, dt_bias_b, gdn_norm_w, w_out, norm2_w, w_group, w_router, w_gate, w_up,
                         w_down, out_norm_w):
    batch, seq, d = x.shape
    n = batch * seq
    x2 = x.reshape(n, d)

    w_main = jnp.concatenate([w_in[:, :3072], w_in[:, 3104:7200], w_in[:, 7232:]], axis=1).astype(BF16)
    w_small = _pad_cols(jnp.concatenate([w_in[:, 3072:3104], w_in[:, 7200:7232]], axis=1), LANES)
    ws_hi, ws_lo = _split2(w_small)
    main, small = _inproj(x2, norm1_w.reshape(1, d), w_main, ws_hi, ws_lo)

    w2f_pad = jnp.zeros((LANES, GLA_HEADS * GLA_DK), F32).at[0:GLA_GATE_RANK].set(w2_f)
    w2b_pad = jnp.zeros((LANES, GLA_HEADS * GLA_DK), F32).at[GLA_GATE_RANK:2 * GLA_GATE_RANK].set(w2_b)
    ga = _gla(main, small, w2f_pad, w2b_pad, b_f.reshape(1, -1), b_b.reshape(1, -1),
              gla_norm_w.reshape(1, -1), batch, seq)

    gates = jnp.stack([a_log_f, dt_bias_f, a_log_b, dt_bias_b]).astype(F32)
    gb = _gdn(main, small, gates, conv_w, gdn_norm_w.reshape(1, -1), batch, seq)

    w_route = _pad_cols(jnp.concatenate([w_router, w_group], axis=1), LANES)
    wr_hi, wr_lo = _split2(w_route)
    x1, h2, ri, rw, counts = _outproj(ga, gb, x2, w_out.astype(BF16), norm2_w.reshape(1, d), wr_hi, wr_lo)

    blk = MOE_BLOCK
    cnt = counts[0, :N_EXPERTS]
    padded = (cnt + blk - 1) // blk * blk
    ends = jnp.cumsum(padded)
    pstart = ends - padded
    n_blocks = -(-(2 * n + N_EXPERTS * (blk - 1)) // blk)
    n_rows = n_blocks * blk
    e01 = ri[:, 0:2]
    dest = pstart[e01] + ri[:, 2:4]
    block_expert = jnp.minimum(
        jnp.searchsorted(ends, jnp.arange(n_blocks, dtype=jnp.int32) * blk, side='right'),
        N_EXPERTS - 1).astype(jnp.int32)
    n_valid = (ends[-1:] // blk).astype(jnp.int32)
    seg = jnp.stack([jnp.append(pstart + cnt, ends[-1]), jnp.append(ends, n_rows)]).astype(jnp.int32)

    def tiles(t):
        return dest.reshape(n // t, t, 2).transpose(0, 2, 1)

    xs = _scatter(seg, tiles(SCATTER_T), h2, n_rows)
    y = _experts(block_expert, n_valid, xs, w_gate.astype(BF16), w_up.astype(BF16), w_down.astype(BF16))
    out = _combine(tiles(COMBINE_T), x1, rw, out_norm_w.reshape(1, d), y)
    return out.reshape(batch, seq, d)


def kernel(x, norm1_w, w_in, gla_gate_w2_fwd, gla_gate_b_fwd, gla_gate_w2_bwd, gla_gate_b_bwd, gla_norm_w,
           gdn_conv_w, gdn_a_log_fwd, gdn_dt_bias_fwd, gdn_a_log_bwd, gdn_dt_bias_bwd, gdn_norm_w, w_out,
           norm2_w, moe_w_group, moe_w_router, moe_w_gate, moe_w_up, moe_w_down, norm_f_w):
    assert norm1_w.shape[0] == 1, "single-layer block"
    return _token_mixer_and_moe(
        x, norm1_w[0], w_in[0], gla_gate_w2_fwd[0], gla_gate_b_fwd[0], gla_gate_w2_bwd[0], gla_gate_b_bwd[0],
        gla_norm_w[0], gdn_conv_w[0], gdn_a_log_fwd[0], gdn_dt_bias_fwd[0], gdn_a_log_bwd[0],
        gdn_dt_bias_bwd[0], gdn_norm_w[0], w_out[0], norm2_w[0], moe_w_group[0], moe_w_router[0],
        moe_w_gate[0], moe_w_up[0], moe_w_down[0], norm_f_w)
```

```python
import functools

import jax
import jax.numpy as jnp
from jax import lax
from jax.experimental import pallas as pl
from jax.experimental.pallas import tpu as pltpu

F32 = jnp.float32
BF16 = jnp.bfloat16

D_MODEL = 1024
GLA_HEADS = 4
GLA_DK = 128
GLA_DV = 256
GLA_GATE_RANK = 16
GLA_GATE_TAU = 16.0
GLA_CHUNK = 64
GLA_GROUP = 4
GDN_HEADS = 8
GDN_DK = 128
GDN_DV = 128
GDN_CONV = 5
GDN_CHUNK = 64
GDN_PREP_GROUP = 8
N_GROUPS = 4
EXPERTS_PER_GROUP = 8
N_EXPERTS = N_GROUPS * EXPERTS_PER_GROUP
D_EXPERT = 256
EPS = 1e-6

LANES = 128
SUBLANES = 8
VMEM_LIMIT = 48 * 1024 * 1024

COL_GQ, COL_GK, COL_GV, COL_GR = 0, 512, 1024, 2048
COL_DQ, COL_DK, COL_DV, COL_DZ = 3072, 4096, 5120, 6144
COL_MA, COL_MB = 7168, 8192
D_MAIN = 9216
SMALL_AF, SMALL_AB, SMALL_BF, SMALL_BB = 32, 40, 48, 56
ROUTE_GROUP_LANE = 32

MOE_BLOCK = 128
INPROJ_TM, INPROJ_TN = 1024, 1024
OUTPROJ_TM = 512
SCATTER_T = 512
COMBINE_T = 256
CONV_ROWS = 256
NEG_INF = float("-inf")


def _dot(a, b):
    return jnp.dot(a, b, preferred_element_type=F32)


def _dot_nt(a, b):
    return lax.dot_general(a, b, (((1,), (1,)), ((), ())), preferred_element_type=F32)


def _dot_tn(a, b):
    return lax.dot_general(a, b, (((0,), (0,)), ((), ())), preferred_element_type=F32)


def _split2(x):
    hi = x.astype(BF16)
    lo = (x - hi.astype(F32)).astype(BF16)
    return hi, lo


def _split3(x):
    hi = x.astype(BF16)
    r = x - hi.astype(F32)
    mid = r.astype(BF16)
    lo = (r - mid.astype(F32)).astype(BF16)
    return hi, mid, lo


def _dot_exact_rhs(x, m_bf16):
    hi, mid, lo = _split3(x)
    return _dot(hi, m_bf16) + _dot(mid, m_bf16) + _dot(lo, m_bf16)


def _dot_exact_lhs(m_bf16, x):
    hi, mid, lo = _split3(x)
    return _dot(m_bf16, hi) + _dot(m_bf16, mid) + _dot(m_bf16, lo)


def _dot_lhs2(m_bf16, x):
    hi, lo = _split2(x)
    return _dot(m_bf16, hi) + _dot(m_bf16, lo)


def _dot3(a, b):
    ah, al = _split2(a)
    bh, bl = _split2(b)
    return _dot(ah, bh) + _dot(al, bh) + _dot(ah, bl)


def _each(fn, *lists):
    return [fn(*args) for args in zip(*lists)]


def _sigmoid(x):
    return 1.0 / (1.0 + jnp.exp(-x))


def _silu(x):
    return x * _sigmoid(x)


def _softplus(x):
    return jnp.maximum(x, 0.0) + jnp.log(1.0 + jnp.exp(-jnp.abs(x)))


def _log_sigmoid(x):
    return jnp.minimum(x, 0.0) - jnp.log(1.0 + jnp.exp(-jnp.abs(x)))


def _iota2(shape, dim):
    return lax.broadcasted_iota(jnp.int32, shape, dim)


def _inproj_kernel(x_ref, nw_ref, w_ref, wsh_ref, wsl_ref, main_ref, small_ref, h_scr):
    @pl.when(pl.program_id(1) == 0)
    def _():
        x = x_ref[...]
        h = x * lax.rsqrt(jnp.mean(x * x, axis=-1, keepdims=True) + EPS) * nw_ref[...]
        hh, hl = _split2(h)
        h_scr[...] = hh
        small_ref[...] = _dot(hh, wsh_ref[...]) + _dot(hl, wsh_ref[...]) + _dot(hh, wsl_ref[...])

    main_ref[...] = _dot(h_scr[...], w_ref[...]).astype(BF16)


def _inproj(x2, norm_w, w_main, ws_hi, ws_lo):
    n = x2.shape[0]
    tm, tn = INPROJ_TM, INPROJ_TN
    return pl.pallas_call(
        _inproj_kernel,
        grid=(n // tm, D_MAIN // tn),
        in_specs=[
            pl.BlockSpec((tm, D_MODEL), lambda i, j: (i, 0)),
            pl.BlockSpec((1, D_MODEL), lambda i, j: (0, 0)),
            pl.BlockSpec((D_MODEL, tn), lambda i, j: (0, j)),
            pl.BlockSpec((D_MODEL, LANES), lambda i, j: (0, 0)),
            pl.BlockSpec((D_MODEL, LANES), lambda i, j: (0, 0)),
        ],
        out_specs=[
            pl.BlockSpec((tm, tn), lambda i, j: (i, j)),
            pl.BlockSpec((tm, LANES), lambda i, j: (i, 0)),
        ],
        out_shape=[
            jax.ShapeDtypeStruct((n, D_MAIN), BF16),
            jax.ShapeDtypeStruct((n, LANES), F32),
        ],
        scratch_shapes=[pltpu.VMEM((tm, D_MODEL), BF16)],
        compiler_params=pltpu.CompilerParams(
            dimension_semantics=("arbitrary", "arbitrary"), vmem_limit_bytes=VMEM_LIMIT),
        name="inproj",
    )(x2, norm_w, w_main, ws_hi, ws_lo)


def _gla_kernel(q_ref, k_ref, v_ref, gr_ref, ma_ref, small_ref, w2f_ref, w2b_ref, bf_ref, bb_ref,
                nw_ref, out_ref, laf_scr, lab_scr, o_scr, stf_scr, stb_scr, *, seq, chunk):
    c = chunk
    n = seq // c
    scale = GLA_DK ** -0.5

    sm = small_ref[...]
    laf_scr[...] = _log_sigmoid(_dot3(sm, w2f_ref[...]) + bf_ref[...]) * (1.0 / GLA_GATE_TAU)
    lab_scr[...] = _log_sigmoid(_dot3(sm, w2b_ref[...]) + bb_ref[...]) * (1.0 / GLA_GATE_TAU)
    stf_scr[...] = jnp.zeros_like(stf_scr)
    stb_scr[...] = jnp.zeros_like(stb_scr)

    row = _iota2((c, c), 0)
    col = _iota2((c, c), 1)
    low = row >= col
    upp = row <= col
    low_m = jnp.where(low, 1.0, 0.0).astype(BF16)
    upp_m = jnp.where(upp, 1.0, 0.0).astype(BF16)

    g = GLA_GROUP

    def finish(rows, o):
        y = o * lax.rsqrt(jnp.mean(o * o, axis=-1, keepdims=True) + EPS) * nw_ref[...]
        y = y * _silu(gr_ref[rows, :].astype(F32))
        y = y * _sigmoid(ma_ref[rows, :].astype(F32))
        out_ref[rows, :] = y.astype(BF16)

    def group(gi, second_touch):
        ids = [gi * g + j for j in range(g)] + [n - 1 - gi * g - j for j in range(g)]
        rows = [pl.ds(pl.multiple_of(i * c, c), c) for i in ids]
        la = [laf_scr[r, :] for r in rows[:g]] + [lab_scr[r, :] for r in rows[g:]]
        csum = [low_m] * g + [upp_m] * g
        mask = [low] * g + [upp] * g
        tot_row = [c - 1] * g + [0] * g
        qf = [q_ref[r, :].astype(F32) * scale for r in rows]
        kf = [k_ref[r, :].astype(F32) for r in rows]
        vc = [v_ref[r, :] for r in rows]

        cum = _each(_dot_lhs2, csum, la)
        tot = _each(lambda x, r: x[r:r + 1, :], cum, tot_row)
        q_dec = _each(lambda q, x: (q * jnp.exp(x)).astype(BF16), qf, cum)
        k_inv = _each(lambda k, x: (k * jnp.exp(-x)).astype(BF16), kf, cum)
        k_tail = _each(lambda k, t, x: (k * jnp.exp(t - x)).astype(BF16), kf, tot, cum)
        s = _each(lambda m, q, k: jnp.where(m, _dot_nt(q, k), 0.0).astype(BF16), mask, q_dec, k_inv)
        o = _each(_dot, s, vc)
        kv = _each(_dot_tn, vc, k_tail)
        dec = _each(jnp.exp, tot)

        for st_scr, probs in ((stf_scr, range(g)), (stb_scr, range(g, 2 * g))):
            st = st_scr[...]
            for p in probs:
                o[p] = o[p] + _dot_nt(q_dec[p], st.astype(BF16))
                st = dec[p] * st + kv[p]
            st_scr[...] = st

        for r, o_p in zip(rows, o):
            if second_touch:
                finish(r, o_scr[r, :] + o_p)
            else:
                o_scr[r, :] = o_p

    def first_half(gi, carry):
        group(gi, False)
        return carry

    def second_half(gi, carry):
        group(gi, True)
        return carry

    n_groups = n // g
    lax.fori_loop(0, n_groups // 2, first_half, 0)
    lax.fori_loop(n_groups // 2, n_groups, second_half, 0)


def _gla(main, small, w2f_pad, w2b_pad, b_f, b_b, norm_w, batch, seq):
    n = batch * seq
    h = GLA_HEADS
    kern = functools.partial(_gla_kernel, seq=seq, chunk=GLA_CHUNK)
    qk_blk = lambda off: pl.BlockSpec((seq, GLA_DK), lambda b, hh, off=off: (b, off // GLA_DK + hh))
    v_blk = lambda off: pl.BlockSpec((seq, GLA_DV), lambda b, hh, off=off: (b, off // GLA_DV + hh))
    return pl.pallas_call(
        kern,
        grid=(batch, h),
        in_specs=[
            qk_blk(COL_GQ), qk_blk(COL_GK), v_blk(COL_GV), v_blk(COL_GR), v_blk(COL_MA),
            pl.BlockSpec((seq, LANES), lambda b, hh: (b, 0)),
            pl.BlockSpec((LANES, GLA_DK), lambda b, hh: (0, hh)),
            pl.BlockSpec((LANES, GLA_DK), lambda b, hh: (0, hh)),
            pl.BlockSpec((1, GLA_DK), lambda b, hh: (0, hh)),
            pl.BlockSpec((1, GLA_DK), lambda b, hh: (0, hh)),
            pl.BlockSpec((1, GLA_DV), lambda b, hh: (0, 0)),
        ],
        out_specs=pl.BlockSpec((seq, GLA_DV), lambda b, hh: (b, hh)),
        out_shape=jax.ShapeDtypeStruct((n, D_MODEL), BF16),
        scratch_shapes=[
            pltpu.VMEM((seq, GLA_DK), F32), pltpu.VMEM((seq, GLA_DK), F32),
            pltpu.VMEM((seq, GLA_DV), F32),
            pltpu.VMEM((GLA_DV, GLA_DK), F32), pltpu.VMEM((GLA_DV, GLA_DK), F32),
        ],
        compiler_params=pltpu.CompilerParams(
            dimension_semantics=("arbitrary", "arbitrary"), vmem_limit_bytes=VMEM_LIMIT),
        name="gla",
    )(main, main, main, main, main, small, w2f_pad, w2b_pad, b_f, b_b, norm_w)


TRI_BLOCK = 16


def _mm(a, b):
    return _dot(a.astype(BF16), b.astype(BF16))


def _nilpotent_inverse(a_list, eye, index):
    t_list = _each(lambda a: eye - a, a_list)
    p_list = a_list
    power = 2
    while power < index:
        p_list = _each(lambda p: _mm(p, p), p_list)
        t_list = _each(lambda t, p: t + _mm(t, p), t_list, p_list)
        power *= 2
    return t_list


def _tri_inverse(a_list, eye, diag_blocks, chunk):
    ad_list = _each(lambda a: jnp.where(diag_blocks, a, 0.0), a_list)
    ao_list = _each(lambda a: jnp.where(diag_blocks, 0.0, a), a_list)
    d_list = _nilpotent_inverse(ad_list, eye, TRI_BLOCK)
    n_list = _each(_mm, d_list, ao_list)
    t_list = _nilpotent_inverse(n_list, eye, chunk // TRI_BLOCK)
    return _each(_mm, t_list, d_list)


def _gdn_kernel(gate_ref, q_ref, k_ref, v_ref, z_ref, mb_ref, small_ref, cwq_ref, cwk_ref, cwv_ref,
                nw_ref, out_ref, pad_scr, qs_scr, ks_scr, vs_scr, gf_scr, gb_scr, btf_scr, btb_scr,
                o_scr, nmat_scr, bmat_scr, qp_scr, cd_scr, sf_scr, sb_scr, *, seq, chunk):
    c = chunk
    n = seq // c
    hh = pl.program_id(1)
    scale = GDN_DK ** -0.5

    zeros8 = jnp.zeros((SUBLANES, LANES), F32)
    pad_scr[0:SUBLANES, :] = zeros8
    pad_scr[seq + SUBLANES:seq + 2 * SUBLANES, :] = zeros8
    half = GDN_CONV // 2

    def conv_into(src_ref, cw_ref, dst_ref, normalise, mult):
        pad_scr[SUBLANES:seq + SUBLANES, :] = src_ref[...].astype(F32)
        w = cw_ref[...]

        def body(i, carry):
            r0 = pl.multiple_of(i * CONV_ROWS, CONV_ROWS)
            win = pad_scr[pl.ds(r0, CONV_ROWS + 2 * SUBLANES), :]
            acc = jnp.zeros((CONV_ROWS, LANES), F32)
            for j in range(GDN_CONV):
                o = SUBLANES - half + j
                acc = acc + win[o:o + CONV_ROWS, :] * w[j:j + 1, :]
            y = _silu(acc)
            if normalise:
                y = y * lax.rsqrt(jnp.sum(y * y, axis=-1, keepdims=True) + EPS) * mult
            dst_ref[pl.ds(r0, CONV_ROWS), :] = y.astype(BF16)
            return carry

        lax.fori_loop(0, seq // CONV_ROWS, body, 0)

    conv_into(q_ref, cwq_ref, qs_scr, True, scale)
    conv_into(k_ref, cwk_ref, ks_scr, True, 1.0)
    conv_into(v_ref, cwv_ref, vs_scr, False, 1.0)

    sm = small_ref[...]
    sel_row = _iota2((LANES, LANES), 0)

    def lane_broadcast(lane):
        sel = jnp.where(sel_row == lane, 1.0, 0.0).astype(BF16)
        return _dot_exact_rhs(sm, sel)

    def log_decay(a_log, dt_bias, lane):
        rate = jnp.exp(jnp.full((1, LANES), a_log, F32))
        return -rate * _softplus(lane_broadcast(lane) + dt_bias)

    gf_scr[...] = log_decay(gate_ref[0, hh], gate_ref[1, hh], SMALL_AF + hh)
    gb_scr[...] = log_decay(gate_ref[2, hh], gate_ref[3, hh], SMALL_AB + hh)
    btf_scr[...] = _sigmoid(lane_broadcast(SMALL_BF + hh))
    btb_scr[...] = _sigmoid(lane_broadcast(SMALL_BB + hh))
    sf_scr[...] = jnp.zeros_like(sf_scr)
    sb_scr[...] = jnp.zeros_like(sb_scr)

    row = _iota2((c, c), 0)
    col = _iota2((c, c), 1)
    eye = jnp.where(row == col, 1.0, 0.0).astype(F32)
    low, slow = row >= col, row > col
    upp, supp = row <= col, row < col
    low_m = jnp.where(low, 1.0, 0.0).astype(BF16)
    upp_m = jnp.where(upp, 1.0, 0.0).astype(BF16)

    tri_shift = TRI_BLOCK.bit_length() - 1
    diag_blocks = jnp.right_shift(row, tri_shift) == jnp.right_shift(col, tri_shift)

    def prep_group(gi, carry):
        chunk_ids = [gi * GDN_PREP_GROUP + j for j in range(GDN_PREP_GROUP)]
        rows = [pl.ds(pl.multiple_of(i * c, c), c) for i in chunk_ids]
        qc = [qs_scr[r, :] for r in rows]
        kc = [ks_scr[r, :] for r in rows]
        vc = [vs_scr[r, :] for r in rows]
        gl = [ref[r, :] for r in rows for ref in (gf_scr, gb_scr)]
        bt = [ref[r, :] for r in rows for ref in (btf_scr, btb_scr)]
        csum = [low_m, upp_m] * GDN_PREP_GROUP
        incl = [low, upp] * GDN_PREP_GROUP
        strict = [slow, supp] * GDN_PREP_GROUP
        tot_row = [c - 1, 0] * GDN_PREP_GROUP

        def both(per_chunk):
            return [x for x in per_chunk for _ in range(2)]

        kk = both(_each(_dot_nt, kc, kc))
        qk = both(_each(_dot_nt, qc, kc))
        qf = both(_each(lambda x: x.astype(F32), qc))
        kf = both(_each(lambda x: x.astype(F32), kc))
        vf = both(_each(lambda x: x.astype(F32), vc))

        gc = _each(_dot_lhs2, csum, gl)
        tot = _each(lambda g, r: g[r:r + 1, :], gc, tot_row)
        e = _each(lambda m, g, st: jnp.exp(_dot_lhs2(m, jnp.where(st, g[:, :c], 0.0))), csum, gl, strict)
        a = _each(lambda kk_, b, e_, st: kk_ * b[:, :c] * jnp.where(st, e_, 0.0), kk, bt, e, strict)
        t_inv = _tri_inverse(a, eye, diag_blocks, c)
        egc = _each(jnp.exp, gc)
        u = _each(lambda t, v, b: _mm(t, v * b), t_inv, vf, bt)
        w = _each(lambda t, k, b, eg: _mm(t, k * b * eg), t_inv, kf, bt, egc)
        wu = _each(lambda w_, u_: jnp.concatenate([w_.astype(BF16), u_.astype(BF16)], axis=1), w, u)
        attn = _each(lambda qk_, e_, inc: (qk_ * jnp.where(inc, e_, 0.0)).astype(BF16), qk, e, incl)
        k_tail = _each(lambda k, t, g: (k * jnp.exp(t - g)).astype(BF16), kf, tot, gc)
        kwu = _each(_dot_tn, k_tail, wu)
        awu = _each(_dot, attn, wu)

        for p in range(2 * GDN_PREP_GROUP):
            slot = chunk_ids[p // 2] + (p % 2) * n
            nmat_scr[slot] = kwu[p][:, :GDN_DK].astype(BF16)
            bmat_scr[slot] = kwu[p][:, GDN_DK:]
            qp_scr[slot] = (qf[p] * egc[p] - awu[p][:, :GDN_DK]).astype(BF16)
            cd_scr[slot] = jnp.broadcast_to(jnp.exp(tot[p]), (SUBLANES, LANES))
        for j, r in enumerate(rows):
            o_scr[r, :] = awu[2 * j][:, GDN_DK:] + awu[2 * j + 1][:, GDN_DK:]
        return carry

    lax.fori_loop(0, n // GDN_PREP_GROUP, prep_group, 0)

    def scan_step(i, carry):
        j = n - 1 - i
        slots = [i, n + j]
        rows = [pl.ds(pl.multiple_of(i * c, c), c), pl.ds(pl.multiple_of(j * c, c), c)]
        states = [sf_scr, sb_scr]
        s = [ref[...] for ref in states]
        s_b = _each(lambda x: x.astype(BF16), s)
        ns = _each(lambda sl, x: _dot(nmat_scr[sl], x), slots, s_b)
        qs = _each(lambda sl, x: _dot(qp_scr[sl], x), slots, s_b)
        for ref, sl, s_, ns_ in zip(states, slots, s, ns):
            ref[...] = cd_scr[sl][0:1, :] * s_ + (bmat_scr[sl] - ns_)
        for r, q in zip(rows, qs):
            o_scr[r, :] += q
        return carry

    lax.fori_loop(0, n, scan_step, 0, unroll=2)

    def finish(i, carry):
        r0 = pl.multiple_of(i * CONV_ROWS, CONV_ROWS)
        o = o_scr[pl.ds(r0, CONV_ROWS), :]
        y = o * lax.rsqrt(jnp.mean(o * o, axis=-1, keepdims=True) + EPS) * nw_ref[...]
        y = y * _silu(z_ref[pl.ds(r0, CONV_ROWS), :].astype(F32))
        y = y * _sigmoid(mb_ref[pl.ds(r0, CONV_ROWS), :].astype(F32))
        out_ref[pl.ds(r0, CONV_ROWS), :] = y.astype(BF16)
        return carry

    lax.fori_loop(0, seq // CONV_ROWS, finish, 0)


def _gdn(main, small, gates, conv_w, norm_w, batch, seq):
    n = batch * seq
    kern = functools.partial(_gdn_kernel, seq=seq, chunk=GDN_CHUNK)
    n_chunks = seq // GDN_CHUNK
    blk = lambda off: pl.BlockSpec((seq, LANES), lambda b, hh, off=off: (b, off // LANES + hh))
    cw = lambda part: pl.BlockSpec((GDN_CONV, LANES), lambda b, hh, part=part: (0, part * GDN_HEADS + hh))
    seq_f32 = lambda: pltpu.VMEM((seq, LANES), F32)
    seq_bf16 = lambda: pltpu.VMEM((seq, LANES), BF16)
    return pl.pallas_call(
        kern,
        grid=(batch, GDN_HEADS),
        in_specs=[
            pl.BlockSpec(memory_space=pltpu.SMEM),
            blk(COL_DQ), blk(COL_DK), blk(COL_DV), blk(COL_DZ), blk(COL_MB),
            pl.BlockSpec((seq, LANES), lambda b, hh: (b, 0)),
            cw(0), cw(1), cw(2),
            pl.BlockSpec((1, GDN_DV), lambda b, hh: (0, 0)),
        ],
        out_specs=pl.BlockSpec((seq, GDN_DV), lambda b, hh: (b, hh)),
        out_shape=jax.ShapeDtypeStruct((n, D_MODEL), BF16),
        scratch_shapes=[
            pltpu.VMEM((seq + 2 * SUBLANES, LANES), F32),
            seq_bf16(), seq_bf16(), seq_bf16(),
            seq_f32(), seq_f32(), seq_f32(), seq_f32(),
            seq_f32(),
            pltpu.VMEM((2 * n_chunks, GDN_DK, GDN_DK), BF16),
            pltpu.VMEM((2 * n_chunks, GDN_DK, GDN_DV), F32),
            pltpu.VMEM((2 * n_chunks, GDN_CHUNK, GDN_DK), BF16),
            pltpu.VMEM((2 * n_chunks, SUBLANES, LANES), F32),
            pltpu.VMEM((GDN_DK, GDN_DV), F32), pltpu.VMEM((GDN_DK, GDN_DV), F32),
        ],
        compiler_params=pltpu.CompilerParams(
            dimension_semantics=("arbitrary", "arbitrary"), vmem_limit_bytes=VMEM_LIMIT),
        name="gdn",
    )(gates, main, main, main, main, main, small, conv_w, conv_w, conv_w, norm_w)


def _outproj_kernel(ga_ref, gb_ref, x_ref, wo_ref, nw_ref, wrh_ref, wrl_ref,
                    x1_ref, h2_ref, ri_ref, rw_ref, cnt_ref, carry_scr, *, tm):
    @pl.when(pl.program_id(0) == 0)
    def _():
        carry_scr[...] = jnp.zeros_like(carry_scr)

    mixed = (ga_ref[...].astype(F32) + gb_ref[...].astype(F32)).astype(BF16)
    x1 = x_ref[...] + _dot(mixed, wo_ref[...])
    x1_ref[...] = x1
    h2 = x1 * lax.rsqrt(jnp.mean(x1 * x1, axis=-1, keepdims=True) + EPS) * nw_ref[...]
    h2_ref[...] = h2
    hh, hl = _split2(h2)
    lg = _dot(hh, wrh_ref[...]) + _dot(hl, wrh_ref[...]) + _dot(hh, wrl_ref[...])

    lane_i = _iota2((tm, LANES), 1)
    lane = lane_i.astype(F32)

    def first_argmax(vals):
        m = jnp.max(vals, axis=-1, keepdims=True)
        idx = jnp.min(jnp.where(vals == m, lane, float(LANES)), axis=-1, keepdims=True)
        return m, idx

    is_g = (lane_i >= ROUTE_GROUP_LANE) & (lane_i < ROUTE_GROUP_LANE + N_GROUPS)
    gmax, glane = first_argmax(jnp.where(is_g, lg, NEG_INF))
    gidx = glane - float(ROUTE_GROUP_LANE)
    gsum = jnp.sum(jnp.where(is_g, jnp.exp(lg - gmax), 0.0), axis=-1, keepdims=True)
    group_w = 1.0 / gsum
    lane_group = jnp.right_shift(lane_i, EXPERTS_PER_GROUP.bit_length() - 1).astype(F32)
    in_grp = (lane_i < N_EXPERTS) & (lane_group == gidx)
    el = jnp.where(in_grp, lg, NEG_INF)
    m1, e0 = first_argmax(el)
    m2, e1 = first_argmax(jnp.where(lane == e0, NEG_INF, el))
    r = jnp.exp(m2 - m1)
    w0 = group_w / (1.0 + r)
    w1 = group_w * r / (1.0 + r)

    pick0 = lane == e0
    pick1 = lane == e1
    onehot = jnp.where(pick0 | pick1, 1.0, 0.0)
    trow = _iota2((tm, tm), 0)
    tcol = _iota2((tm, tm), 1)
    before = jnp.where(trow > tcol, 1.0, 0.0).astype(BF16)
    cnt = _dot(before, onehot.astype(BF16)) + carry_scr[0:1, :]
    rank0 = jnp.sum(jnp.where(pick0, cnt, 0.0), axis=-1, keepdims=True)
    rank1 = jnp.sum(jnp.where(pick1, cnt, 0.0), axis=-1, keepdims=True)
    total = carry_scr[0:1, :] + jnp.sum(onehot, axis=0, keepdims=True)
    carry_scr[...] = jnp.broadcast_to(total, carry_scr.shape)
    cnt_ref[...] = jnp.broadcast_to(total, cnt_ref.shape).astype(jnp.int32)

    ri = jnp.where(lane_i == 0, e0, jnp.where(lane_i == 1, e1, 0.0))
    ri = jnp.where(lane_i == 2, rank0, jnp.where(lane_i == 3, rank1, ri))
    ri_ref[...] = ri.astype(jnp.int32)
    rw_ref[...] = jnp.where(lane_i == 0, w0, jnp.where(lane_i == 1, w1, 0.0))


def _outproj(ga, gb, x2, w_out, norm_w, wr_hi, wr_lo):
    n = x2.shape[0]
    tm = OUTPROJ_TM
    kern = functools.partial(_outproj_kernel, tm=tm)
    row_blk = lambda w: pl.BlockSpec((tm, w), lambda i: (i, 0))
    const = lambda shape: pl.BlockSpec(shape, lambda i: (0, 0))
    return pl.pallas_call(
        kern,
        grid=(n // tm,),
        in_specs=[
            row_blk(D_MODEL), row_blk(D_MODEL), row_blk(D_MODEL),
            const((D_MODEL, D_MODEL)), const((1, D_MODEL)),
            const((D_MODEL, LANES)), const((D_MODEL, LANES)),
        ],
        out_specs=[row_blk(D_MODEL), row_blk(D_MODEL), row_blk(LANES), row_blk(LANES),
                   const((SUBLANES, LANES))],
        out_shape=[
            jax.ShapeDtypeStruct((n, D_MODEL), F32),
            jax.ShapeDtypeStruct((n, D_MODEL), F32),
            jax.ShapeDtypeStruct((n, LANES), jnp.int32),
            jax.ShapeDtypeStruct((n, LANES), F32),
            jax.ShapeDtypeStruct((SUBLANES, LANES), jnp.int32),
        ],
        scratch_shapes=[pltpu.VMEM((SUBLANES, LANES), F32)],
        compiler_params=pltpu.CompilerParams(
            dimension_semantics=("arbitrary",), vmem_limit_bytes=VMEM_LIMIT),
        name="outproj",
    )(ga, gb, x2, w_out, norm_w, wr_hi, wr_lo)


def _row_copy(src_ref, src_row, dst_ref, dst_row, sem):
    return pltpu.make_async_copy(src_ref.at[pl.ds(src_row, 1)], dst_ref.at[pl.ds(dst_row, 1)], sem)


def _scatter_kernel(seg_ref, dest_ref, h2_ref, xs_ref, zero_scr, sem, zsem, *, tile):
    i = pl.program_id(0)

    def issue(t, carry):
        _row_copy(h2_ref, t, xs_ref, dest_ref[0, 0, t], sem).start()
        _row_copy(h2_ref, t, xs_ref, dest_ref[0, 1, t], sem).start()
        return carry

    lax.fori_loop(0, tile, issue, 0)

    @pl.when(i == 0)
    def _():
        zero_scr[...] = jnp.zeros_like(zero_scr)

        def per_expert(e, carry):
            lo, hi = seg_ref[0, e], seg_ref[1, e]

            def start(r, c2):
                _row_copy(zero_scr, 0, xs_ref, r, zsem).start()
                return c2

            def wait(r, c2):
                _row_copy(zero_scr, 0, xs_ref, r, zsem).wait()
                return c2

            lax.fori_loop(lo, hi, start, 0)
            lax.fori_loop(lo, hi, wait, 0)
            return carry

        lax.fori_loop(0, N_EXPERTS + 1, per_expert, 0)

    def drain(t, carry):
        _row_copy(h2_ref, 0, xs_ref, 0, sem).wait()
        _row_copy(h2_ref, 0, xs_ref, 0, sem).wait()
        return carry

    lax.fori_loop(0, tile, drain, 0)


def _scatter(seg, dest3, h2, n_rows):
    n = h2.shape[0]
    tile = SCATTER_T
    kern = functools.partial(_scatter_kernel, tile=tile)
    return pl.pallas_call(
        kern,
        grid=(n // tile,),
        in_specs=[
            pl.BlockSpec(memory_space=pltpu.SMEM),
            pl.BlockSpec((1, 2, tile), lambda i: (i, 0, 0), memory_space=pltpu.SMEM),
            pl.BlockSpec((tile, D_MODEL), lambda i: (i, 0)),
        ],
        out_specs=pl.BlockSpec(memory_space=pl.ANY),
        out_shape=jax.ShapeDtypeStruct((n_rows, D_MODEL), F32),
        scratch_shapes=[pltpu.VMEM((SUBLANES, D_MODEL), F32),
                        pltpu.SemaphoreType.DMA, pltpu.SemaphoreType.DMA],
        compiler_params=pltpu.CompilerParams(
            dimension_semantics=("arbitrary",), vmem_limit_bytes=VMEM_LIMIT),
        name="scatter",
    )(seg, dest3, h2)


def _expert_kernel(be_ref, nv_ref, xs_ref, wg_ref, wu_ref, wd_ref, y_ref):
    i = pl.program_id(0)

    @pl.when(i < nv_ref[0])
    def _():
        x = xs_ref[...].astype(BF16)
        g = _dot(x, wg_ref[0])
        u = _dot(x, wu_ref[0])
        hid = (_silu(g) * u).astype(BF16)
        y_ref[...] = _dot(hid, wd_ref[0])

    @pl.when(i >= nv_ref[0])
    def _():
        y_ref[...] = jnp.zeros_like(y_ref)


def _experts(block_expert, n_valid, xs, w_gate, w_up, w_down):
    n_rows = xs.shape[0]
    blk = MOE_BLOCK
    grid_spec = pltpu.PrefetchScalarGridSpec(
        num_scalar_prefetch=2,
        grid=(n_rows // blk,),
        in_specs=[
            pl.BlockSpec((blk, D_MODEL), lambda i, be, nv: (jnp.minimum(i, nv[0] - 1), 0)),
            pl.BlockSpec((1, D_MODEL, D_EXPERT), lambda i, be, nv: (be[i], 0, 0)),
            pl.BlockSpec((1, D_MODEL, D_EXPERT), lambda i, be, nv: (be[i], 0, 0)),
            pl.BlockSpec((1, D_EXPERT, D_MODEL), lambda i, be, nv: (be[i], 0, 0)),
        ],
        out_specs=pl.BlockSpec((blk, D_MODEL), lambda i, be, nv: (i, 0)),
    )
    return pl.pallas_call(
        _expert_kernel,
        grid_spec=grid_spec,
        out_shape=jax.ShapeDtypeStruct((n_rows, D_MODEL), F32),
        compiler_params=pltpu.CompilerParams(
            dimension_semantics=("arbitrary",), vmem_limit_bytes=VMEM_LIMIT),
        name="experts",
    )(block_expert, n_valid, xs, w_gate, w_up, w_down)


def _combine_kernel(dest_ref, x1_ref, rw_ref, nw_ref, y_ref, out_ref, ya_scr, yb_scr, sem, *, tile):
    def issue(t, carry):
        _row_copy(y_ref, dest_ref[0, 0, t], ya_scr, t, sem).start()
        _row_copy(y_ref, dest_ref[0, 1, t], yb_scr, t, sem).start()
        return carry

    def drain(t, carry):
        _row_copy(y_ref, 0, ya_scr, 0, sem).wait()
        _row_copy(y_ref, 0, yb_scr, 0, sem).wait()
        return carry

    lax.fori_loop(0, tile, issue, 0)
    lax.fori_loop(0, tile, drain, 0)

    rw = rw_ref[...]
    moe = rw[:, 0:1] * ya_scr[...] + rw[:, 1:2] * yb_scr[...]
    x2 = x1_ref[...] + moe
    out_ref[...] = x2 * lax.rsqrt(jnp.mean(x2 * x2, axis=-1, keepdims=True) + EPS) * nw_ref[...]


def _combine(dest3, x1, rw, norm_w, y):
    n = x1.shape[0]
    tile = COMBINE_T
    kern = functools.partial(_combine_kernel, tile=tile)
    return pl.pallas_call(
        kern,
        grid=(n // tile,),
        in_specs=[
            pl.BlockSpec((1, 2, tile), lambda i: (i, 0, 0), memory_space=pltpu.SMEM),
            pl.BlockSpec((tile, D_MODEL), lambda i: (i, 0)),
            pl.BlockSpec((tile, LANES), lambda i: (i, 0)),
            pl.BlockSpec((1, D_MODEL), lambda i: (0, 0)),
            pl.BlockSpec(memory_space=pl.ANY),
        ],
        out_specs=pl.BlockSpec((tile, D_MODEL), lambda i: (i, 0)),
        out_shape=jax.ShapeDtypeStruct((n, D_MODEL), F32),
        scratch_shapes=[
            pltpu.VMEM((tile, D_MODEL), F32), pltpu.VMEM((tile, D_MODEL), F32),
            pltpu.SemaphoreType.DMA,
        ],
        compiler_params=pltpu.CompilerParams(
            dimension_semantics=("arbitrary",), vmem_limit_bytes=VMEM_LIMIT),
        name="combine",
    )(dest3, x1, rw, norm_w, y)


def _pad_cols(w, width):
    return jnp.pad(w, ((0, 0), (0, width - w.shape[1])))


def _token_mixer_and_moe(x, norm1_w, w_in, w2_f, b_f, w2_b, b_b, gla_norm_w, conv_w, a_log_f, dt_bias_f,
                         a_log_b, dt_bias_b, gdn_norm_w, w_out, norm2_w, w_group, w_router, w_gate, w_up,
                         w_down, out_norm_w):
    batch, seq, d = x.shape
    n = batch * seq
    x2 = x.reshape(n, d)

    w_main = jnp.concatenate([w_in[:, :3072], w_in[:, 3104:7200], w_in[:, 7232:]], axis=1).astype(BF16)
    w_small = _pad_cols(jnp.concatenate([w_in[:, 3072:3104], w_in[:, 7200:7232]], axis=1), LANES)
    ws_hi, ws_lo = _split2(w_small)
    main, small = _inproj(x2, norm1_w.reshape(1, d), w_main, ws_hi, ws_lo)

    w2f_pad = jnp.zeros((LANES, GLA_HEADS * GLA_DK), F32).at[0:GLA_GATE_RANK].set(w2_f)
    w2b_pad = jnp.zeros((LANES, GLA_HEADS * GLA_DK), F32).at[GLA_GATE_RANK:2 * GLA_GATE_RANK].set(w2_b)
    ga = _gla(main, small, w2f_pad, w2b_pad, b_f.reshape(1, -1), b_b.reshape(1, -1),
              gla_norm_w.reshape(1, -1), batch, seq)

    gates = jnp.stack([a_log_f, dt_bias_f, a_log_b, dt_bias_b]).astype(F32)
    gb = _gdn(main, small, gates, conv_w, gdn_norm_w.reshape(1, -1), batch, seq)

    w_route = _pad_cols(jnp.concatenate([w_router, w_group], axis=1), LANES)
    wr_hi, wr_lo = _split2(w_route)
    x1, h2, ri, rw, counts = _outproj(ga, gb, x2, w_out.astype(BF16), norm2_w.reshape(1, d), wr_hi, wr_lo)

    blk = MOE_BLOCK
    cnt = counts[0, :N_EXPERTS]
    padded = (cnt + blk - 1) // blk * blk
    ends = jnp.cumsum(padded)
    pstart = ends - padded
    n_blocks = -(-(2 * n + N_EXPERTS * (blk - 1)) // blk)
    n_rows = n_blocks * blk
    e01 = ri[:, 0:2]
    dest = pstart[e01] + ri[:, 2:4]
    block_row = jnp.arange(n_blocks, dtype=jnp.int32) * blk
    block_expert = jnp.minimum(
        jnp.sum((ends[None, :] <= block_row[:, None]).astype(jnp.int32), axis=1), N_EXPERTS - 1)
    n_valid = (ends[-1:] // blk).astype(jnp.int32)
    seg = jnp.stack([jnp.append(pstart + cnt, ends[-1]), jnp.append(ends, n_rows)]).astype(jnp.int32)

    def tiles(t):
        return dest.reshape(n // t, t, 2).transpose(0, 2, 1)

    xs = _scatter(seg, tiles(SCATTER_T), h2, n_rows)
    y = _experts(block_expert, n_valid, xs, w_gate.astype(BF16), w_up.astype(BF16), w_down.astype(BF16))
    out = _combine(tiles(COMBINE_T), x1, rw, out_norm_w.reshape(1, d), y)
    return out.reshape(batch, seq, d)


def kernel(x, norm1_w, w_in, gla_gate_w2_fwd, gla_gate_b_fwd, gla_gate_w2_bwd, gla_gate_b_bwd, gla_norm_w,
           gdn_conv_w, gdn_a_log_fwd, gdn_dt_bias_fwd, gdn_a_log_bwd, gdn_dt_bias_bwd, gdn_norm_w, w_out,
           norm2_w, moe_w_group, moe_w_router, moe_w_gate, moe_w_up, moe_w_down, norm_f_w):
    assert norm1_w.shape[0] == 1, "single-layer block"
    return _token_mixer_and_moe(
        x, norm1_w[0], w_in[0], gla_gate_w2_fwd[0], gla_gate_b_fwd[0], gla_gate_w2_bwd[0], gla_gate_b_bwd[0],
        gla_norm_w[0], gdn_conv_w[0], gdn_a_log_fwd[0], gdn_dt_bias_fwd[0], gdn_a_log_bwd[0],
        gdn_dt_bias_bwd[0], gdn_norm_w[0], w_out[0], norm2_w[0], moe_w_group[0], moe_w_router[0],
        moe_w_gate[0], moe_w_up[0], moe_w_down[0], norm_f_w)
```

```python
import functools

import jax
import jax.numpy as jnp
from jax import lax
from jax.experimental import pallas as pl
from jax.experimental.pallas import tpu as pltpu

F32 = jnp.float32
BF16 = jnp.bfloat16

D_MODEL = 1024
GLA_HEADS = 4
GLA_DK = 128
GLA_DV = 256
GLA_GATE_RANK = 16
GLA_GATE_TAU = 16.0
GLA_CHUNK = 64
GLA_GROUP = 4
GDN_HEADS = 8
GDN_DK = 128
GDN_DV = 128
GDN_CONV = 5
GDN_CHUNK = 64
GDN_PREP_GROUP = 8
N_GROUPS = 4
EXPERTS_PER_GROUP = 8
N_EXPERTS = N_GROUPS * EXPERTS_PER_GROUP
D_EXPERT = 256
EPS = 1e-6

LANES = 128
SUBLANES = 8
VMEM_LIMIT = 48 * 1024 * 1024

COL_GQ, COL_GK, COL_GV, COL_GR = 0, 512, 1024, 2048
COL_DQ, COL_DK, COL_DV, COL_DZ = 3072, 4096, 5120, 6144
COL_MA, COL_MB = 7168, 8192
D_MAIN = 9216
SMALL_AF, SMALL_AB, SMALL_BF, SMALL_BB = 32, 40, 48, 56
ROUTE_GROUP_LANE = 32

MOE_BLOCK = 128
ROW_TILE = D_MODEL // LANES
assert ROW_TILE == SUBLANES
DMA_UNROLL = 8
INPROJ_TM, INPROJ_TN = 1024, 1024
OUTPROJ_TM = 512
SCATTER_T = 512
COMBINE_T = 256
CONV_ROWS = 256
NEG_INF = float("-inf")


def _dot(a, b):
    return jnp.dot(a, b, preferred_element_type=F32)


def _dot_nt(a, b):
    return lax.dot_general(a, b, (((1,), (1,)), ((), ())), preferred_element_type=F32)


def _dot_tn(a, b):
    return lax.dot_general(a, b, (((0,), (0,)), ((), ())), preferred_element_type=F32)


def _split2(x):
    hi = x.astype(BF16)
    lo = (x - hi.astype(F32)).astype(BF16)
    return hi, lo


def _split3(x):
    hi = x.astype(BF16)
    r = x - hi.astype(F32)
    mid = r.astype(BF16)
    lo = (r - mid.astype(F32)).astype(BF16)
    return hi, mid, lo


def _dot_exact_rhs(x, m_bf16):
    hi, mid, lo = _split3(x)
    return _dot(hi, m_bf16) + _dot(mid, m_bf16) + _dot(lo, m_bf16)


def _dot_exact_lhs(m_bf16, x):
    hi, mid, lo = _split3(x)
    return _dot(m_bf16, hi) + _dot(m_bf16, mid) + _dot(m_bf16, lo)


def _dot_lhs2(m_bf16, x):
    hi, lo = _split2(x)
    return _dot(m_bf16, hi) + _dot(m_bf16, lo)


def _dot3(a, b):
    ah, al = _split2(a)
    bh, bl = _split2(b)
    return _dot(ah, bh) + _dot(al, bh) + _dot(ah, bl)


def _store_row_tiles(ref, x):
    rows = x.shape[0]
    for j in range(ROW_TILE):
        ref[pl.ds(j, rows, stride=ROW_TILE), :] = x[:, j * LANES:(j + 1) * LANES]


def _load_row_tiles(ref):
    rows = ref.shape[0] // ROW_TILE
    return jnp.concatenate([ref[pl.ds(j, rows, stride=ROW_TILE), :] for j in range(ROW_TILE)], axis=1)


def _each(fn, *lists):
    return [fn(*args) for args in zip(*lists)]


def _sigmoid(x):
    return 1.0 / (1.0 + jnp.exp(-x))


def _silu(x):
    return x * _sigmoid(x)


def _softplus(x):
    return jnp.maximum(x, 0.0) + jnp.log(1.0 + jnp.exp(-jnp.abs(x)))


def _log_sigmoid(x):
    return jnp.minimum(x, 0.0) - jnp.log(1.0 + jnp.exp(-jnp.abs(x)))


def _iota2(shape, dim):
    return lax.broadcasted_iota(jnp.int32, shape, dim)


def _inproj_kernel(x_ref, nw_ref, w_ref, wsh_ref, wsl_ref, main_ref, small_ref, h_scr):
    @pl.when(pl.program_id(1) == 0)
    def _():
        x = x_ref[...]
        h = x * lax.rsqrt(jnp.mean(x * x, axis=-1, keepdims=True) + EPS) * nw_ref[...]
        hh, hl = _split2(h)
        h_scr[...] = hh
        small_ref[...] = _dot(hh, wsh_ref[...]) + _dot(hl, wsh_ref[...]) + _dot(hh, wsl_ref[...])

    main_ref[...] = _dot(h_scr[...], w_ref[...]).astype(BF16)


def _inproj(x2, norm_w, w_main, ws_hi, ws_lo):
    n = x2.shape[0]
    tm, tn = INPROJ_TM, INPROJ_TN
    return pl.pallas_call(
        _inproj_kernel,
        grid=(n // tm, D_MAIN // tn),
        in_specs=[
            pl.BlockSpec((tm, D_MODEL), lambda i, j: (i, 0)),
            pl.BlockSpec((1, D_MODEL), lambda i, j: (0, 0)),
            pl.BlockSpec((D_MODEL, tn), lambda i, j: (0, j)),
            pl.BlockSpec((D_MODEL, LANES), lambda i, j: (0, 0)),
            pl.BlockSpec((D_MODEL, LANES), lambda i, j: (0, 0)),
        ],
        out_specs=[
            pl.BlockSpec((tm, tn), lambda i, j: (i, j)),
            pl.BlockSpec((tm, LANES), lambda i, j: (i, 0)),
        ],
        out_shape=[
            jax.ShapeDtypeStruct((n, D_MAIN), BF16),
            jax.ShapeDtypeStruct((n, LANES), F32),
        ],
        scratch_shapes=[pltpu.VMEM((tm, D_MODEL), BF16)],
        compiler_params=pltpu.CompilerParams(
            dimension_semantics=("arbitrary", "arbitrary"), vmem_limit_bytes=VMEM_LIMIT),
        name="inproj",
    )(x2, norm_w, w_main, ws_hi, ws_lo)


def _gla_kernel(q_ref, k_ref, v_ref, gr_ref, ma_ref, small_ref, w2f_ref, w2b_ref, bf_ref, bb_ref,
                nw_ref, out_ref, laf_scr, lab_scr, o_scr, stf_scr, stb_scr, *, seq, chunk):
    c = chunk
    n = seq // c
    scale = GLA_DK ** -0.5

    sm = small_ref[...]
    laf_scr[...] = _log_sigmoid(_dot3(sm, w2f_ref[...]) + bf_ref[...]) * (1.0 / GLA_GATE_TAU)
    lab_scr[...] = _log_sigmoid(_dot3(sm, w2b_ref[...]) + bb_ref[...]) * (1.0 / GLA_GATE_TAU)
    stf_scr[...] = jnp.zeros_like(stf_scr)
    stb_scr[...] = jnp.zeros_like(stb_scr)

    row = _iota2((c, c), 0)
    col = _iota2((c, c), 1)
    low = row >= col
    upp = row <= col
    low_m = jnp.where(low, 1.0, 0.0).astype(BF16)
    upp_m = jnp.where(upp, 1.0, 0.0).astype(BF16)

    g = GLA_GROUP

    def finish(rows, o):
        y = o * lax.rsqrt(jnp.mean(o * o, axis=-1, keepdims=True) + EPS) * nw_ref[...]
        y = y * _silu(gr_ref[rows, :].astype(F32))
        y = y * _sigmoid(ma_ref[rows, :].astype(F32))
        out_ref[rows, :] = y.astype(BF16)

    def group(gi, second_touch):
        ids = [gi * g + j for j in range(g)] + [n - 1 - gi * g - j for j in range(g)]
        rows = [pl.ds(pl.multiple_of(i * c, c), c) for i in ids]
        la = [laf_scr[r, :] for r in rows[:g]] + [lab_scr[r, :] for r in rows[g:]]
        csum = [low_m] * g + [upp_m] * g
        mask = [low] * g + [upp] * g
        tot_row = [c - 1] * g + [0] * g
        qf = [q_ref[r, :].astype(F32) * scale for r in rows]
        kf = [k_ref[r, :].astype(F32) for r in rows]
        vc = [v_ref[r, :] for r in rows]

        cum = _each(_dot_lhs2, csum, la)
        tot = _each(lambda x, r: x[r:r + 1, :], cum, tot_row)
        q_dec = _each(lambda q, x: (q * jnp.exp(x)).astype(BF16), qf, cum)
        k_inv = _each(lambda k, x: (k * jnp.exp(-x)).astype(BF16), kf, cum)
        k_tail = _each(lambda k, t, x: (k * jnp.exp(t - x)).astype(BF16), kf, tot, cum)
        s = _each(lambda m, q, k: jnp.where(m, _dot_nt(q, k), 0.0).astype(BF16), mask, q_dec, k_inv)
        o = _each(_dot, s, vc)
        kv = _each(_dot_tn, vc, k_tail)
        dec = _each(jnp.exp, tot)

        for st_scr, probs in ((stf_scr, range(g)), (stb_scr, range(g, 2 * g))):
            st = st_scr[...]
            for p in probs:
                o[p] = o[p] + _dot_nt(q_dec[p], st.astype(BF16))
                st = dec[p] * st + kv[p]
            st_scr[...] = st

        for r, o_p in zip(rows, o):
            if second_touch:
                finish(r, o_scr[r, :] + o_p)
            else:
                o_scr[r, :] = o_p

    def first_half(gi, carry):
        group(gi, False)
        return carry

    def second_half(gi, carry):
        group(gi, True)
        return carry

    n_groups = n // g
    lax.fori_loop(0, n_groups // 2, first_half, 0)
    lax.fori_loop(n_groups // 2, n_groups, second_half, 0)


def _gla(main, small, w2f_pad, w2b_pad, b_f, b_b, norm_w, batch, seq):
    n = batch * seq
    h = GLA_HEADS
    kern = functools.partial(_gla_kernel, seq=seq, chunk=GLA_CHUNK)
    qk_blk = lambda off: pl.BlockSpec((seq, GLA_DK), lambda b, hh, off=off: (b, off // GLA_DK + hh))
    v_blk = lambda off: pl.BlockSpec((seq, GLA_DV), lambda b, hh, off=off: (b, off // GLA_DV + hh))
    return pl.pallas_call(
        kern,
        grid=(batch, h),
        in_specs=[
            qk_blk(COL_GQ), qk_blk(COL_GK), v_blk(COL_GV), v_blk(COL_GR), v_blk(COL_MA),
            pl.BlockSpec((seq, LANES), lambda b, hh: (b, 0)),
            pl.BlockSpec((LANES, GLA_DK), lambda b, hh: (0, hh)),
            pl.BlockSpec((LANES, GLA_DK), lambda b, hh: (0, hh)),
            pl.BlockSpec((1, GLA_DK), lambda b, hh: (0, hh)),
            pl.BlockSpec((1, GLA_DK), lambda b, hh: (0, hh)),
            pl.BlockSpec((1, GLA_DV), lambda b, hh: (0, 0)),
        ],
        out_specs=pl.BlockSpec((seq, GLA_DV), lambda b, hh: (b, hh)),
        out_shape=jax.ShapeDtypeStruct((n, D_MODEL), BF16),
        scratch_shapes=[
            pltpu.VMEM((seq, GLA_DK), F32), pltpu.VMEM((seq, GLA_DK), F32),
            pltpu.VMEM((seq, GLA_DV), F32),
            pltpu.VMEM((GLA_DV, GLA_DK), F32), pltpu.VMEM((GLA_DV, GLA_DK), F32),
        ],
        compiler_params=pltpu.CompilerParams(
            dimension_semantics=("arbitrary", "arbitrary"), vmem_limit_bytes=VMEM_LIMIT),
        name="gla",
    )(main, main, main, main, main, small, w2f_pad, w2b_pad, b_f, b_b, norm_w)


TRI_BLOCK = 16


def _mm(a, b):
    return _dot(a.astype(BF16), b.astype(BF16))


def _nilpotent_inverse(a_list, eye, index):
    t_list = _each(lambda a: eye - a, a_list)
    p_list = a_list
    power = 2
    while power < index:
        p_list = _each(lambda p: _mm(p, p), p_list)
        t_list = _each(lambda t, p: t + _mm(t, p), t_list, p_list)
        power *= 2
    return t_list


def _tri_inverse(a_list, eye, diag_blocks, chunk):
    ad_list = _each(lambda a: jnp.where(diag_blocks, a, 0.0), a_list)
    ao_list = _each(lambda a: jnp.where(diag_blocks, 0.0, a), a_list)
    d_list = _nilpotent_inverse(ad_list, eye, TRI_BLOCK)
    n_list = _each(_mm, d_list, ao_list)
    t_list = _nilpotent_inverse(n_list, eye, chunk // TRI_BLOCK)
    return _each(_mm, t_list, d_list)


def _gdn_kernel(gate_ref, q_ref, k_ref, v_ref, z_ref, mb_ref, small_ref, cwq_ref, cwk_ref, cwv_ref,
                nw_ref, out_ref, pad_scr, qs_scr, ks_scr, vs_scr, gf_scr, gb_scr, btf_scr, btb_scr,
                o_scr, nmat_scr, bmat_scr, qp_scr, cd_scr, sf_scr, sb_scr, *, seq, chunk):
    c = chunk
    n = seq // c
    hh = pl.program_id(1)
    scale = GDN_DK ** -0.5

    zeros8 = jnp.zeros((SUBLANES, LANES), F32)
    pad_scr[0:SUBLANES, :] = zeros8
    pad_scr[seq + SUBLANES:seq + 2 * SUBLANES, :] = zeros8
    half = GDN_CONV // 2

    def conv_into(src_ref, cw_ref, dst_ref, normalise, mult):
        pad_scr[SUBLANES:seq + SUBLANES, :] = src_ref[...].astype(F32)
        w = cw_ref[...]

        def body(i, carry):
            r0 = pl.multiple_of(i * CONV_ROWS, CONV_ROWS)
            win = pad_scr[pl.ds(r0, CONV_ROWS + 2 * SUBLANES), :]
            acc = jnp.zeros((CONV_ROWS, LANES), F32)
            for j in range(GDN_CONV):
                o = SUBLANES - half + j
                acc = acc + win[o:o + CONV_ROWS, :] * w[j:j + 1, :]
            y = _silu(acc)
            if normalise:
                y = y * lax.rsqrt(jnp.sum(y * y, axis=-1, keepdims=True) + EPS) * mult
            dst_ref[pl.ds(r0, CONV_ROWS), :] = y.astype(BF16)
            return carry

        lax.fori_loop(0, seq // CONV_ROWS, body, 0)

    conv_into(q_ref, cwq_ref, qs_scr, True, scale)
    conv_into(k_ref, cwk_ref, ks_scr, True, 1.0)
    conv_into(v_ref, cwv_ref, vs_scr, False, 1.0)

    sm = small_ref[...]
    lane = _iota2(sm.shape, 1)
    log_decay = -jnp.exp(gate_ref[0:1, :]) * _softplus(sm + gate_ref[1:2, :])
    gate_vals = jnp.where(lane < SMALL_BF, log_decay, _sigmoid(sm))
    gate_hi, gate_lo = _split2(gate_vals)
    sel_row = _iota2((LANES, LANES), 0)

    def lane_broadcast(lane_id):
        sel = jnp.where(sel_row == lane_id, 1.0, 0.0).astype(BF16)
        return _dot(gate_hi, sel) + _dot(gate_lo, sel)

    gf_scr[...] = lane_broadcast(SMALL_AF + hh)
    gb_scr[...] = lane_broadcast(SMALL_AB + hh)
    btf_scr[...] = lane_broadcast(SMALL_BF + hh)
    btb_scr[...] = lane_broadcast(SMALL_BB + hh)
    sf_scr[...] = jnp.zeros_like(sf_scr)
    sb_scr[...] = jnp.zeros_like(sb_scr)

    row = _iota2((c, c), 0)
    col = _iota2((c, c), 1)
    eye = jnp.where(row == col, 1.0, 0.0).astype(F32)
    low, slow = row >= col, row > col
    upp, supp = row <= col, row < col
    low_m = jnp.where(low, 1.0, 0.0).astype(BF16)
    upp_m = jnp.where(upp, 1.0, 0.0).astype(BF16)

    tri_shift = TRI_BLOCK.bit_length() - 1
    diag_blocks = jnp.right_shift(row, tri_shift) == jnp.right_shift(col, tri_shift)

    def prep_group(gi, carry):
        chunk_ids = [gi * GDN_PREP_GROUP + j for j in range(GDN_PREP_GROUP)]
        rows = [pl.ds(pl.multiple_of(i * c, c), c) for i in chunk_ids]
        qc = [qs_scr[r, :] for r in rows]
        kc = [ks_scr[r, :] for r in rows]
        vc = [vs_scr[r, :] for r in rows]
        gl = [ref[r, :] for r in rows for ref in (gf_scr, gb_scr)]
        bt = [ref[r, :] for r in rows for ref in (btf_scr, btb_scr)]
        csum = [low_m, upp_m] * GDN_PREP_GROUP
        incl = [low, upp] * GDN_PREP_GROUP
        strict = [slow, supp] * GDN_PREP_GROUP
        tot_row = [c - 1, 0] * GDN_PREP_GROUP

        def both(per_chunk):
            return [x for x in per_chunk for _ in range(2)]

        kk = both(_each(_dot_nt, kc, kc))
        qk = both(_each(_dot_nt, qc, kc))
        qf = both(_each(lambda x: x.astype(F32), qc))
        kf = both(_each(lambda x: x.astype(F32), kc))
        vf = both(_each(lambda x: x.astype(F32), vc))

        gc = _each(_dot_lhs2, csum, gl)
        tot = _each(lambda g, r: g[r:r + 1, :], gc, tot_row)
        e = _each(lambda m, g, st: jnp.exp(_dot_lhs2(m, jnp.where(st, g[:, :c], 0.0))), csum, gl, strict)
        a = _each(lambda kk_, b, e_, st: kk_ * b[:, :c] * jnp.where(st, e_, 0.0), kk, bt, e, strict)
        t_inv = _tri_inverse(a, eye, diag_blocks, c)
        egc = _each(jnp.exp, gc)
        u = _each(lambda t, v, b: _mm(t, v * b), t_inv, vf, bt)
        w = _each(lambda t, k, b, eg: _mm(t, k * b * eg), t_inv, kf, bt, egc)
        wu = _each(lambda w_, u_: jnp.concatenate([w_.astype(BF16), u_.astype(BF16)], axis=1), w, u)
        attn = _each(lambda qk_, e_, inc: (qk_ * jnp.where(inc, e_, 0.0)).astype(BF16), qk, e, incl)
        k_tail = _each(lambda k, t, g: (k * jnp.exp(t - g)).astype(BF16), kf, tot, gc)
        kwu = _each(_dot_tn, k_tail, wu)
        awu = _each(_dot, attn, wu)

        for p in range(2 * GDN_PREP_GROUP):
            slot = chunk_ids[p // 2] + (p % 2) * n
            nmat_scr[slot] = kwu[p][:, :GDN_DK].astype(BF16)
            bmat_scr[slot] = kwu[p][:, GDN_DK:]
            qp_scr[slot] = (qf[p] * egc[p] - awu[p][:, :GDN_DK]).astype(BF16)
            cd_scr[slot] = jnp.broadcast_to(jnp.exp(tot[p]), (SUBLANES, LANES))
        for j, r in enumerate(rows):
            o_scr[r, :] = awu[2 * j][:, GDN_DK:] + awu[2 * j + 1][:, GDN_DK:]
        return carry

    lax.fori_loop(0, n // GDN_PREP_GROUP, prep_group, 0)

    def scan_step(i, carry):
        j = n - 1 - i
        slots = [i, n + j]
        rows = [pl.ds(pl.multiple_of(i * c, c), c), pl.ds(pl.multiple_of(j * c, c), c)]
        states = [sf_scr, sb_scr]
        s = [ref[...] for ref in states]
        s_b = _each(lambda x: x.astype(BF16), s)
        ns = _each(lambda sl, x: _dot(nmat_scr[sl], x), slots, s_b)
        qs = _each(lambda sl, x: _dot(qp_scr[sl], x), slots, s_b)
        for ref, sl, s_, ns_ in zip(states, slots, s, ns):
            ref[...] = cd_scr[sl][0:1, :] * s_ + (bmat_scr[sl] - ns_)
        for r, q in zip(rows, qs):
            o_scr[r, :] += q
        return carry

    lax.fori_loop(0, n, scan_step, 0, unroll=2)

    def finish(i, carry):
        r0 = pl.multiple_of(i * CONV_ROWS, CONV_ROWS)
        o = o_scr[pl.ds(r0, CONV_ROWS), :]
        y = o * lax.rsqrt(jnp.mean(o * o, axis=-1, keepdims=True) + EPS) * nw_ref[...]
        y = y * _silu(z_ref[pl.ds(r0, CONV_ROWS), :].astype(F32))
        y = y * _sigmoid(mb_ref[pl.ds(r0, CONV_ROWS), :].astype(F32))
        out_ref[pl.ds(r0, CONV_ROWS), :] = y.astype(BF16)
        return carry

    lax.fori_loop(0, seq // CONV_ROWS, finish, 0)


def _gdn(main, small, gates, conv_w, norm_w, batch, seq):
    n = batch * seq
    kern = functools.partial(_gdn_kernel, seq=seq, chunk=GDN_CHUNK)
    n_chunks = seq // GDN_CHUNK
    blk = lambda off: pl.BlockSpec((seq, LANES), lambda b, hh, off=off: (b, off // LANES + hh))
    cw = lambda part: pl.BlockSpec((GDN_CONV, LANES), lambda b, hh, part=part: (0, part * GDN_HEADS + hh))
    seq_f32 = lambda: pltpu.VMEM((seq, LANES), F32)
    seq_bf16 = lambda: pltpu.VMEM((seq, LANES), BF16)
    return pl.pallas_call(
        kern,
        grid=(batch, GDN_HEADS),
        in_specs=[
            pl.BlockSpec((SUBLANES, LANES), lambda b, hh: (0, 0)),
            blk(COL_DQ), blk(COL_DK), blk(COL_DV), blk(COL_DZ), blk(COL_MB),
            pl.BlockSpec((seq, LANES), lambda b, hh: (b, 0)),
            cw(0), cw(1), cw(2),
            pl.BlockSpec((1, GDN_DV), lambda b, hh: (0, 0)),
        ],
        out_specs=pl.BlockSpec((seq, GDN_DV), lambda b, hh: (b, hh)),
        out_shape=jax.ShapeDtypeStruct((n, D_MODEL), BF16),
        scratch_shapes=[
            pltpu.VMEM((seq + 2 * SUBLANES, LANES), F32),
            seq_bf16(), seq_bf16(), seq_bf16(),
            seq_f32(), seq_f32(), seq_f32(), seq_f32(),
            seq_f32(),
            pltpu.VMEM((2 * n_chunks, GDN_DK, GDN_DK), BF16),
            pltpu.VMEM((2 * n_chunks, GDN_DK, GDN_DV), F32),
            pltpu.VMEM((2 * n_chunks, GDN_CHUNK, GDN_DK), BF16),
            pltpu.VMEM((2 * n_chunks, SUBLANES, LANES), F32),
            pltpu.VMEM((GDN_DK, GDN_DV), F32), pltpu.VMEM((GDN_DK, GDN_DV), F32),
        ],
        compiler_params=pltpu.CompilerParams(
            dimension_semantics=("arbitrary", "arbitrary"), vmem_limit_bytes=VMEM_LIMIT),
        name="gdn",
    )(gates, main, main, main, main, main, small, conv_w, conv_w, conv_w, norm_w)


def _outproj_kernel(ga_ref, gb_ref, x_ref, wo_ref, nw_ref, wrh_ref, wrl_ref,
                    x1_ref, h2_ref, ri_ref, rw_ref, cnt_ref, carry_scr, *, tm):
    @pl.when(pl.program_id(0) == 0)
    def _():
        carry_scr[...] = jnp.zeros_like(carry_scr)

    mixed = (ga_ref[...].astype(F32) + gb_ref[...].astype(F32)).astype(BF16)
    x1 = x_ref[...] + _dot(mixed, wo_ref[...])
    x1_ref[...] = x1
    h2 = x1 * lax.rsqrt(jnp.mean(x1 * x1, axis=-1, keepdims=True) + EPS) * nw_ref[...]
    _store_row_tiles(h2_ref, h2)
    hh, hl = _split2(h2)
    lg = _dot(hh, wrh_ref[...]) + _dot(hl, wrh_ref[...]) + _dot(hh, wrl_ref[...])

    lane_i = _iota2((tm, LANES), 1)
    lane = lane_i.astype(F32)

    def first_argmax(vals):
        m = jnp.max(vals, axis=-1, keepdims=True)
        idx = jnp.min(jnp.where(vals == m, lane, float(LANES)), axis=-1, keepdims=True)
        return m, idx

    is_g = (lane_i >= ROUTE_GROUP_LANE) & (lane_i < ROUTE_GROUP_LANE + N_GROUPS)
    gmax, glane = first_argmax(jnp.where(is_g, lg, NEG_INF))
    gidx = glane - float(ROUTE_GROUP_LANE)
    gsum = jnp.sum(jnp.where(is_g, jnp.exp(lg - gmax), 0.0), axis=-1, keepdims=True)
    group_w = 1.0 / gsum
    lane_group = jnp.right_shift(lane_i, EXPERTS_PER_GROUP.bit_length() - 1).astype(F32)
    in_grp = (lane_i < N_EXPERTS) & (lane_group == gidx)
    el = jnp.where(in_grp, lg, NEG_INF)
    m1, e0 = first_argmax(el)
    m2, e1 = first_argmax(jnp.where(lane == e0, NEG_INF, el))
    r = jnp.exp(m2 - m1)
    w0 = group_w / (1.0 + r)
    w1 = group_w * r / (1.0 + r)

    pick0 = lane == e0
    pick1 = lane == e1
    onehot = jnp.where(pick0 | pick1, 1.0, 0.0)
    trow = _iota2((tm, tm), 0)
    tcol = _iota2((tm, tm), 1)
    before = jnp.where(trow > tcol, 1.0, 0.0).astype(BF16)
    cnt = _dot(before, onehot.astype(BF16)) + carry_scr[0:1, :]
    rank0 = jnp.sum(jnp.where(pick0, cnt, 0.0), axis=-1, keepdims=True)
    rank1 = jnp.sum(jnp.where(pick1, cnt, 0.0), axis=-1, keepdims=True)
    total = carry_scr[0:1, :] + jnp.sum(onehot, axis=0, keepdims=True)
    carry_scr[...] = jnp.broadcast_to(total, carry_scr.shape)
    cnt_ref[...] = jnp.broadcast_to(total, cnt_ref.shape).astype(jnp.int32)

    ri = jnp.where(lane_i == 0, e0, jnp.where(lane_i == 1, e1, 0.0))
    ri = jnp.where(lane_i == 2, rank0, jnp.where(lane_i == 3, rank1, ri))
    ri_ref[...] = jnp.transpose(ri)[0:SUBLANES, :].astype(jnp.int32)
    rw_ref[...] = jnp.where(lane_i == 0, w0, jnp.where(lane_i == 1, w1, 0.0))


def _outproj(ga, gb, x2, w_out, norm_w, wr_hi, wr_lo):
    n = x2.shape[0]
    tm = OUTPROJ_TM
    kern = functools.partial(_outproj_kernel, tm=tm)
    row_blk = lambda w: pl.BlockSpec((tm, w), lambda i: (i, 0))
    const = lambda shape: pl.BlockSpec(shape, lambda i: (0, 0))
    return pl.pallas_call(
        kern,
        grid=(n // tm,),
        in_specs=[
            row_blk(D_MODEL), row_blk(D_MODEL), row_blk(D_MODEL),
            const((D_MODEL, D_MODEL)), const((1, D_MODEL)),
            const((D_MODEL, LANES)), const((D_MODEL, LANES)),
        ],
        out_specs=[row_blk(D_MODEL),
                   pl.BlockSpec((tm * ROW_TILE, LANES), lambda i: (i, 0)),
                   pl.BlockSpec((SUBLANES, tm), lambda i: (0, i)),
                   row_blk(LANES),
                   const((SUBLANES, LANES))],
        out_shape=[
            jax.ShapeDtypeStruct((n, D_MODEL), F32),
            jax.ShapeDtypeStruct((n * ROW_TILE, LANES), F32),
            jax.ShapeDtypeStruct((SUBLANES, n), jnp.int32),
            jax.ShapeDtypeStruct((n, LANES), F32),
            jax.ShapeDtypeStruct((SUBLANES, LANES), jnp.int32),
        ],
        scratch_shapes=[pltpu.VMEM((SUBLANES, LANES), F32)],
        compiler_params=pltpu.CompilerParams(
            dimension_semantics=("arbitrary",), vmem_limit_bytes=VMEM_LIMIT),
        name="outproj",
    )(ga, gb, x2, w_out, norm_w, wr_hi, wr_lo)


def _row_copy(src_ref, src_row, dst_ref, dst_row, sem):
    src = src_ref.at[pl.ds(pl.multiple_of(src_row * ROW_TILE, ROW_TILE), ROW_TILE)]
    dst = dst_ref.at[pl.ds(pl.multiple_of(dst_row * ROW_TILE, ROW_TILE), ROW_TILE)]
    return pltpu.make_async_copy(src, dst, sem)


def _scatter_kernel(seg_ref, d0_ref, d1_ref, h2_ref, xs_ref, zero_scr, sem, zsem, *, tile):
    i = pl.program_id(0)

    def issue(t, carry):
        _row_copy(h2_ref, t, xs_ref, d0_ref[t], sem).start()
        _row_copy(h2_ref, t, xs_ref, d1_ref[t], sem).start()
        return carry

    lax.fori_loop(0, tile, issue, 0, unroll=DMA_UNROLL)

    @pl.when(i == 0)
    def _():
        zero_scr[...] = jnp.zeros_like(zero_scr)

        def per_expert(e, carry):
            lo, hi = seg_ref[0, e], seg_ref[1, e]

            def start(r, c2):
                _row_copy(zero_scr, 0, xs_ref, r, zsem).start()
                return c2

            def wait(r, c2):
                _row_copy(zero_scr, 0, xs_ref, r, zsem).wait()
                return c2

            lax.fori_loop(lo, hi, start, 0)
            lax.fori_loop(lo, hi, wait, 0)
            return carry

        lax.fori_loop(0, N_EXPERTS + 1, per_expert, 0)

    def drain(t, carry):
        _row_copy(h2_ref, 0, xs_ref, 0, sem).wait()
        _row_copy(h2_ref, 0, xs_ref, 0, sem).wait()
        return carry

    lax.fori_loop(0, tile, drain, 0, unroll=DMA_UNROLL)


def _scatter(seg, dest0, dest1, h2t, n_rows):
    n = dest0.shape[0]
    tile = SCATTER_T
    kern = functools.partial(_scatter_kernel, tile=tile)
    return pl.pallas_call(
        kern,
        grid=(n // tile,),
        in_specs=[
            pl.BlockSpec(memory_space=pltpu.SMEM),
            pl.BlockSpec((tile,), lambda i: (i,), memory_space=pltpu.SMEM),
            pl.BlockSpec((tile,), lambda i: (i,), memory_space=pltpu.SMEM),
            pl.BlockSpec((tile * ROW_TILE, LANES), lambda i: (i, 0)),
        ],
        out_specs=pl.BlockSpec(memory_space=pl.ANY),
        out_shape=jax.ShapeDtypeStruct((n_rows * ROW_TILE, LANES), F32),
        scratch_shapes=[pltpu.VMEM((ROW_TILE, LANES), F32),
                        pltpu.SemaphoreType.DMA, pltpu.SemaphoreType.DMA],
        compiler_params=pltpu.CompilerParams(
            dimension_semantics=("arbitrary",), vmem_limit_bytes=VMEM_LIMIT),
        name="scatter",
    )(seg, dest0, dest1, h2t)


def _expert_kernel(be_ref, nv_ref, xs_ref, wg_ref, wu_ref, wd_ref, y_ref):
    i = pl.program_id(0)

    @pl.when(i < nv_ref[0])
    def _():
        x = _load_row_tiles(xs_ref).astype(BF16)
        g = _dot(x, wg_ref[0])
        u = _dot(x, wu_ref[0])
        hid = (_silu(g) * u).astype(BF16)
        _store_row_tiles(y_ref, _dot(hid, wd_ref[0]))

    @pl.when(i >= nv_ref[0])
    def _():
        y_ref[...] = jnp.zeros_like(y_ref)


def _experts(block_expert, n_valid, xs, w_gate, w_up, w_down):
    blk = MOE_BLOCK
    n_rows = xs.shape[0] // ROW_TILE
    grid_spec = pltpu.PrefetchScalarGridSpec(
        num_scalar_prefetch=2,
        grid=(n_rows // blk,),
        in_specs=[
            pl.BlockSpec((blk * ROW_TILE, LANES), lambda i, be, nv: (jnp.minimum(i, nv[0] - 1), 0)),
            pl.BlockSpec((1, D_MODEL, D_EXPERT), lambda i, be, nv: (be[i], 0, 0)),
            pl.BlockSpec((1, D_MODEL, D_EXPERT), lambda i, be, nv: (be[i], 0, 0)),
            pl.BlockSpec((1, D_EXPERT, D_MODEL), lambda i, be, nv: (be[i], 0, 0)),
        ],
        out_specs=pl.BlockSpec((blk * ROW_TILE, LANES), lambda i, be, nv: (i, 0)),
    )
    return pl.pallas_call(
        _expert_kernel,
        grid_spec=grid_spec,
        out_shape=jax.ShapeDtypeStruct((n_rows * ROW_TILE, LANES), F32),
        compiler_params=pltpu.CompilerParams(
            dimension_semantics=("arbitrary",), vmem_limit_bytes=VMEM_LIMIT),
        name="experts",
    )(block_expert, n_valid, xs, w_gate, w_up, w_down)


def _combine_kernel(d0_ref, d1_ref, x1_ref, rw_ref, nw_ref, y_ref, out_ref, ya_scr, yb_scr, sem, *, tile):
    def issue(t, carry):
        _row_copy(y_ref, d0_ref[t], ya_scr, t, sem).start()
        _row_copy(y_ref, d1_ref[t], yb_scr, t, sem).start()
        return carry

    def drain(t, carry):
        _row_copy(y_ref, 0, ya_scr, 0, sem).wait()
        _row_copy(y_ref, 0, yb_scr, 0, sem).wait()
        return carry

    lax.fori_loop(0, tile, issue, 0, unroll=DMA_UNROLL)
    lax.fori_loop(0, tile, drain, 0, unroll=DMA_UNROLL)

    rw = rw_ref[...]
    moe = rw[:, 0:1] * _load_row_tiles(ya_scr) + rw[:, 1:2] * _load_row_tiles(yb_scr)
    x2 = x1_ref[...] + moe
    out_ref[...] = x2 * lax.rsqrt(jnp.mean(x2 * x2, axis=-1, keepdims=True) + EPS) * nw_ref[...]


def _combine(dest0, dest1, x1, rw, norm_w, y):
    n = x1.shape[0]
    tile = COMBINE_T
    kern = functools.partial(_combine_kernel, tile=tile)
    return pl.pallas_call(
        kern,
        grid=(n // tile,),
        in_specs=[
            pl.BlockSpec((tile,), lambda i: (i,), memory_space=pltpu.SMEM),
            pl.BlockSpec((tile,), lambda i: (i,), memory_space=pltpu.SMEM),
            pl.BlockSpec((tile, D_MODEL), lambda i: (i, 0)),
            pl.BlockSpec((tile, LANES), lambda i: (i, 0)),
            pl.BlockSpec((1, D_MODEL), lambda i: (0, 0)),
            pl.BlockSpec(memory_space=pl.ANY),
        ],
        out_specs=pl.BlockSpec((tile, D_MODEL), lambda i: (i, 0)),
        out_shape=jax.ShapeDtypeStruct((n, D_MODEL), F32),
        scratch_shapes=[
            pltpu.VMEM((tile * ROW_TILE, LANES), F32), pltpu.VMEM((tile * ROW_TILE, LANES), F32),
            pltpu.SemaphoreType.DMA,
        ],
        compiler_params=pltpu.CompilerParams(
            dimension_semantics=("arbitrary",), vmem_limit_bytes=VMEM_LIMIT),
        name="combine",
    )(dest0, dest1, x1, rw, norm_w, y)


def _pad_cols(w, width):
    return jnp.pad(w, ((0, 0), (0, width - w.shape[1])))


def _token_mixer_and_moe(x, norm1_w, w_in, w2_f, b_f, w2_b, b_b, gla_norm_w, conv_w, a_log_f, dt_bias_f,
                         a_log_b, dt_bias_b, gdn_norm_w, w_out, norm2_w, w_group, w_router, w_gate, w_up,
                         w_down, out_norm_w):
    batch, seq, d = x.shape
    n = batch * seq
    x2 = x.reshape(n, d)

    w_main = jnp.concatenate([w_in[:, :3072], w_in[:, 3104:7200], w_in[:, 7232:]], axis=1).astype(BF16)
    w_small = _pad_cols(jnp.concatenate([w_in[:, 3072:3104], w_in[:, 7200:7232]], axis=1), LANES)
    ws_hi, ws_lo = _split2(w_small)
    main, small = _inproj(x2, norm1_w.reshape(1, d), w_main, ws_hi, ws_lo)

    w2f_pad = jnp.zeros((LANES, GLA_HEADS * GLA_DK), F32).at[0:GLA_GATE_RANK].set(w2_f)
    w2b_pad = jnp.zeros((LANES, GLA_HEADS * GLA_DK), F32).at[GLA_GATE_RANK:2 * GLA_GATE_RANK].set(w2_b)
    ga = _gla(main, small, w2f_pad, w2b_pad, b_f.reshape(1, -1), b_b.reshape(1, -1),
              gla_norm_w.reshape(1, -1), batch, seq)

    gates = jnp.zeros((SUBLANES, LANES), F32)
    gates = gates.at[0, SMALL_AF:SMALL_BF].set(jnp.concatenate([a_log_f, a_log_b]))
    gates = gates.at[1, SMALL_AF:SMALL_BF].set(jnp.concatenate([dt_bias_f, dt_bias_b]))
    gb = _gdn(main, small, gates, conv_w, gdn_norm_w.reshape(1, -1), batch, seq)

    w_route = _pad_cols(jnp.concatenate([w_router, w_group], axis=1), LANES)
    wr_hi, wr_lo = _split2(w_route)
    x1, h2t, rt, rw, counts = _outproj(ga, gb, x2, w_out.astype(BF16), norm2_w.reshape(1, d), wr_hi, wr_lo)

    blk = MOE_BLOCK
    cnt = counts[0, :N_EXPERTS]
    padded = (cnt + blk - 1) // blk * blk
    ends = jnp.cumsum(padded)
    pstart = (ends - padded).astype(jnp.int32)
    n_blocks = -(-(2 * n + N_EXPERTS * (blk - 1)) // blk)
    n_rows = n_blocks * blk
    block_row = jnp.arange(n_blocks, dtype=jnp.int32) * blk
    block_expert = jnp.minimum(
        jnp.sum((ends[None, :] <= block_row[:, None]).astype(jnp.int32), axis=1), N_EXPERTS - 1)
    n_valid = (ends[-1:] // blk).astype(jnp.int32)
    seg = jnp.stack([jnp.append(pstart + cnt, ends[-1]), jnp.append(ends, n_rows)]).astype(jnp.int32)

    experts = jnp.arange(N_EXPERTS, dtype=jnp.int32)
    seg_start = jnp.sum(jnp.where(rt[0:2, :, None] == experts, pstart, 0), axis=-1)
    dest = seg_start + rt[2:4]
    dest0, dest1 = dest[0], dest[1]

    xs = _scatter(seg, dest0, dest1, h2t, n_rows)
    y = _experts(block_expert, n_valid, xs, w_gate.astype(BF16), w_up.astype(BF16), w_down.astype(BF16))
    out = _combine(dest0, dest1, x1, rw, out_norm_w.reshape(1, d), y)
    return out.reshape(batch, seq, d)


def kernel(x, norm1_w, w_in, gla_gate_w2_fwd, gla_gate_b_fwd, gla_gate_w2_bwd, gla_gate_b_bwd, gla_norm_w,
           gdn_conv_w, gdn_a_log_fwd, gdn_dt_bias_fwd, gdn_a_log_bwd, gdn_dt_bias_bwd, gdn_norm_w, w_out,
           norm2_w, moe_w_group, moe_w_router, moe_w_gate, moe_w_up, moe_w_down, norm_f_w):
    assert norm1_w.shape[0] == 1, "single-layer block"
    return _token_mixer_and_moe(
        x, norm1_w[0], w_in[0], gla_gate_w2_fwd[0], gla_gate_b_fwd[0], gla_gate_w2_bwd[0], gla_gate_b_bwd[0],
        gla_norm_w[0], gdn_conv_w[0], gdn_a_log_fwd[0], gdn_dt_bias_fwd[0], gdn_a_log_bwd[0],
        gdn_dt_bias_bwd[0], gdn_norm_w[0], w_out[0], norm2_w[0], moe_w_group[0], moe_w_router[0],
        moe_w_gate[0], moe_w_up[0], moe_w_down[0], norm_f_w)
```

```python
import functools

import jax
import jax.numpy as jnp
from jax import lax
from jax.experimental import pallas as pl
from jax.experimental.pallas import tpu as pltpu

F32 = jnp.float32
BF16 = jnp.bfloat16

D_MODEL = 1024
GLA_HEADS = 4
GLA_DK = 128
GLA_DV = 256
GLA_GATE_RANK = 16
GLA_GATE_TAU = 16.0
GLA_CHUNK = 64
GLA_GROUP = 4
GDN_HEADS = 8
GDN_DK = 128
GDN_DV = 128
GDN_CONV = 5
GDN_CHUNK = 128
GDN_PREP_GROUP = 8
N_GROUPS = 4
EXPERTS_PER_GROUP = 8
N_EXPERTS = N_GROUPS * EXPERTS_PER_GROUP
D_EXPERT = 256
EPS = 1e-6

LANES = 128
SUBLANES = 8
VMEM_LIMIT = 48 * 1024 * 1024

COL_GQ, COL_GK, COL_GV, COL_GR = 0, 512, 1024, 2048
COL_DQ, COL_DK, COL_DV, COL_DZ = 3072, 4096, 5120, 6144
COL_MA, COL_MB = 7168, 8192
D_MAIN = 9216
SMALL_AF, SMALL_AB, SMALL_BF, SMALL_BB = 32, 40, 48, 56
ROUTE_GROUP_LANE = 32

MOE_BLOCK = 256
ROW_TILE = D_MODEL // LANES
assert ROW_TILE == SUBLANES
DMA_UNROLL = 8
INPROJ_TM, INPROJ_TN = 1024, 1024
OUTPROJ_TM = 512
SCATTER_T = 512
COMBINE_T = 256
CONV_ROWS = 256
NEG_INF = float("-inf")


def _dot(a, b):
    return jnp.dot(a, b, preferred_element_type=F32)


def _dot_nt(a, b):
    return lax.dot_general(a, b, (((1,), (1,)), ((), ())), preferred_element_type=F32)


def _dot_tn(a, b):
    return lax.dot_general(a, b, (((0,), (0,)), ((), ())), preferred_element_type=F32)


def _split2(x):
    hi = x.astype(BF16)
    lo = (x - hi.astype(F32)).astype(BF16)
    return hi, lo


def _split3(x):
    hi = x.astype(BF16)
    r = x - hi.astype(F32)
    mid = r.astype(BF16)
    lo = (r - mid.astype(F32)).astype(BF16)
    return hi, mid, lo


def _dot_exact_rhs(x, m_bf16):
    hi, mid, lo = _split3(x)
    return _dot(hi, m_bf16) + _dot(mid, m_bf16) + _dot(lo, m_bf16)


def _dot_exact_lhs(m_bf16, x):
    hi, mid, lo = _split3(x)
    return _dot(m_bf16, hi) + _dot(m_bf16, mid) + _dot(m_bf16, lo)


def _dot_lhs2(m_bf16, x):
    hi, lo = _split2(x)
    return _dot(m_bf16, hi) + _dot(m_bf16, lo)


def _dot_lhs2_wide(m2_bf16, x):
    return _dot(m2_bf16, jnp.concatenate(_split2(x), axis=0))


def _dot3(a, b):
    ah, al = _split2(a)
    bh, bl = _split2(b)
    return _dot(ah, bh) + _dot(al, bh) + _dot(ah, bl)


def _store_row_tiles(ref, x):
    rows = x.shape[0]
    for j in range(ROW_TILE):
        ref[pl.ds(j, rows, stride=ROW_TILE), :] = x[:, j * LANES:(j + 1) * LANES]


def _load_row_tiles(ref):
    rows = ref.shape[0] // ROW_TILE
    return jnp.concatenate([ref[pl.ds(j, rows, stride=ROW_TILE), :] for j in range(ROW_TILE)], axis=1)


def _each(fn, *lists):
    return [fn(*args) for args in zip(*lists)]


def _sigmoid(x):
    return 1.0 / (1.0 + jnp.exp(-x))


def _silu(x):
    return x * _sigmoid(x)


def _softplus(x):
    return jnp.maximum(x, 0.0) + jnp.log(1.0 + jnp.exp(-jnp.abs(x)))


def _log_sigmoid(x):
    return jnp.minimum(x, 0.0) - jnp.log(1.0 + jnp.exp(-jnp.abs(x)))


def _iota2(shape, dim):
    return lax.broadcasted_iota(jnp.int32, shape, dim)


def _inproj_kernel(x_ref, nw_ref, w_ref, wsh_ref, wsl_ref, main_ref, small_ref, h_scr):
    @pl.when(pl.program_id(1) == 0)
    def _():
        x = x_ref[...]
        h = x * lax.rsqrt(jnp.mean(x * x, axis=-1, keepdims=True) + EPS) * nw_ref[...]
        hh, hl = _split2(h)
        h_scr[...] = hh
        small_ref[...] = _dot(hh, wsh_ref[...]) + _dot(hl, wsh_ref[...]) + _dot(hh, wsl_ref[...])

    main_ref[...] = _dot(h_scr[...], w_ref[...]).astype(BF16)


def _inproj(x2, norm_w, w_main, ws_hi, ws_lo):
    n = x2.shape[0]
    tm, tn = INPROJ_TM, INPROJ_TN
    return pl.pallas_call(
        _inproj_kernel,
        grid=(n // tm, D_MAIN // tn),
        in_specs=[
            pl.BlockSpec((tm, D_MODEL), lambda i, j: (i, 0)),
            pl.BlockSpec((1, D_MODEL), lambda i, j: (0, 0)),
            pl.BlockSpec((D_MODEL, tn), lambda i, j: (0, j)),
            pl.BlockSpec((D_MODEL, LANES), lambda i, j: (0, 0)),
            pl.BlockSpec((D_MODEL, LANES), lambda i, j: (0, 0)),
        ],
        out_specs=[
            pl.BlockSpec((tm, tn), lambda i, j: (i, j)),
            pl.BlockSpec((tm, LANES), lambda i, j: (i, 0)),
        ],
        out_shape=[
            jax.ShapeDtypeStruct((n, D_MAIN), BF16),
            jax.ShapeDtypeStruct((n, LANES), F32),
        ],
        scratch_shapes=[pltpu.VMEM((tm, D_MODEL), BF16)],
        compiler_params=pltpu.CompilerParams(
            dimension_semantics=("arbitrary", "arbitrary"), vmem_limit_bytes=VMEM_LIMIT),
        name="inproj",
    )(x2, norm_w, w_main, ws_hi, ws_lo)


def _gla_kernel(q_ref, k_ref, v_ref, gr_ref, ma_ref, small_ref, w2f_ref, w2b_ref, bf_ref, bb_ref,
                nw_ref, out_ref, laf_scr, lab_scr, o_scr, stf_scr, stb_scr, *, seq, chunk):
    c = chunk
    n = seq // c
    scale = GLA_DK ** -0.5

    sm = small_ref[...]
    laf_scr[...] = _log_sigmoid(_dot3(sm, w2f_ref[...]) + bf_ref[...]) * (1.0 / GLA_GATE_TAU)
    lab_scr[...] = _log_sigmoid(_dot3(sm, w2b_ref[...]) + bb_ref[...]) * (1.0 / GLA_GATE_TAU)
    stf_scr[...] = jnp.zeros_like(stf_scr)
    stb_scr[...] = jnp.zeros_like(stb_scr)

    row = _iota2((c, c), 0)
    col = _iota2((c, c), 1)
    low = row >= col
    upp = row <= col
    low_m = jnp.concatenate([jnp.where(low, 1.0, 0.0).astype(BF16)] * 2, axis=1)
    upp_m = jnp.concatenate([jnp.where(upp, 1.0, 0.0).astype(BF16)] * 2, axis=1)

    g = GLA_GROUP

    def finish(rows, o):
        y = o * lax.rsqrt(jnp.mean(o * o, axis=-1, keepdims=True) + EPS) * nw_ref[...]
        y = y * _silu(gr_ref[rows, :].astype(F32))
        y = y * _sigmoid(ma_ref[rows, :].astype(F32))
        out_ref[rows, :] = y.astype(BF16)

    def group(gi, second_touch):
        ids = [gi * g + j for j in range(g)] + [n - 1 - gi * g - j for j in range(g)]
        rows = [pl.ds(pl.multiple_of(i * c, c), c) for i in ids]
        la = [laf_scr[r, :] for r in rows[:g]] + [lab_scr[r, :] for r in rows[g:]]
        csum = [low_m] * g + [upp_m] * g
        mask = [low] * g + [upp] * g
        tot_row = [c - 1] * g + [0] * g
        qf = [q_ref[r, :].astype(F32) * scale for r in rows]
        kf = [k_ref[r, :].astype(F32) for r in rows]
        vc = [v_ref[r, :] for r in rows]

        cum = _each(_dot_lhs2_wide, csum, la)
        tot = _each(lambda x, r: x[r:r + 1, :], cum, tot_row)
        q_dec = _each(lambda q, x: (q * jnp.exp(x)).astype(BF16), qf, cum)
        k_inv = _each(lambda k, x: (k * jnp.exp(-x)).astype(BF16), kf, cum)
        k_tail = _each(lambda k, t, x: (k * jnp.exp(t - x)).astype(BF16), kf, tot, cum)
        s = _each(lambda m, q, k: jnp.where(m, _dot_nt(q, k), 0.0).astype(BF16), mask, q_dec, k_inv)
        o = _each(_dot, s, vc)
        kv = _each(_dot_tn, vc, k_tail)
        dec = _each(jnp.exp, tot)

        for st_scr, probs in ((stf_scr, range(g)), (stb_scr, range(g, 2 * g))):
            st = st_scr[...]
            for p in probs:
                o[p] = o[p] + _dot_nt(q_dec[p], st.astype(BF16))
                st = dec[p] * st + kv[p]
            st_scr[...] = st

        for r, o_p in zip(rows, o):
            if second_touch:
                finish(r, o_scr[r, :] + o_p)
            else:
                o_scr[r, :] = o_p

    def first_half(gi, carry):
        group(gi, False)
        return carry

    def second_half(gi, carry):
        group(gi, True)
        return carry

    n_groups = n // g
    lax.fori_loop(0, n_groups // 2, first_half, 0)
    lax.fori_loop(n_groups // 2, n_groups, second_half, 0)


def _gla(main, small, w2f_pad, w2b_pad, b_f, b_b, norm_w, batch, seq):
    n = batch * seq
    h = GLA_HEADS
    kern = functools.partial(_gla_kernel, seq=seq, chunk=GLA_CHUNK)
    qk_blk = lambda off: pl.BlockSpec((seq, GLA_DK), lambda b, hh, off=off: (b, off // GLA_DK + hh))
    v_blk = lambda off: pl.BlockSpec((seq, GLA_DV), lambda b, hh, off=off: (b, off // GLA_DV + hh))
    return pl.pallas_call(
        kern,
        grid=(batch, h),
        in_specs=[
            qk_blk(COL_GQ), qk_blk(COL_GK), v_blk(COL_GV), v_blk(COL_GR), v_blk(COL_MA),
            pl.BlockSpec((seq, LANES), lambda b, hh: (b, 0)),
            pl.BlockSpec((LANES, GLA_DK), lambda b, hh: (0, hh)),
            pl.BlockSpec((LANES, GLA_DK), lambda b, hh: (0, hh)),
            pl.BlockSpec((1, GLA_DK), lambda b, hh: (0, hh)),
            pl.BlockSpec((1, GLA_DK), lambda b, hh: (0, hh)),
            pl.BlockSpec((1, GLA_DV), lambda b, hh: (0, 0)),
        ],
        out_specs=pl.BlockSpec((seq, GLA_DV), lambda b, hh: (b, hh)),
        out_shape=jax.ShapeDtypeStruct((n, D_MODEL), BF16),
        scratch_shapes=[
            pltpu.VMEM((seq, GLA_DK), F32), pltpu.VMEM((seq, GLA_DK), F32),
            pltpu.VMEM((seq, GLA_DV), F32),
            pltpu.VMEM((GLA_DV, GLA_DK), F32), pltpu.VMEM((GLA_DV, GLA_DK), F32),
        ],
        compiler_params=pltpu.CompilerParams(
            dimension_semantics=("arbitrary", "arbitrary"), vmem_limit_bytes=VMEM_LIMIT),
        name="gla",
    )(main, main, main, main, main, small, w2f_pad, w2b_pad, b_f, b_b, norm_w)


TRI_BLOCK = 16


def _mm(a, b):
    return _dot(a.astype(BF16), b.astype(BF16))


def _nilpotent_inverse(a_list, eye, index):
    t_list = _each(lambda a: eye - a, a_list)
    p_list = a_list
    power = 2
    while power < index:
        p_list = _each(lambda p: _mm(p, p), p_list)
        t_list = _each(lambda t, p: t + _mm(t, p), t_list, p_list)
        power *= 2
    return t_list


def _tri_inverse(a_list, eye, diag_blocks, chunk):
    ad_list = _each(lambda a: jnp.where(diag_blocks, a, 0.0), a_list)
    ao_list = _each(lambda a: jnp.where(diag_blocks, 0.0, a), a_list)
    d_list = _nilpotent_inverse(ad_list, eye, TRI_BLOCK)
    n_list = _each(_mm, d_list, ao_list)
    t_list = _nilpotent_inverse(n_list, eye, chunk // TRI_BLOCK)
    return _each(_mm, t_list, d_list)


def _gdn_kernel(gate_ref, q_ref, k_ref, v_ref, z_ref, mb_ref, small_ref, cwq_ref, cwk_ref, cwv_ref,
                nw_ref, out_ref, pad_scr, qs_scr, ks_scr, vs_scr, gf_scr, gb_scr, btf_scr, btb_scr,
                o_scr, nmat_scr, bmat_scr, qp_scr, cd_scr, sf_scr, sb_scr, *, seq, chunk):
    c = chunk
    n = seq // c
    hh = pl.program_id(1)
    scale = GDN_DK ** -0.5

    zeros8 = jnp.zeros((SUBLANES, LANES), F32)
    pad_scr[0:SUBLANES, :] = zeros8
    pad_scr[seq + SUBLANES:seq + 2 * SUBLANES, :] = zeros8
    half = GDN_CONV // 2

    def conv_into(src_ref, cw_ref, dst_ref, normalise, mult):
        pad_scr[SUBLANES:seq + SUBLANES, :] = src_ref[...].astype(F32)
        w = cw_ref[...]

        def body(i, carry):
            r0 = pl.multiple_of(i * CONV_ROWS, CONV_ROWS)
            acc = jnp.zeros((CONV_ROWS, LANES), F32)
            for j in range(GDN_CONV):
                tap = pad_scr[pl.ds(r0 + (SUBLANES - half + j), CONV_ROWS), :]
                acc = acc + tap * w[j:j + 1, :]
            y = _silu(acc)
            if normalise:
                y = y * lax.rsqrt(jnp.sum(y * y, axis=-1, keepdims=True) + EPS) * mult
            dst_ref[pl.ds(r0, CONV_ROWS), :] = y.astype(BF16)
            return carry

        lax.fori_loop(0, seq // CONV_ROWS, body, 0)

    conv_into(q_ref, cwq_ref, qs_scr, True, scale)
    conv_into(k_ref, cwk_ref, ks_scr, True, 1.0)
    conv_into(v_ref, cwv_ref, vs_scr, False, 1.0)

    sm = small_ref[...]
    lane = _iota2(sm.shape, 1)
    log_decay = -jnp.exp(gate_ref[0:1, :]) * _softplus(sm + gate_ref[1:2, :])
    gate_vals = jnp.where(lane < SMALL_BF, log_decay, _sigmoid(sm))
    gate_hl = jnp.concatenate(_split2(gate_vals), axis=1)
    sel_lane = jnp.bitwise_and(_iota2((2 * LANES, 4 * LANES), 0), LANES - 1)
    sel_gate = jnp.right_shift(_iota2((2 * LANES, 4 * LANES), 1), LANES.bit_length() - 1)
    sel = jnp.where(sel_lane == SMALL_AF + GDN_HEADS * sel_gate + hh, 1.0, 0.0).astype(BF16)
    spread = _dot(gate_hl, sel)
    gf_scr[...] = spread[:, 0 * LANES:1 * LANES]
    gb_scr[...] = spread[:, 1 * LANES:2 * LANES]
    btf_scr[...] = spread[:, 2 * LANES:3 * LANES]
    btb_scr[...] = spread[:, 3 * LANES:4 * LANES]
    sf_scr[...] = jnp.zeros_like(sf_scr)
    sb_scr[...] = jnp.zeros_like(sb_scr)

    row = _iota2((c, c), 0)
    col = _iota2((c, c), 1)
    eye = jnp.where(row == col, 1.0, 0.0).astype(F32)
    low, slow = row >= col, row > col
    upp, supp = row <= col, row < col
    low_m = jnp.concatenate([jnp.where(low, 1.0, 0.0).astype(BF16)] * 2, axis=1)
    upp_m = jnp.concatenate([jnp.where(upp, 1.0, 0.0).astype(BF16)] * 2, axis=1)

    tri_shift = TRI_BLOCK.bit_length() - 1
    diag_blocks = jnp.right_shift(row, tri_shift) == jnp.right_shift(col, tri_shift)

    def prep_group(gi, carry):
        chunk_ids = [gi * GDN_PREP_GROUP + j for j in range(GDN_PREP_GROUP)]
        rows = [pl.ds(pl.multiple_of(i * c, c), c) for i in chunk_ids]
        qc = [qs_scr[r, :] for r in rows]
        kc = [ks_scr[r, :] for r in rows]
        vc = [vs_scr[r, :] for r in rows]
        gl = [ref[r, :] for r in rows for ref in (gf_scr, gb_scr)]
        bt = [ref[r, :] for r in rows for ref in (btf_scr, btb_scr)]
        csum = [low_m, upp_m] * GDN_PREP_GROUP
        incl = [low, upp] * GDN_PREP_GROUP
        strict = [slow, supp] * GDN_PREP_GROUP
        tot_row = [c - 1, 0] * GDN_PREP_GROUP

        def both(per_chunk):
            return [x for x in per_chunk for _ in range(2)]

        kk = both(_each(_dot_nt, kc, kc))
        qk = both(_each(_dot_nt, qc, kc))
        qf = both(_each(lambda x: x.astype(F32), qc))
        kf = both(_each(lambda x: x.astype(F32), kc))
        vf = both(_each(lambda x: x.astype(F32), vc))

        gc = _each(_dot_lhs2_wide, csum, gl)
        tot = _each(lambda g, r: g[r:r + 1, :], gc, tot_row)
        e = _each(lambda m, g, st: jnp.exp(_dot_lhs2_wide(m, jnp.where(st, g[:, :c], 0.0))), csum, gl, strict)
        a = _each(lambda kk_, b, e_, st: kk_ * b[:, :c] * jnp.where(st, e_, 0.0), kk, bt, e, strict)
        t_inv = _tri_inverse(a, eye, diag_blocks, c)
        egc = _each(jnp.exp, gc)
        wu = _each(lambda t, k, v, b, eg: _mm(t, jnp.concatenate([k * b * eg, v * b], axis=1)).astype(BF16),
                   t_inv, kf, vf, bt, egc)
        attn = _each(lambda qk_, e_, inc: (qk_ * jnp.where(inc, e_, 0.0)).astype(BF16), qk, e, incl)
        k_tail = _each(lambda k, t, g: (k * jnp.exp(t - g)).astype(BF16), kf, tot, gc)
        kwu = _each(_dot_tn, k_tail, wu)
        awu = _each(_dot, attn, wu)

        for p in range(2 * GDN_PREP_GROUP):
            slot = chunk_ids[p // 2] + (p % 2) * n
            nmat_scr[slot] = kwu[p][:, :GDN_DK].astype(BF16)
            bmat_scr[slot] = kwu[p][:, GDN_DK:]
            qp_scr[slot] = (qf[p] * egc[p] - awu[p][:, :GDN_DK]).astype(BF16)
            cd_scr[slot] = jnp.broadcast_to(jnp.exp(tot[p]), (SUBLANES, LANES))
        for j, r in enumerate(rows):
            o_scr[r, :] = awu[2 * j][:, GDN_DK:] + awu[2 * j + 1][:, GDN_DK:]
        return carry

    lax.fori_loop(0, n // GDN_PREP_GROUP, prep_group, 0)

    def scan_step(i, carry):
        j = n - 1 - i
        slots = [i, n + j]
        rows = [pl.ds(pl.multiple_of(i * c, c), c), pl.ds(pl.multiple_of(j * c, c), c)]
        states = [sf_scr, sb_scr]
        s = [ref[...] for ref in states]
        s_b = _each(lambda x: x.astype(BF16), s)
        ns = _each(lambda sl, x: _dot(nmat_scr[sl], x), slots, s_b)
        qs = _each(lambda sl, x: _dot(qp_scr[sl], x), slots, s_b)
        for ref, sl, s_, ns_ in zip(states, slots, s, ns):
            ref[...] = cd_scr[sl][0:1, :] * s_ + (bmat_scr[sl] - ns_)
        for r, q in zip(rows, qs):
            o_scr[r, :] += q
        return carry

    lax.fori_loop(0, n, scan_step, 0, unroll=2)

    def finish(i, carry):
        r0 = pl.multiple_of(i * CONV_ROWS, CONV_ROWS)
        o = o_scr[pl.ds(r0, CONV_ROWS), :]
        y = o * lax.rsqrt(jnp.mean(o * o, axis=-1, keepdims=True) + EPS) * nw_ref[...]
        y = y * _silu(z_ref[pl.ds(r0, CONV_ROWS), :].astype(F32))
        y = y * _sigmoid(mb_ref[pl.ds(r0, CONV_ROWS), :].astype(F32))
        out_ref[pl.ds(r0, CONV_ROWS), :] = y.astype(BF16)
        return carry

    lax.fori_loop(0, seq // CONV_ROWS, finish, 0)


def _gdn(main, small, gates, conv_w, norm_w, batch, seq):
    n = batch * seq
    kern = functools.partial(_gdn_kernel, seq=seq, chunk=GDN_CHUNK)
    n_chunks = seq // GDN_CHUNK
    blk = lambda off: pl.BlockSpec((seq, LANES), lambda b, hh, off=off: (b, off // LANES + hh))
    cw = lambda part: pl.BlockSpec((GDN_CONV, LANES), lambda b, hh, part=part: (0, part * GDN_HEADS + hh))
    seq_f32 = lambda: pltpu.VMEM((seq, LANES), F32)
    seq_bf16 = lambda: pltpu.VMEM((seq, LANES), BF16)
    return pl.pallas_call(
        kern,
        grid=(batch, GDN_HEADS),
        in_specs=[
            pl.BlockSpec((SUBLANES, LANES), lambda b, hh: (0, 0)),
            blk(COL_DQ), blk(COL_DK), blk(COL_DV), blk(COL_DZ), blk(COL_MB),
            pl.BlockSpec((seq, LANES), lambda b, hh: (b, 0)),
            cw(0), cw(1), cw(2),
            pl.BlockSpec((1, GDN_DV), lambda b, hh: (0, 0)),
        ],
        out_specs=pl.BlockSpec((seq, GDN_DV), lambda b, hh: (b, hh)),
        out_shape=jax.ShapeDtypeStruct((n, D_MODEL), BF16),
        scratch_shapes=[
            pltpu.VMEM((seq + 2 * SUBLANES, LANES), F32),
            seq_bf16(), seq_bf16(), seq_bf16(),
            seq_f32(), seq_f32(), seq_f32(), seq_f32(),
            seq_f32(),
            pltpu.VMEM((2 * n_chunks, GDN_DK, GDN_DK), BF16),
            pltpu.VMEM((2 * n_chunks, GDN_DK, GDN_DV), F32),
            pltpu.VMEM((2 * n_chunks, GDN_CHUNK, GDN_DK), BF16),
            pltpu.VMEM((2 * n_chunks, SUBLANES, LANES), F32),
            pltpu.VMEM((GDN_DK, GDN_DV), F32), pltpu.VMEM((GDN_DK, GDN_DV), F32),
        ],
        compiler_params=pltpu.CompilerParams(
            dimension_semantics=("arbitrary", "arbitrary"), vmem_limit_bytes=VMEM_LIMIT),
        name="gdn",
    )(gates, main, main, main, main, main, small, conv_w, conv_w, conv_w, norm_w)


def _outproj_kernel(ga_ref, gb_ref, x_ref, wo_ref, nw_ref, wrh_ref, wrl_ref,
                    x1_ref, h2_ref, ri_ref, rw_ref, cnt_ref, carry_scr, *, tm):
    @pl.when(pl.program_id(0) == 0)
    def _():
        carry_scr[...] = jnp.zeros_like(carry_scr)

    mixed = (ga_ref[...].astype(F32) + gb_ref[...].astype(F32)).astype(BF16)
    x1 = x_ref[...] + _dot(mixed, wo_ref[...])
    x1_ref[...] = x1
    h2 = x1 * lax.rsqrt(jnp.mean(x1 * x1, axis=-1, keepdims=True) + EPS) * nw_ref[...]
    _store_row_tiles(h2_ref, h2)
    hh, hl = _split2(h2)
    lg = _dot(hh, wrh_ref[...]) + _dot(hl, wrh_ref[...]) + _dot(hh, wrl_ref[...])

    lane_i = _iota2((tm, LANES), 1)
    lane = lane_i.astype(F32)

    def first_argmax(vals):
        m = jnp.max(vals, axis=-1, keepdims=True)
        idx = jnp.min(jnp.where(vals == m, lane, float(LANES)), axis=-1, keepdims=True)
        return m, idx

    is_g = (lane_i >= ROUTE_GROUP_LANE) & (lane_i < ROUTE_GROUP_LANE + N_GROUPS)
    gmax, glane = first_argmax(jnp.where(is_g, lg, NEG_INF))
    gidx = glane - float(ROUTE_GROUP_LANE)
    gsum = jnp.sum(jnp.where(is_g, jnp.exp(lg - gmax), 0.0), axis=-1, keepdims=True)
    group_w = 1.0 / gsum
    lane_group = jnp.right_shift(lane_i, EXPERTS_PER_GROUP.bit_length() - 1).astype(F32)
    in_grp = (lane_i < N_EXPERTS) & (lane_group == gidx)
    el = jnp.where(in_grp, lg, NEG_INF)
    m1, e0 = first_argmax(el)
    m2, e1 = first_argmax(jnp.where(lane == e0, NEG_INF, el))
    r = jnp.exp(m2 - m1)
    w0 = group_w / (1.0 + r)
    w1 = group_w * r / (1.0 + r)

    pick0 = lane == e0
    pick1 = lane == e1
    onehot = jnp.where(pick0 | pick1, 1.0, 0.0)
    trow = _iota2((tm, tm), 0)
    tcol = _iota2((tm, tm), 1)
    before = jnp.where(trow > tcol, 1.0, 0.0).astype(BF16)
    cnt = _dot(before, onehot.astype(BF16)) + carry_scr[0:1, :]
    rank0 = jnp.sum(jnp.where(pick0, cnt, 0.0), axis=-1, keepdims=True)
    rank1 = jnp.sum(jnp.where(pick1, cnt, 0.0), axis=-1, keepdims=True)
    total = carry_scr[0:1, :] + jnp.sum(onehot, axis=0, keepdims=True)
    carry_scr[...] = jnp.broadcast_to(total, carry_scr.shape)
    cnt_ref[...] = jnp.broadcast_to(total, cnt_ref.shape).astype(jnp.int32)

    ri = jnp.where(lane_i == 0, e0, jnp.where(lane_i == 1, e1, 0.0))
    ri = jnp.where(lane_i == 2, rank0, jnp.where(lane_i == 3, rank1, ri))
    ri_ref[...] = jnp.transpose(ri)[0:SUBLANES, :].astype(jnp.int32)
    rw_ref[...] = jnp.where(lane_i == 0, w0, jnp.where(lane_i == 1, w1, 0.0))


def _outproj(ga, gb, x2, w_out, norm_w, wr_hi, wr_lo):
    n = x2.shape[0]
    tm = OUTPROJ_TM
    kern = functools.partial(_outproj_kernel, tm=tm)
    row_blk = lambda w: pl.BlockSpec((tm, w), lambda i: (i, 0))
    const = lambda shape: pl.BlockSpec(shape, lambda i: (0, 0))
    return pl.pallas_call(
        kern,
        grid=(n // tm,),
        in_specs=[
            row_blk(D_MODEL), row_blk(D_MODEL), row_blk(D_MODEL),
            const((D_MODEL, D_MODEL)), const((1, D_MODEL)),
            const((D_MODEL, LANES)), const((D_MODEL, LANES)),
        ],
        out_specs=[row_blk(D_MODEL),
                   pl.BlockSpec((tm * ROW_TILE, LANES), lambda i: (i, 0)),
                   pl.BlockSpec((SUBLANES, tm), lambda i: (0, i)),
                   row_blk(LANES),
                   const((SUBLANES, LANES))],
        out_shape=[
            jax.ShapeDtypeStruct((n, D_MODEL), F32),
            jax.ShapeDtypeStruct((n * ROW_TILE, LANES), F32),
            jax.ShapeDtypeStruct((SUBLANES, n), jnp.int32),
            jax.ShapeDtypeStruct((n, LANES), F32),
            jax.ShapeDtypeStruct((SUBLANES, LANES), jnp.int32),
        ],
        scratch_shapes=[pltpu.VMEM((SUBLANES, LANES), F32)],
        compiler_params=pltpu.CompilerParams(
            dimension_semantics=("arbitrary",), vmem_limit_bytes=VMEM_LIMIT),
        name="outproj",
    )(ga, gb, x2, w_out, norm_w, wr_hi, wr_lo)


def _row_copy(src_ref, src_row, dst_ref, dst_row, sem):
    src = src_ref.at[pl.ds(pl.multiple_of(src_row * ROW_TILE, ROW_TILE), ROW_TILE)]
    dst = dst_ref.at[pl.ds(pl.multiple_of(dst_row * ROW_TILE, ROW_TILE), ROW_TILE)]
    return pltpu.make_async_copy(src, dst, sem)


def _scatter_kernel(seg_ref, d0_ref, d1_ref, h2_ref, xs_ref, zero_scr, sem, zsem, *, tile):
    i = pl.program_id(0)

    def issue(t, carry):
        _row_copy(h2_ref, t, xs_ref, d0_ref[t], sem).start(priority=0)
        _row_copy(h2_ref, t, xs_ref, d1_ref[t], sem).start(priority=1)
        return carry

    lax.fori_loop(0, tile, issue, 0, unroll=DMA_UNROLL)

    @pl.when(i == 0)
    def _():
        zero_scr[...] = jnp.zeros_like(zero_scr)

        def per_expert(e, carry):
            lo, hi = seg_ref[0, e], seg_ref[1, e]

            def start(r, c2):
                _row_copy(zero_scr, 0, xs_ref, r, zsem).start()
                return c2

            def wait(r, c2):
                _row_copy(zero_scr, 0, xs_ref, r, zsem).wait()
                return c2

            lax.fori_loop(lo, hi, start, 0)
            lax.fori_loop(lo, hi, wait, 0)
            return carry

        lax.fori_loop(0, N_EXPERTS + 1, per_expert, 0)

    def drain(t, carry):
        _row_copy(h2_ref, 0, xs_ref, 0, sem).wait()
        _row_copy(h2_ref, 0, xs_ref, 0, sem).wait()
        return carry

    lax.fori_loop(0, tile, drain, 0, unroll=DMA_UNROLL)


def _scatter(seg, dest0, dest1, h2t, n_rows):
    n = dest0.shape[0]
    tile = SCATTER_T
    kern = functools.partial(_scatter_kernel, tile=tile)
    return pl.pallas_call(
        kern,
        grid=(n // tile,),
        in_specs=[
            pl.BlockSpec(memory_space=pltpu.SMEM),
            pl.BlockSpec((tile,), lambda i: (i,), memory_space=pltpu.SMEM),
            pl.BlockSpec((tile,), lambda i: (i,), memory_space=pltpu.SMEM),
            pl.BlockSpec((tile * ROW_TILE, LANES), lambda i: (i, 0)),
        ],
        out_specs=pl.BlockSpec(memory_space=pl.ANY),
        out_shape=jax.ShapeDtypeStruct((n_rows * ROW_TILE, LANES), F32),
        scratch_shapes=[pltpu.VMEM((ROW_TILE, LANES), F32),
                        pltpu.SemaphoreType.DMA, pltpu.SemaphoreType.DMA],
        compiler_params=pltpu.CompilerParams(
            dimension_semantics=("arbitrary",), vmem_limit_bytes=VMEM_LIMIT),
        name="scatter",
    )(seg, dest0, dest1, h2t)


def _expert_kernel(be_ref, nv_ref, xs_ref, wg_ref, wu_ref, wd_ref, y_ref):
    i = pl.program_id(0)

    @pl.when(i < nv_ref[0])
    def _():
        x = _load_row_tiles(xs_ref).astype(BF16)
        g = _dot(x, wg_ref[0])
        u = _dot(x, wu_ref[0])
        hid = (_silu(g) * u).astype(BF16)
        _store_row_tiles(y_ref, _dot(hid, wd_ref[0]))

    @pl.when(i >= nv_ref[0])
    def _():
        y_ref[...] = jnp.zeros_like(y_ref)


def _experts(block_expert, n_valid, xs, w_gate, w_up, w_down):
    blk = MOE_BLOCK
    n_rows = xs.shape[0] // ROW_TILE
    grid_spec = pltpu.PrefetchScalarGridSpec(
        num_scalar_prefetch=2,
        grid=(n_rows // blk,),
        in_specs=[
            pl.BlockSpec((blk * ROW_TILE, LANES), lambda i, be, nv: (jnp.minimum(i, nv[0] - 1), 0)),
            pl.BlockSpec((1, D_MODEL, D_EXPERT), lambda i, be, nv: (be[i], 0, 0)),
            pl.BlockSpec((1, D_MODEL, D_EXPERT), lambda i, be, nv: (be[i], 0, 0)),
            pl.BlockSpec((1, D_EXPERT, D_MODEL), lambda i, be, nv: (be[i], 0, 0)),
        ],
        out_specs=pl.BlockSpec((blk * ROW_TILE, LANES), lambda i, be, nv: (i, 0)),
    )
    return pl.pallas_call(
        _expert_kernel,
        grid_spec=grid_spec,
        out_shape=jax.ShapeDtypeStruct((n_rows * ROW_TILE, LANES), F32),
        compiler_params=pltpu.CompilerParams(
            dimension_semantics=("arbitrary",), vmem_limit_bytes=VMEM_LIMIT),
        name="experts",
    )(block_expert, n_valid, xs, w_gate, w_up, w_down)


def _combine_kernel(d0_ref, d1_ref, x1_ref, rw_ref, nw_ref, y_ref, out_ref, ya_scr, yb_scr, sem, *, tile):
    def issue(t, carry):
        _row_copy(y_ref, d0_ref[t], ya_scr, t, sem).start(priority=0)
        _row_copy(y_ref, d1_ref[t], yb_scr, t, sem).start(priority=1)
        return carry

    def drain(t, carry):
        _row_copy(y_ref, 0, ya_scr, 0, sem).wait()
        _row_copy(y_ref, 0, yb_scr, 0, sem).wait()
        return carry

    lax.fori_loop(0, tile, issue, 0, unroll=DMA_UNROLL)
    lax.fori_loop(0, tile, drain, 0, unroll=DMA_UNROLL)

    rw = rw_ref[...]
    moe = rw[:, 0:1] * _load_row_tiles(ya_scr) + rw[:, 1:2] * _load_row_tiles(yb_scr)
    x2 = x1_ref[...] + moe
    out_ref[...] = x2 * lax.rsqrt(jnp.mean(x2 * x2, axis=-1, keepdims=True) + EPS) * nw_ref[...]


def _combine(dest0, dest1, x1, rw, norm_w, y):
    n = x1.shape[0]
    tile = COMBINE_T
    kern = functools.partial(_combine_kernel, tile=tile)
    return pl.pallas_call(
        kern,
        grid=(n // tile,),
        in_specs=[
            pl.BlockSpec((tile,), lambda i: (i,), memory_space=pltpu.SMEM),
            pl.BlockSpec((tile,), lambda i: (i,), memory_space=pltpu.SMEM),
            pl.BlockSpec((tile, D_MODEL), lambda i: (i, 0)),
            pl.BlockSpec((tile, LANES), lambda i: (i, 0)),
            pl.BlockSpec((1, D_MODEL), lambda i: (0, 0)),
            pl.BlockSpec(memory_space=pl.ANY),
        ],
        out_specs=pl.BlockSpec((tile, D_MODEL), lambda i: (i, 0)),
        out_shape=jax.ShapeDtypeStruct((n, D_MODEL), F32),
        scratch_shapes=[
            pltpu.VMEM((tile * ROW_TILE, LANES), F32), pltpu.VMEM((tile * ROW_TILE, LANES), F32),
            pltpu.SemaphoreType.DMA,
        ],
        compiler_params=pltpu.CompilerParams(
            dimension_semantics=("arbitrary",), vmem_limit_bytes=VMEM_LIMIT),
        name="combine",
    )(dest0, dest1, x1, rw, norm_w, y)


def _pad_cols(w, width):
    return jnp.pad(w, ((0, 0), (0, width - w.shape[1])))


def _token_mixer_and_moe(x, norm1_w, w_in, w2_f, b_f, w2_b, b_b, gla_norm_w, conv_w, a_log_f, dt_bias_f,
                         a_log_b, dt_bias_b, gdn_norm_w, w_out, norm2_w, w_group, w_router, w_gate, w_up,
                         w_down, out_norm_w):
    batch, seq, d = x.shape
    n = batch * seq
    x2 = x.reshape(n, d)

    w_main = jnp.concatenate([w_in[:, :3072], w_in[:, 3104:7200], w_in[:, 7232:]], axis=1).astype(BF16)
    w_small = _pad_cols(jnp.concatenate([w_in[:, 3072:3104], w_in[:, 7200:7232]], axis=1), LANES)
    ws_hi, ws_lo = _split2(w_small)
    main, small = _inproj(x2, norm1_w.reshape(1, d), w_main, ws_hi, ws_lo)

    w2f_pad = jnp.zeros((LANES, GLA_HEADS * GLA_DK), F32).at[0:GLA_GATE_RANK].set(w2_f)
    w2b_pad = jnp.zeros((LANES, GLA_HEADS * GLA_DK), F32).at[GLA_GATE_RANK:2 * GLA_GATE_RANK].set(w2_b)
    ga = _gla(main, small, w2f_pad, w2b_pad, b_f.reshape(1, -1), b_b.reshape(1, -1),
              gla_norm_w.reshape(1, -1), batch, seq)

    gates = jnp.zeros((SUBLANES, LANES), F32)
    gates = gates.at[0, SMALL_AF:SMALL_BF].set(jnp.concatenate([a_log_f, a_log_b]))
    gates = gates.at[1, SMALL_AF:SMALL_BF].set(jnp.concatenate([dt_bias_f, dt_bias_b]))
    gb = _gdn(main, small, gates, conv_w, gdn_norm_w.reshape(1, -1), batch, seq)

    w_route = _pad_cols(jnp.concatenate([w_router, w_group], axis=1), LANES)
    wr_hi, wr_lo = _split2(w_route)
    x1, h2t, rt, rw, counts = _outproj(ga, gb, x2, w_out.astype(BF16), norm2_w.reshape(1, d), wr_hi, wr_lo)

    blk = MOE_BLOCK
    cnt = counts[0, :N_EXPERTS]
    padded = (cnt + blk - 1) // blk * blk
    ends = jnp.cumsum(padded)
    pstart = (ends - padded).astype(jnp.int32)
    n_blocks = -(-(2 * n + N_EXPERTS * (blk - 1)) // blk)
    n_rows = n_blocks * blk
    block_row = jnp.arange(n_blocks, dtype=jnp.int32) * blk
    block_expert = jnp.minimum(
        jnp.sum((ends[None, :] <= block_row[:, None]).astype(jnp.int32), axis=1), N_EXPERTS - 1)
    n_valid = (ends[-1:] // blk).astype(jnp.int32)
    seg = jnp.stack([jnp.append(pstart + cnt, ends[-1]), jnp.append(ends, n_rows)]).astype(jnp.int32)

    experts = jnp.arange(N_EXPERTS, dtype=jnp.int32)
    seg_start = jnp.sum(jnp.where(rt[0:2, :, None] == experts, pstart, 0), axis=-1)
    dest = seg_start + rt[2:4]
    dest0, dest1 = dest[0], dest[1]

    xs = _scatter(seg, dest0, dest1, h2t, n_rows)
    y = _experts(block_expert, n_valid, xs, w_gate.astype(BF16), w_up.astype(BF16), w_down.astype(BF16))
    out = _combine(dest0, dest1, x1, rw, out_norm_w.reshape(1, d), y)
    return out.reshape(batch, seq, d)


def kernel(x, norm1_w, w_in, gla_gate_w2_fwd, gla_gate_b_fwd, gla_gate_w2_bwd, gla_gate_b_bwd, gla_norm_w,
           gdn_conv_w, gdn_a_log_fwd, gdn_dt_bias_fwd, gdn_a_log_bwd, gdn_dt_bias_bwd, gdn_norm_w, w_out,
           norm2_w, moe_w_group, moe_w_router, moe_w_gate, moe_w_up, moe_w_down, norm_f_w):
    assert norm1_w.shape[0] == 1, "single-layer block"
    return _token_mixer_and_moe(
        x, norm1_w[0], w_in[0], gla_gate_w2_fwd[0], gla_gate_b_fwd[0], gla_gate_w2_bwd[0], gla_gate_b_bwd[0],
        gla_norm_w[0], gdn_conv_w[0], gdn_a_log_fwd[0], gdn_dt_bias_fwd[0], gdn_a_log_bwd[0],
        gdn_dt_bias_bwd[0], gdn_norm_w[0], w_out[0], norm2_w[0], moe_w_group[0], moe_w_router[0],
        moe_w_gate[0], moe_w_up[0], moe_w_down[0], norm_f_w)
```

```python
import functools

import jax
import jax.numpy as jnp
import numpy as np
from jax import lax
from jax.experimental import pallas as pl
from jax.experimental.pallas import tpu as pltpu

F32 = jnp.float32
BF16 = jnp.bfloat16
U32 = jnp.uint32

D_MODEL = 1024
GLA_HEADS = 4
GLA_DK = 128
GLA_DV = 256
GLA_GATE_RANK = 16
GLA_GATE_TAU = 16.0
GLA_CHUNK = 64
GLA_GROUP = 4
GDN_HEADS = 8
GDN_DK = 128
GDN_DV = 128
GDN_CONV = 5
GDN_CHUNK = 128
GDN_PREP_GROUP = 8
N_GROUPS = 4
EXPERTS_PER_GROUP = 8
N_EXPERTS = N_GROUPS * EXPERTS_PER_GROUP
D_EXPERT = 256
EPS = 1e-6

LANES = 128
SUBLANES = 8
VMEM_LIMIT = 48 * 1024 * 1024

COL_GQ, COL_GK, COL_GV, COL_GR = 0, 512, 1024, 2048
COL_DQ, COL_DK, COL_DV, COL_DZ = 3072, 4096, 5120, 6144
COL_MA, COL_MB = 7168, 8192
D_MAIN = 9216
SMALL_AF, SMALL_AB, SMALL_BF, SMALL_BB = 32, 40, 48, 56
ROUTE_GROUP_LANE = 32

MOE_BLOCK = 256
ROW_TILE = D_MODEL // 2 // LANES
HIGH_HALF = np.uint32(0xFFFF0000)
DMA_UNROLL = 8
INPROJ_TM, INPROJ_TN = 1024, 1024
OUTPROJ_TM = 512
SCATTER_T = 512
COMBINE_T = 256
CONV_ROWS = 256
NEG_INF = float("-inf")


def _dot(a, b):
    return jnp.dot(a, b, preferred_element_type=F32)


def _dot_nt(a, b):
    return lax.dot_general(a, b, (((1,), (1,)), ((), ())), preferred_element_type=F32)


def _dot_tn(a, b):
    return lax.dot_general(a, b, (((0,), (0,)), ((), ())), preferred_element_type=F32)


def _split2(x):
    hi = x.astype(BF16)
    lo = (x - hi.astype(F32)).astype(BF16)
    return hi, lo


def _split3(x):
    hi = x.astype(BF16)
    r = x - hi.astype(F32)
    mid = r.astype(BF16)
    lo = (r - mid.astype(F32)).astype(BF16)
    return hi, mid, lo


def _dot_exact_rhs(x, m_bf16):
    hi, mid, lo = _split3(x)
    return _dot(hi, m_bf16) + _dot(mid, m_bf16) + _dot(lo, m_bf16)


def _dot_exact_lhs(m_bf16, x):
    hi, mid, lo = _split3(x)
    return _dot(m_bf16, hi) + _dot(m_bf16, mid) + _dot(m_bf16, lo)


def _dot_lhs2(m_bf16, x):
    hi, lo = _split2(x)
    return _dot(m_bf16, hi) + _dot(m_bf16, lo)


def _dot_lhs2_wide(m2_bf16, x):
    return _dot(m2_bf16, jnp.concatenate(_split2(x), axis=0))


def _dot3(a, b):
    ah, al = _split2(a)
    bh, bl = _split2(b)
    return _dot(ah, bh) + _dot(al, bh) + _dot(ah, bl)


def _store_row_tiles(ref, x):
    rows = x.shape[0]
    half = D_MODEL // 2
    hi = lax.bitcast_convert_type(x[:, :half].astype(BF16).astype(F32), U32)
    lo = lax.bitcast_convert_type(x[:, half:].astype(BF16).astype(F32), U32)
    packed = jnp.bitwise_or(jnp.bitwise_and(hi, HIGH_HALF), jnp.right_shift(lo, 16))
    for j in range(ROW_TILE):
        ref[pl.ds(j, rows, stride=ROW_TILE), :] = packed[:, j * LANES:(j + 1) * LANES]


def _load_row_tiles(ref):
    rows = ref.shape[0] // ROW_TILE
    packed = jnp.concatenate([ref[pl.ds(j, rows, stride=ROW_TILE), :] for j in range(ROW_TILE)], axis=1)
    hi = lax.bitcast_convert_type(jnp.bitwise_and(packed, HIGH_HALF), F32)
    lo = lax.bitcast_convert_type(jnp.left_shift(packed, 16), F32)
    return jnp.concatenate([hi, lo], axis=1)


def _each(fn, *lists):
    return [fn(*args) for args in zip(*lists)]


def _sigmoid(x):
    return 1.0 / (1.0 + jnp.exp(-x))


def _silu(x):
    return x * _sigmoid(x)


def _softplus(x):
    return jnp.maximum(x, 0.0) + jnp.log(1.0 + jnp.exp(-jnp.abs(x)))


def _log_sigmoid(x):
    return jnp.minimum(x, 0.0) - jnp.log(1.0 + jnp.exp(-jnp.abs(x)))


def _iota2(shape, dim):
    return lax.broadcasted_iota(jnp.int32, shape, dim)


def _inproj_kernel(x_ref, nw_ref, w_ref, wsh_ref, wsl_ref, main_ref, small_ref, h_scr):
    @pl.when(pl.program_id(1) == 0)
    def _():
        x = x_ref[...]
        h = x * lax.rsqrt(jnp.mean(x * x, axis=-1, keepdims=True) + EPS) * nw_ref[...]
        hh, hl = _split2(h)
        h_scr[...] = hh
        small_ref[...] = _dot(hh, wsh_ref[...]) + _dot(hl, wsh_ref[...]) + _dot(hh, wsl_ref[...])

    main_ref[...] = _dot(h_scr[...], w_ref[...]).astype(BF16)


def _inproj(x2, norm_w, w_main, ws_hi, ws_lo):
    n = x2.shape[0]
    tm, tn = INPROJ_TM, INPROJ_TN
    return pl.pallas_call(
        _inproj_kernel,
        grid=(n // tm, D_MAIN // tn),
        in_specs=[
            pl.BlockSpec((tm, D_MODEL), lambda i, j: (i, 0)),
            pl.BlockSpec((1, D_MODEL), lambda i, j: (0, 0)),
            pl.BlockSpec((D_MODEL, tn), lambda i, j: (0, j)),
            pl.BlockSpec((D_MODEL, LANES), lambda i, j: (0, 0)),
            pl.BlockSpec((D_MODEL, LANES), lambda i, j: (0, 0)),
        ],
        out_specs=[
            pl.BlockSpec((tm, tn), lambda i, j: (i, j)),
            pl.BlockSpec((tm, LANES), lambda i, j: (i, 0)),
        ],
        out_shape=[
            jax.ShapeDtypeStruct((n, D_MAIN), BF16),
            jax.ShapeDtypeStruct((n, LANES), F32),
        ],
        scratch_shapes=[pltpu.VMEM((tm, D_MODEL), BF16)],
        compiler_params=pltpu.CompilerParams(
            dimension_semantics=("arbitrary", "arbitrary"), vmem_limit_bytes=VMEM_LIMIT),
        name="inproj",
    )(x2, norm_w, w_main, ws_hi, ws_lo)


def _gla_kernel(q_ref, k_ref, v_ref, gr_ref, ma_ref, small_ref, w2f_ref, w2b_ref, bf_ref, bb_ref,
                nw_ref, out_ref, laf_scr, lab_scr, o_scr, stf_scr, stb_scr, *, seq, chunk):
    c = chunk
    n = seq // c
    scale = GLA_DK ** -0.5

    sm = small_ref[...]
    laf_scr[...] = _log_sigmoid(_dot3(sm, w2f_ref[...]) + bf_ref[...]) * (1.0 / GLA_GATE_TAU)
    lab_scr[...] = _log_sigmoid(_dot3(sm, w2b_ref[...]) + bb_ref[...]) * (1.0 / GLA_GATE_TAU)
    stf_scr[...] = jnp.zeros_like(stf_scr)
    stb_scr[...] = jnp.zeros_like(stb_scr)

    row = _iota2((c, c), 0)
    col = _iota2((c, c), 1)
    low = row >= col
    upp = row <= col
    low_m = jnp.concatenate([jnp.where(low, 1.0, 0.0).astype(BF16)] * 2, axis=1)
    upp_m = jnp.concatenate([jnp.where(upp, 1.0, 0.0).astype(BF16)] * 2, axis=1)

    g = GLA_GROUP

    def finish(rows, o):
        y = o * lax.rsqrt(jnp.mean(o * o, axis=-1, keepdims=True) + EPS) * nw_ref[...]
        y = y * _silu(gr_ref[rows, :].astype(F32))
        y = y * _sigmoid(ma_ref[rows, :].astype(F32))
        out_ref[rows, :] = y.astype(BF16)

    def group(gi, second_touch):
        ids = [gi * g + j for j in range(g)] + [n - 1 - gi * g - j for j in range(g)]
        rows = [pl.ds(pl.multiple_of(i * c, c), c) for i in ids]
        la = [laf_scr[r, :] for r in rows[:g]] + [lab_scr[r, :] for r in rows[g:]]
        csum = [low_m] * g + [upp_m] * g
        mask = [low] * g + [upp] * g
        tot_row = [c - 1] * g + [0] * g
        qf = [q_ref[r, :].astype(F32) * scale for r in rows]
        kf = [k_ref[r, :].astype(F32) for r in rows]
        vc = [v_ref[r, :] for r in rows]

        cum = _each(_dot_lhs2_wide, csum, la)
        tot = _each(lambda x, r: x[r:r + 1, :], cum, tot_row)
        q_dec = _each(lambda q, x: (q * jnp.exp(x)).astype(BF16), qf, cum)
        k_inv = _each(lambda k, x: (k * jnp.exp(-x)).astype(BF16), kf, cum)
        k_tail = _each(lambda k, t, x: (k * jnp.exp(t - x)).astype(BF16), kf, tot, cum)
        s = _each(lambda m, q, k: jnp.where(m, _dot_nt(q, k), 0.0).astype(BF16), mask, q_dec, k_inv)
        o = _each(_dot, s, vc)
        kv = _each(_dot_tn, vc, k_tail)
        dec = _each(jnp.exp, tot)

        for st_scr, probs in ((stf_scr, range(g)), (stb_scr, range(g, 2 * g))):
            st = st_scr[...]
            for p in probs:
                o[p] = o[p] + _dot_nt(q_dec[p], st.astype(BF16))
                st = dec[p] * st + kv[p]
            st_scr[...] = st

        for r, o_p in zip(rows, o):
            if second_touch:
                finish(r, o_scr[r, :] + o_p)
            else:
                o_scr[r, :] = o_p

    def first_half(gi, carry):
        group(gi, False)
        return carry

    def second_half(gi, carry):
        group(gi, True)
        return carry

    n_groups = n // g
    lax.fori_loop(0, n_groups // 2, first_half, 0)
    lax.fori_loop(n_groups // 2, n_groups, second_half, 0)


def _gla(main, small, w2f_pad, w2b_pad, b_f, b_b, norm_w, batch, seq):
    n = batch * seq
    h = GLA_HEADS
    kern = functools.partial(_gla_kernel, seq=seq, chunk=GLA_CHUNK)
    qk_blk = lambda off: pl.BlockSpec((seq, GLA_DK), lambda b, hh, off=off: (b, off // GLA_DK + hh))
    v_blk = lambda off: pl.BlockSpec((seq, GLA_DV), lambda b, hh, off=off: (b, off // GLA_DV + hh))
    return pl.pallas_call(
        kern,
        grid=(batch, h),
        in_specs=[
            qk_blk(COL_GQ), qk_blk(COL_GK), v_blk(COL_GV), v_blk(COL_GR), v_blk(COL_MA),
            pl.BlockSpec((seq, LANES), lambda b, hh: (b, 0)),
            pl.BlockSpec((LANES, GLA_DK), lambda b, hh: (0, hh)),
            pl.BlockSpec((LANES, GLA_DK), lambda b, hh: (0, hh)),
            pl.BlockSpec((1, GLA_DK), lambda b, hh: (0, hh)),
            pl.BlockSpec((1, GLA_DK), lambda b, hh: (0, hh)),
            pl.BlockSpec((1, GLA_DV), lambda b, hh: (0, 0)),
        ],
        out_specs=pl.BlockSpec((seq, GLA_DV), lambda b, hh: (b, hh)),
        out_shape=jax.ShapeDtypeStruct((n, D_MODEL), BF16),
        scratch_shapes=[
            pltpu.VMEM((seq, GLA_DK), F32), pltpu.VMEM((seq, GLA_DK), F32),
            pltpu.VMEM((seq, GLA_DV), F32),
            pltpu.VMEM((GLA_DV, GLA_DK), F32), pltpu.VMEM((GLA_DV, GLA_DK), F32),
        ],
        compiler_params=pltpu.CompilerParams(
            dimension_semantics=("arbitrary", "arbitrary"), vmem_limit_bytes=VMEM_LIMIT),
        name="gla",
    )(main, main, main, main, main, small, w2f_pad, w2b_pad, b_f, b_b, norm_w)


TRI_BLOCK = 16


def _mm(a, b):
    return _dot(a.astype(BF16), b.astype(BF16))


def _nilpotent_inverse(a_list, eye, index, tick):
    t_list = _each(lambda a: eye - a, a_list)
    p_list = a_list
    power = 2
    while power < index:
        p_list = _each(lambda p: _mm(p, p), p_list)
        tick()
        t_list = _each(lambda t, p: t + _mm(t, p), t_list, p_list)
        tick()
        power *= 2
    return t_list


def _tri_inverse(a_list, eye, diag_blocks, chunk, tick):
    ad_list = _each(lambda a: jnp.where(diag_blocks, a, 0.0), a_list)
    ao_list = _each(lambda a: jnp.where(diag_blocks, 0.0, a), a_list)
    d_list = _nilpotent_inverse(ad_list, eye, TRI_BLOCK, tick)
    n_list = _each(_mm, d_list, ao_list)
    tick()
    t_list = _nilpotent_inverse(n_list, eye, chunk // TRI_BLOCK, tick)
    out = _each(_mm, t_list, d_list)
    tick()
    return out


def _gdn_kernel(gate_ref, q_ref, k_ref, v_ref, z_ref, mb_ref, small_ref, cwq_ref, cwk_ref, cwv_ref,
                nw_ref, out_ref, pad_scr, qs_scr, ks_scr, vs_scr, gf_scr, gb_scr, btf_scr, btb_scr,
                o_scr, nmat_scr, bmat_scr, qp_scr, cd_scr, sf_scr, sb_scr, *, seq, chunk, n_heads_total):
    c = chunk
    n = seq // c
    step = pl.program_id(0)
    hh = lax.rem(jnp.minimum(step, n_heads_total - 1), GDN_HEADS)
    cur = lax.rem(step, 2)
    prev = 1 - cur
    scale = GDN_DK ** -0.5

    @pl.when(step == 0)
    def _():
        nmat_scr[...] = jnp.zeros_like(nmat_scr)
        bmat_scr[...] = jnp.zeros_like(bmat_scr)
        qp_scr[...] = jnp.zeros_like(qp_scr)
        cd_scr[...] = jnp.zeros_like(cd_scr)
        o_scr[...] = jnp.zeros_like(o_scr)

    zeros8 = jnp.zeros((SUBLANES, LANES), F32)
    pad_scr[0:SUBLANES, :] = zeros8
    pad_scr[seq + SUBLANES:seq + 2 * SUBLANES, :] = zeros8
    half = GDN_CONV // 2

    def conv_into(src_ref, cw_ref, dst_ref, normalise, mult):
        pad_scr[SUBLANES:seq + SUBLANES, :] = src_ref[...].astype(F32)
        w = cw_ref[...]

        def body(i, carry):
            r0 = pl.multiple_of(i * CONV_ROWS, CONV_ROWS)
            acc = jnp.zeros((CONV_ROWS, LANES), F32)
            for j in range(GDN_CONV):
                tap = pad_scr[pl.ds(r0 + (SUBLANES - half + j), CONV_ROWS), :]
                acc = acc + tap * w[j:j + 1, :]
            y = _silu(acc)
            if normalise:
                y = y * lax.rsqrt(jnp.sum(y * y, axis=-1, keepdims=True) + EPS) * mult
            dst_ref[pl.ds(r0, CONV_ROWS), :] = y.astype(BF16)
            return carry

        lax.fori_loop(0, seq // CONV_ROWS, body, 0)

    conv_into(q_ref, cwq_ref, qs_scr, True, scale)
    conv_into(k_ref, cwk_ref, ks_scr, True, 1.0)
    conv_into(v_ref, cwv_ref, vs_scr, False, 1.0)

    sm = small_ref[...]
    lane = _iota2(sm.shape, 1)
    log_decay = -jnp.exp(gate_ref[0:1, :]) * _softplus(sm + gate_ref[1:2, :])
    gate_vals = jnp.where(lane < SMALL_BF, log_decay, _sigmoid(sm))
    gate_hl = jnp.concatenate(_split2(gate_vals), axis=1)
    sel_lane = jnp.bitwise_and(_iota2((2 * LANES, 4 * LANES), 0), LANES - 1)
    sel_gate = jnp.right_shift(_iota2((2 * LANES, 4 * LANES), 1), LANES.bit_length() - 1)
    sel = jnp.where(sel_lane == SMALL_AF + GDN_HEADS * sel_gate + hh, 1.0, 0.0).astype(BF16)
    spread = _dot(gate_hl, sel)
    gf_scr[...] = spread[:, 0 * LANES:1 * LANES]
    gb_scr[...] = spread[:, 1 * LANES:2 * LANES]
    btf_scr[...] = spread[:, 2 * LANES:3 * LANES]
    btb_scr[...] = spread[:, 3 * LANES:4 * LANES]
    sf_scr[...] = jnp.zeros_like(sf_scr)
    sb_scr[...] = jnp.zeros_like(sb_scr)

    row = _iota2((c, c), 0)
    col = _iota2((c, c), 1)
    eye = jnp.where(row == col, 1.0, 0.0).astype(F32)
    low, slow = row >= col, row > col
    upp, supp = row <= col, row < col
    low_m = jnp.concatenate([jnp.where(low, 1.0, 0.0).astype(BF16)] * 2, axis=1)
    upp_m = jnp.concatenate([jnp.where(upp, 1.0, 0.0).astype(BF16)] * 2, axis=1)

    tri_shift = TRI_BLOCK.bit_length() - 1
    diag_blocks = jnp.right_shift(row, tri_shift) == jnp.right_shift(col, tri_shift)

    cur_slot, prev_slot = cur * (2 * n), prev * (2 * n)
    cur_row, prev_row = cur * seq, prev * seq

    def scan_step(i):
        j = n - 1 - i
        slots = [prev_slot + i, prev_slot + n + j]
        rows = [pl.ds(pl.multiple_of(prev_row + i * c, c), c), pl.ds(pl.multiple_of(prev_row + j * c, c), c)]
        states = [sf_scr, sb_scr]
        s = [ref[...] for ref in states]
        s_b = _each(lambda x: x.astype(BF16), s)
        ns = _each(lambda sl, x: _dot(nmat_scr[sl], x), slots, s_b)
        qs = _each(lambda sl, x: _dot(qp_scr[sl], x), slots, s_b)
        for ref, sl, s_, ns_ in zip(states, slots, s, ns):
            ref[...] = cd_scr[sl][0:1, :] * s_ + (bmat_scr[sl] - ns_)
        for r, q in zip(rows, qs):
            o_scr[r, :] += q

    n_groups = n // GDN_PREP_GROUP
    scans_per_group = n // n_groups

    def prep_group(gi, carry):
        pending = [gi * scans_per_group + j for j in range(scans_per_group)]

        def tick():
            if pending:
                scan_step(pending.pop(0))

        chunk_ids = [gi * GDN_PREP_GROUP + j for j in range(GDN_PREP_GROUP)]
        rows = [pl.ds(pl.multiple_of(i * c, c), c) for i in chunk_ids]
        qc = [qs_scr[r, :] for r in rows]
        kc = [ks_scr[r, :] for r in rows]
        vc = [vs_scr[r, :] for r in rows]
        gl = [ref[r, :] for r in rows for ref in (gf_scr, gb_scr)]
        bt = [ref[r, :] for r in rows for ref in (btf_scr, btb_scr)]
        csum = [low_m, upp_m] * GDN_PREP_GROUP
        incl = [low, upp] * GDN_PREP_GROUP
        strict = [slow, supp] * GDN_PREP_GROUP
        tot_row = [c - 1, 0] * GDN_PREP_GROUP

        def both(per_chunk):
            return [x for x in per_chunk for _ in range(2)]

        kk = both(_each(_dot_nt, kc, kc))
        qk = both(_each(_dot_nt, qc, kc))
        tick()
        qf = both(_each(lambda x: x.astype(F32), qc))
        kf = both(_each(lambda x: x.astype(F32), kc))
        vf = both(_each(lambda x: x.astype(F32), vc))

        gc = _each(_dot_lhs2_wide, csum, gl)
        tick()
        tot = _each(lambda g, r: g[r:r + 1, :], gc, tot_row)
        e = _each(lambda m, g, st: jnp.exp(_dot_lhs2_wide(m, jnp.where(st, g[:, :c], 0.0))), csum, gl, strict)
        tick()
        a = _each(lambda kk_, b, e_, st: kk_ * b[:, :c] * jnp.where(st, e_, 0.0), kk, bt, e, strict)
        t_inv = _tri_inverse(a, eye, diag_blocks, c, tick)
        egc = _each(jnp.exp, gc)
        wu = _each(lambda t, k, v, b, eg: _mm(t, jnp.concatenate([k * b * eg, v * b], axis=1)).astype(BF16),
                   t_inv, kf, vf, bt, egc)
        tick()
        attn = _each(lambda qk_, e_, inc: (qk_ * jnp.where(inc, e_, 0.0)).astype(BF16), qk, e, incl)
        k_tail = _each(lambda k, t, g: (k * jnp.exp(t - g)).astype(BF16), kf, tot, gc)
        kwu = _each(_dot_tn, k_tail, wu)
        tick()
        awu = _each(_dot, attn, wu)
        while pending:
            tick()

        for p in range(2 * GDN_PREP_GROUP):
            slot = cur_slot + chunk_ids[p // 2] + (p % 2) * n
            nmat_scr[slot] = kwu[p][:, :GDN_DK].astype(BF16)
            bmat_scr[slot] = kwu[p][:, GDN_DK:]
            qp_scr[slot] = (qf[p] * egc[p] - awu[p][:, :GDN_DK]).astype(BF16)
            cd_scr[slot] = jnp.broadcast_to(jnp.exp(tot[p]), (SUBLANES, LANES))
        for j, i in enumerate(chunk_ids):
            r = pl.ds(pl.multiple_of(cur_row + i * c, c), c)
            o_scr[r, :] = awu[2 * j][:, GDN_DK:] + awu[2 * j + 1][:, GDN_DK:]
        return carry

    lax.fori_loop(0, n_groups, prep_group, 0)

    def finish(i, carry):
        r0 = pl.multiple_of(i * CONV_ROWS, CONV_ROWS)
        o = o_scr[pl.ds(pl.multiple_of(prev_row + r0, CONV_ROWS), CONV_ROWS), :]
        y = o * lax.rsqrt(jnp.mean(o * o, axis=-1, keepdims=True) + EPS) * nw_ref[...]
        y = y * _silu(z_ref[pl.ds(r0, CONV_ROWS), :].astype(F32))
        y = y * _sigmoid(mb_ref[pl.ds(r0, CONV_ROWS), :].astype(F32))
        out_ref[pl.ds(r0, CONV_ROWS), :] = y.astype(BF16)
        return carry

    lax.fori_loop(0, seq // CONV_ROWS, finish, 0)


def _gdn(main, small, gates, conv_w, norm_w, batch, seq):
    n = batch * seq
    total = batch * GDN_HEADS
    kern = functools.partial(_gdn_kernel, seq=seq, chunk=GDN_CHUNK, n_heads_total=total)
    n_chunks = seq // GDN_CHUNK

    def head_of(step):
        idx = jnp.minimum(step, total - 1)
        return idx // GDN_HEADS, idx % GDN_HEADS

    def prev_head_of(step):
        idx = jnp.maximum(step - 1, 0)
        return idx // GDN_HEADS, idx % GDN_HEADS

    def blk(off, which):
        def index(s):
            b, hh = which(s)
            return b, off // LANES + hh
        return pl.BlockSpec((seq, LANES), index)

    def cw(part):
        return pl.BlockSpec((GDN_CONV, LANES), lambda s: (0, part * GDN_HEADS + head_of(s)[1]))

    seq_f32 = lambda: pltpu.VMEM((seq, LANES), F32)
    seq_bf16 = lambda: pltpu.VMEM((seq, LANES), BF16)
    return pl.pallas_call(
        kern,
        grid=(total + 1,),
        in_specs=[
            pl.BlockSpec((SUBLANES, LANES), lambda s: (0, 0)),
            blk(COL_DQ, head_of), blk(COL_DK, head_of), blk(COL_DV, head_of),
            blk(COL_DZ, prev_head_of), blk(COL_MB, prev_head_of),
            pl.BlockSpec((seq, LANES), lambda s: (head_of(s)[0], 0)),
            cw(0), cw(1), cw(2),
            pl.BlockSpec((1, GDN_DV), lambda s: (0, 0)),
        ],
        out_specs=pl.BlockSpec((seq, GDN_DV), lambda s: prev_head_of(s)),
        out_shape=jax.ShapeDtypeStruct((n, D_MODEL), BF16),
        scratch_shapes=[
            pltpu.VMEM((seq + 2 * SUBLANES, LANES), F32),
            seq_bf16(), seq_bf16(), seq_bf16(),
            seq_f32(), seq_f32(), seq_f32(), seq_f32(),
            pltpu.VMEM((2 * seq, LANES), F32),
            pltpu.VMEM((4 * n_chunks, GDN_DK, GDN_DK), BF16),
            pltpu.VMEM((4 * n_chunks, GDN_DK, GDN_DV), F32),
            pltpu.VMEM((4 * n_chunks, GDN_CHUNK, GDN_DK), BF16),
            pltpu.VMEM((4 * n_chunks, SUBLANES, LANES), F32),
            pltpu.VMEM((GDN_DK, GDN_DV), F32), pltpu.VMEM((GDN_DK, GDN_DV), F32),
        ],
        compiler_params=pltpu.CompilerParams(
            dimension_semantics=("arbitrary",), vmem_limit_bytes=VMEM_LIMIT),
        name="gdn",
    )(gates, main, main, main, main, main, small, conv_w, conv_w, conv_w, norm_w)


def _outproj_kernel(ga_ref, gb_ref, x_ref, wo_ref, nw_ref, wrh_ref, wrl_ref,
                    x1_ref, h2_ref, ri_ref, rw_ref, cnt_ref, carry_scr, *, tm):
    @pl.when(pl.program_id(0) == 0)
    def _():
        carry_scr[...] = jnp.zeros_like(carry_scr)

    mixed = (ga_ref[...].astype(F32) + gb_ref[...].astype(F32)).astype(BF16)
    x1 = x_ref[...] + _dot(mixed, wo_ref[...])
    x1_ref[...] = x1
    h2 = x1 * lax.rsqrt(jnp.mean(x1 * x1, axis=-1, keepdims=True) + EPS) * nw_ref[...]
    _store_row_tiles(h2_ref, h2)
    hh, hl = _split2(h2)
    lg = _dot(hh, wrh_ref[...]) + _dot(hl, wrh_ref[...]) + _dot(hh, wrl_ref[...])

    lane_i = _iota2((tm, LANES), 1)
    lane = lane_i.astype(F32)

    def first_argmax(vals):
        m = jnp.max(vals, axis=-1, keepdims=True)
        idx = jnp.min(jnp.where(vals == m, lane, float(LANES)), axis=-1, keepdims=True)
        return m, idx

    is_g = (lane_i >= ROUTE_GROUP_LANE) & (lane_i < ROUTE_GROUP_LANE + N_GROUPS)
    gmax, glane = first_argmax(jnp.where(is_g, lg, NEG_INF))
    gidx = glane - float(ROUTE_GROUP_LANE)
    gsum = jnp.sum(jnp.where(is_g, jnp.exp(lg - gmax), 0.0), axis=-1, keepdims=True)
    group_w = 1.0 / gsum
    lane_group = jnp.right_shift(lane_i, EXPERTS_PER_GROUP.bit_length() - 1).astype(F32)
    in_grp = (lane_i < N_EXPERTS) & (lane_group == gidx)
    el = jnp.where(in_grp, lg, NEG_INF)
    m1, e0 = first_argmax(el)
    m2, e1 = first_argmax(jnp.where(lane == e0, NEG_INF, el))
    r = jnp.exp(m2 - m1)
    w0 = group_w / (1.0 + r)
    w1 = group_w * r / (1.0 + r)

    pick0 = lane == e0
    pick1 = lane == e1
    onehot = jnp.where(pick0 | pick1, 1.0, 0.0)
    trow = _iota2((tm, tm), 0)
    tcol = _iota2((tm, tm), 1)
    before = jnp.where(trow > tcol, 1.0, 0.0).astype(BF16)
    cnt = _dot(before, onehot.astype(BF16)) + carry_scr[0:1, :]
    rank0 = jnp.sum(jnp.where(pick0, cnt, 0.0), axis=-1, keepdims=True)
    rank1 = jnp.sum(jnp.where(pick1, cnt, 0.0), axis=-1, keepdims=True)
    total = carry_scr[0:1, :] + jnp.sum(onehot, axis=0, keepdims=True)
    carry_scr[...] = jnp.broadcast_to(total, carry_scr.shape)
    cnt_ref[...] = jnp.broadcast_to(total, cnt_ref.shape).astype(jnp.int32)

    ri = jnp.where(lane_i == 0, e0, jnp.where(lane_i == 1, e1, 0.0))
    ri = jnp.where(lane_i == 2, rank0, jnp.where(lane_i == 3, rank1, ri))
    ri_ref[...] = jnp.transpose(ri)[0:SUBLANES, :].astype(jnp.int32)
    rw_ref[...] = jnp.where(lane_i == 0, w0, jnp.where(lane_i == 1, w1, 0.0))


def _outproj(ga, gb, x2, w_out, norm_w, wr_hi, wr_lo):
    n = x2.shape[0]
    tm = OUTPROJ_TM
    kern = functools.partial(_outproj_kernel, tm=tm)
    row_blk = lambda w: pl.BlockSpec((tm, w), lambda i: (i, 0))
    const = lambda shape: pl.BlockSpec(shape, lambda i: (0, 0))
    return pl.pallas_call(
        kern,
        grid=(n // tm,),
        in_specs=[
            row_blk(D_MODEL), row_blk(D_MODEL), row_blk(D_MODEL),
            const((D_MODEL, D_MODEL)), const((1, D_MODEL)),
            const((D_MODEL, LANES)), const((D_MODEL, LANES)),
        ],
        out_specs=[row_blk(D_MODEL),
                   pl.BlockSpec((tm * ROW_TILE, LANES), lambda i: (i, 0)),
                   pl.BlockSpec((SUBLANES, tm), lambda i: (0, i)),
                   row_blk(LANES),
                   const((SUBLANES, LANES))],
        out_shape=[
            jax.ShapeDtypeStruct((n, D_MODEL), F32),
            jax.ShapeDtypeStruct((n * ROW_TILE, LANES), U32),
            jax.ShapeDtypeStruct((SUBLANES, n), jnp.int32),
            jax.ShapeDtypeStruct((n, LANES), F32),
            jax.ShapeDtypeStruct((SUBLANES, LANES), jnp.int32),
        ],
        scratch_shapes=[pltpu.VMEM((SUBLANES, LANES), F32)],
        compiler_params=pltpu.CompilerParams(
            dimension_semantics=("arbitrary",), vmem_limit_bytes=VMEM_LIMIT),
        name="outproj",
    )(ga, gb, x2, w_out, norm_w, wr_hi, wr_lo)


def _row_copy(src_ref, src_row, dst_ref, dst_row, sem):
    src = src_ref.at[pl.ds(pl.multiple_of(src_row * ROW_TILE, ROW_TILE), ROW_TILE)]
    dst = dst_ref.at[pl.ds(pl.multiple_of(dst_row * ROW_TILE, ROW_TILE), ROW_TILE)]
    return pltpu.make_async_copy(src, dst, sem)


def _scatter_kernel(seg_ref, d0_ref, d1_ref, h2_ref, xs_ref, zero_scr, sem, zsem, *, tile):
    i = pl.program_id(0)

    def issue(t, carry):
        _row_copy(h2_ref, t, xs_ref, d0_ref[t], sem).start(priority=0)
        _row_copy(h2_ref, t, xs_ref, d1_ref[t], sem).start(priority=1)
        return carry

    lax.fori_loop(0, tile, issue, 0, unroll=DMA_UNROLL)

    @pl.when(i == 0)
    def _():
        zero_scr[...] = jnp.zeros_like(zero_scr)

        def per_expert(e, carry):
            lo, hi = seg_ref[0, e], seg_ref[1, e]

            def start(r, c2):
                _row_copy(zero_scr, 0, xs_ref, r, zsem).start()
                return c2

            def wait(r, c2):
                _row_copy(zero_scr, 0, xs_ref, r, zsem).wait()
                return c2

            lax.fori_loop(lo, hi, start, 0)
            lax.fori_loop(lo, hi, wait, 0)
            return carry

        lax.fori_loop(0, N_EXPERTS + 1, per_expert, 0)

    def drain(t, carry):
        _row_copy(h2_ref, 0, xs_ref, 0, sem).wait()
        _row_copy(h2_ref, 0, xs_ref, 0, sem).wait()
        return carry

    lax.fori_loop(0, tile, drain, 0, unroll=DMA_UNROLL)


def _scatter(seg, dest0, dest1, h2t, n_rows):
    n = dest0.shape[0]
    tile = SCATTER_T
    kern = functools.partial(_scatter_kernel, tile=tile)
    return pl.pallas_call(
        kern,
        grid=(n // tile,),
        in_specs=[
            pl.BlockSpec(memory_space=pltpu.SMEM),
            pl.BlockSpec((tile,), lambda i: (i,), memory_space=pltpu.SMEM),
            pl.BlockSpec((tile,), lambda i: (i,), memory_space=pltpu.SMEM),
            pl.BlockSpec((tile * ROW_TILE, LANES), lambda i: (i, 0)),
        ],
        out_specs=pl.BlockSpec(memory_space=pl.ANY),
        out_shape=jax.ShapeDtypeStruct((n_rows * ROW_TILE, LANES), U32),
        scratch_shapes=[pltpu.VMEM((SUBLANES, LANES), U32),
                        pltpu.SemaphoreType.DMA, pltpu.SemaphoreType.DMA],
        compiler_params=pltpu.CompilerParams(
            dimension_semantics=("arbitrary",), vmem_limit_bytes=VMEM_LIMIT),
        name="scatter",
    )(seg, dest0, dest1, h2t)


def _expert_kernel(be_ref, nv_ref, xs_ref, wg_ref, wu_ref, wd_ref, y_ref):
    i = pl.program_id(0)

    @pl.when(i < nv_ref[0])
    def _():
        x = _load_row_tiles(xs_ref).astype(BF16)
        g = _dot(x, wg_ref[0].astype(BF16))
        u = _dot(x, wu_ref[0].astype(BF16))
        hid = (_silu(g) * u).astype(BF16)
        _store_row_tiles(y_ref, _dot(hid, wd_ref[0].astype(BF16)))

    @pl.when(i >= nv_ref[0])
    def _():
        y_ref[...] = jnp.zeros_like(y_ref)


def _experts(block_expert, n_valid, xs, w_gate, w_up, w_down):
    blk = MOE_BLOCK
    n_rows = xs.shape[0] // ROW_TILE
    grid_spec = pltpu.PrefetchScalarGridSpec(
        num_scalar_prefetch=2,
        grid=(n_rows // blk,),
        in_specs=[
            pl.BlockSpec((blk * ROW_TILE, LANES), lambda i, be, nv: (jnp.minimum(i, nv[0] - 1), 0)),
            pl.BlockSpec((1, D_MODEL, D_EXPERT), lambda i, be, nv: (be[i], 0, 0)),
            pl.BlockSpec((1, D_MODEL, D_EXPERT), lambda i, be, nv: (be[i], 0, 0)),
            pl.BlockSpec((1, D_EXPERT, D_MODEL), lambda i, be, nv: (be[i], 0, 0)),
        ],
        out_specs=pl.BlockSpec((blk * ROW_TILE, LANES), lambda i, be, nv: (i, 0)),
    )
    return pl.pallas_call(
        _expert_kernel,
        grid_spec=grid_spec,
        out_shape=jax.ShapeDtypeStruct((n_rows * ROW_TILE, LANES), U32),
        compiler_params=pltpu.CompilerParams(
            dimension_semantics=("arbitrary",), vmem_limit_bytes=VMEM_LIMIT),
        name="experts",
    )(block_expert, n_valid, xs, w_gate, w_up, w_down)


def _combine_kernel(d0_ref, d1_ref, x1_ref, rw_ref, nw_ref, y_ref, out_ref, ya_scr, yb_scr, sem, *, tile):
    def issue(t, carry):
        _row_copy(y_ref, d0_ref[t], ya_scr, t, sem).start(priority=0)
        _row_copy(y_ref, d1_ref[t], yb_scr, t, sem).start(priority=1)
        return carry

    def drain(t, carry):
        _row_copy(y_ref, 0, ya_scr, 0, sem).wait()
        _row_copy(y_ref, 0, yb_scr, 0, sem).wait()
        return carry

    lax.fori_loop(0, tile, issue, 0, unroll=DMA_UNROLL)
    lax.fori_loop(0, tile, drain, 0, unroll=DMA_UNROLL)

    rw = rw_ref[...]
    moe = rw[:, 0:1] * _load_row_tiles(ya_scr) + rw[:, 1:2] * _load_row_tiles(yb_scr)
    x2 = x1_ref[...] + moe
    out_ref[...] = x2 * lax.rsqrt(jnp.mean(x2 * x2, axis=-1, keepdims=True) + EPS) * nw_ref[...]


def _combine(dest0, dest1, x1, rw, norm_w, y):
    n = x1.shape[0]
    tile = COMBINE_T
    kern = functools.partial(_combine_kernel, tile=tile)
    return pl.pallas_call(
        kern,
        grid=(n // tile,),
        in_specs=[
            pl.BlockSpec((tile,), lambda i: (i,), memory_space=pltpu.SMEM),
            pl.BlockSpec((tile,), lambda i: (i,), memory_space=pltpu.SMEM),
            pl.BlockSpec((tile, D_MODEL), lambda i: (i, 0)),
            pl.BlockSpec((tile, LANES), lambda i: (i, 0)),
            pl.BlockSpec((1, D_MODEL), lambda i: (0, 0)),
            pl.BlockSpec(memory_space=pl.ANY),
        ],
        out_specs=pl.BlockSpec((tile, D_MODEL), lambda i: (i, 0)),
        out_shape=jax.ShapeDtypeStruct((n, D_MODEL), F32),
        scratch_shapes=[
            pltpu.VMEM((tile * ROW_TILE, LANES), U32), pltpu.VMEM((tile * ROW_TILE, LANES), U32),
            pltpu.SemaphoreType.DMA,
        ],
        compiler_params=pltpu.CompilerParams(
            dimension_semantics=("arbitrary",), vmem_limit_bytes=VMEM_LIMIT),
        name="combine",
    )(dest0, dest1, x1, rw, norm_w, y)


def _pad_cols(w, width):
    return jnp.pad(w, ((0, 0), (0, width - w.shape[1])))


def _token_mixer_and_moe(x, norm1_w, w_in, w2_f, b_f, w2_b, b_b, gla_norm_w, conv_w, a_log_f, dt_bias_f,
                         a_log_b, dt_bias_b, gdn_norm_w, w_out, norm2_w, w_group, w_router, w_gate, w_up,
                         w_down, out_norm_w):
    batch, seq, d = x.shape
    n = batch * seq
    x2 = x.reshape(n, d)

    w_main = jnp.concatenate([w_in[:, :3072], w_in[:, 3104:7200], w_in[:, 7232:]], axis=1).astype(BF16)
    w_small = _pad_cols(jnp.concatenate([w_in[:, 3072:3104], w_in[:, 7200:7232]], axis=1), LANES)
    ws_hi, ws_lo = _split2(w_small)
    main, small = _inproj(x2, norm1_w.reshape(1, d), w_main, ws_hi, ws_lo)

    w2f_pad = jnp.zeros((LANES, GLA_HEADS * GLA_DK), F32).at[0:GLA_GATE_RANK].set(w2_f)
    w2b_pad = jnp.zeros((LANES, GLA_HEADS * GLA_DK), F32).at[GLA_GATE_RANK:2 * GLA_GATE_RANK].set(w2_b)
    ga = _gla(main, small, w2f_pad, w2b_pad, b_f.reshape(1, -1), b_b.reshape(1, -1),
              gla_norm_w.reshape(1, -1), batch, seq)

    gates = jnp.zeros((SUBLANES, LANES), F32)
    gates = gates.at[0, SMALL_AF:SMALL_BF].set(jnp.concatenate([a_log_f, a_log_b]))
    gates = gates.at[1, SMALL_AF:SMALL_BF].set(jnp.concatenate([dt_bias_f, dt_bias_b]))
    gb = _gdn(main, small, gates, conv_w, gdn_norm_w.reshape(1, -1), batch, seq)

    w_route = _pad_cols(jnp.concatenate([w_router, w_group], axis=1), LANES)
    wr_hi, wr_lo = _split2(w_route)
    x1, h2t, rt, rw, counts = _outproj(ga, gb, x2, w_out.astype(BF16), norm2_w.reshape(1, d), wr_hi, wr_lo)

    blk = MOE_BLOCK
    cnt = counts[0, :N_EXPERTS]
    padded = (cnt + blk - 1) // blk * blk
    ends = jnp.cumsum(padded)
    pstart = (ends - padded).astype(jnp.int32)
    n_blocks = -(-(2 * n + N_EXPERTS * (blk - 1)) // blk)
    n_rows = n_blocks * blk
    block_row = jnp.arange(n_blocks, dtype=jnp.int32) * blk
    block_expert = jnp.minimum(
        jnp.sum((ends[None, :] <= block_row[:, None]).astype(jnp.int32), axis=1), N_EXPERTS - 1)
    n_valid = (ends[-1:] // blk).astype(jnp.int32)
    seg = jnp.stack([jnp.append(pstart + cnt, ends[-1]), jnp.append(ends, n_rows)]).astype(jnp.int32)

    experts = jnp.arange(N_EXPERTS, dtype=jnp.int32)
    seg_start = jnp.sum(jnp.where(rt[0:2, :, None] == experts, pstart, 0), axis=-1)
    dest = seg_start + rt[2:4]
    dest0, dest1 = dest[0], dest[1]

    xs = _scatter(seg, dest0, dest1, h2t, n_rows)
    y = _experts(block_expert, n_valid, xs, w_gate, w_up, w_down)
    out = _combine(dest0, dest1, x1, rw, out_norm_w.reshape(1, d), y)
    return out.reshape(batch, seq, d)


def kernel(x, norm1_w, w_in, gla_gate_w2_fwd, gla_gate_b_fwd, gla_gate_w2_bwd, gla_gate_b_bwd, gla_norm_w,
           gdn_conv_w, gdn_a_log_fwd, gdn_dt_bias_fwd, gdn_a_log_bwd, gdn_dt_bias_bwd, gdn_norm_w, w_out,
           norm2_w, moe_w_group, moe_w_router, moe_w_gate, moe_w_up, moe_w_down, norm_f_w):
    assert norm1_w.shape[0] == 1, "single-layer block"
    return _token_mixer_and_moe(
        x, norm1_w[0], w_in[0], gla_gate_w2_fwd[0], gla_gate_b_fwd[0], gla_gate_w2_bwd[0], gla_gate_b_bwd[0],
        gla_norm_w[0], gdn_conv_w[0], gdn_a_log_fwd[0], gdn_dt_bias_fwd[0], gdn_a_log_bwd[0],
        gdn_dt_bias_bwd[0], gdn_norm_w[0], w_out[0], norm2_w[0], moe_w_group[0], moe_w_router[0],
        moe_w_gate[0], moe_w_up[0], moe_w_down[0], norm_f_w)
```

```python
import functools

import jax
import jax.numpy as jnp
import numpy as np
from jax import lax
from jax.experimental import pallas as pl
from jax.experimental.pallas import tpu as pltpu

F32 = jnp.float32
BF16 = jnp.bfloat16
U32 = jnp.uint32

D_MODEL = 1024
GLA_HEADS = 4
GLA_DK = 128
GLA_DV = 256
GLA_GATE_RANK = 16
GLA_GATE_TAU = 16.0
GLA_CHUNK = 64
GLA_GROUP = 8
GDN_HEADS = 8
GDN_DK = 128
GDN_DV = 128
GDN_CONV = 5
GDN_CHUNK = 128
GDN_PREP_GROUP = 8
N_GROUPS = 4
EXPERTS_PER_GROUP = 8
N_EXPERTS = N_GROUPS * EXPERTS_PER_GROUP
D_EXPERT = 256
EPS = 1e-6

LANES = 128
SUBLANES = 8
VMEM_LIMIT = 48 * 1024 * 1024

COL_GQ, COL_GK, COL_GV, COL_GR = 0, 512, 1024, 2048
COL_DQ, COL_DK, COL_DV, COL_DZ = 3072, 4096, 5120, 6144
COL_MA, COL_MB = 7168, 8192
D_MAIN = 9216
SMALL_AF, SMALL_AB, SMALL_BF, SMALL_BB = 32, 40, 48, 56
ROUTE_GROUP_LANE = 32

MOE_BLOCK = 256
ROW_TILE = D_MODEL // 2 // LANES
HIGH_HALF = np.uint32(0xFFFF0000)
DMA_UNROLL = 8
INPROJ_TM, INPROJ_TN = 1024, 1024
OUTPROJ_TM = 512
OUTPROJ_SUB = 128
SCATTER_T = 512
COMBINE_T = 256
CONV_ROWS = 256
NEG_INF = float("-inf")


def _dot(a, b):
    return jnp.dot(a, b, preferred_element_type=F32)


def _dot_nt(a, b):
    return lax.dot_general(a, b, (((1,), (1,)), ((), ())), preferred_element_type=F32)


def _dot_tn(a, b):
    return lax.dot_general(a, b, (((0,), (0,)), ((), ())), preferred_element_type=F32)


def _split2(x):
    hi = x.astype(BF16)
    lo = (x - hi.astype(F32)).astype(BF16)
    return hi, lo


def _split3(x):
    hi = x.astype(BF16)
    r = x - hi.astype(F32)
    mid = r.astype(BF16)
    lo = (r - mid.astype(F32)).astype(BF16)
    return hi, mid, lo


def _dot_exact_rhs(x, m_bf16):
    hi, mid, lo = _split3(x)
    return _dot(hi, m_bf16) + _dot(mid, m_bf16) + _dot(lo, m_bf16)


def _dot_exact_lhs(m_bf16, x):
    hi, mid, lo = _split3(x)
    return _dot(m_bf16, hi) + _dot(m_bf16, mid) + _dot(m_bf16, lo)


def _dot_lhs2(m_bf16, x):
    hi, lo = _split2(x)
    return _dot(m_bf16, hi) + _dot(m_bf16, lo)


def _dot_lhs2_wide(m2_bf16, x):
    return _dot(m2_bf16, jnp.concatenate(_split2(x), axis=0))


def _cumsum_rows(x, reverse):
    rows = x.shape[0]
    row = _iota2(x.shape, 0)
    shift = 1
    while shift < rows:
        if reverse:
            x = x + jnp.where(row < rows - shift, pltpu.roll(x, rows - shift, axis=0), 0.0)
        else:
            x = x + jnp.where(row >= shift, pltpu.roll(x, shift, axis=0), 0.0)
        shift *= 2
    return x


def _dot3(a, b):
    ah, al = _split2(a)
    bh, bl = _split2(b)
    return _dot(ah, bh) + _dot(al, bh) + _dot(ah, bl)


def _store_row_tiles(ref, x):
    rows = x.shape[0]
    half = D_MODEL // 2
    hi = lax.bitcast_convert_type(x[:, :half].astype(BF16).astype(F32), U32)
    lo = lax.bitcast_convert_type(x[:, half:].astype(BF16).astype(F32), U32)
    packed = jnp.bitwise_or(jnp.bitwise_and(hi, HIGH_HALF), jnp.right_shift(lo, 16))
    for j in range(ROW_TILE):
        ref[pl.ds(j, rows, stride=ROW_TILE), :] = packed[:, j * LANES:(j + 1) * LANES]


def _load_row_tiles(ref):
    rows = ref.shape[0] // ROW_TILE
    packed = jnp.concatenate([ref[pl.ds(j, rows, stride=ROW_TILE), :] for j in range(ROW_TILE)], axis=1)
    hi = lax.bitcast_convert_type(jnp.bitwise_and(packed, HIGH_HALF), F32)
    lo = lax.bitcast_convert_type(jnp.left_shift(packed, 16), F32)
    return jnp.concatenate([hi, lo], axis=1)


def _each(fn, *lists):
    return [fn(*args) for args in zip(*lists)]


def _sigmoid(x):
    return 1.0 / (1.0 + jnp.exp(-x))


def _silu(x):
    return x * _sigmoid(x)


def _softplus(x):
    return jnp.maximum(x, 0.0) + jnp.log(1.0 + jnp.exp(-jnp.abs(x)))


def _log_sigmoid(x):
    return jnp.minimum(x, 0.0) - jnp.log(1.0 + jnp.exp(-jnp.abs(x)))


def _iota2(shape, dim):
    return lax.broadcasted_iota(jnp.int32, shape, dim)


def _inproj_kernel(x_ref, nw_ref, w_ref, wsh_ref, wsl_ref, main_ref, small_ref, h_scr):
    @pl.when(pl.program_id(1) == 0)
    def _():
        x = x_ref[...]
        h = x * lax.rsqrt(jnp.mean(x * x, axis=-1, keepdims=True) + EPS) * nw_ref[...]
        hh, hl = _split2(h)
        h_scr[...] = hh
        small_ref[...] = _dot(hh, wsh_ref[...]) + _dot(hl, wsh_ref[...]) + _dot(hh, wsl_ref[...])

    main_ref[...] = _dot(h_scr[...], w_ref[...]).astype(BF16)


def _inproj(x2, norm_w, w_main, ws_hi, ws_lo):
    n = x2.shape[0]
    tm, tn = INPROJ_TM, INPROJ_TN
    return pl.pallas_call(
        _inproj_kernel,
        grid=(n // tm, D_MAIN // tn),
        in_specs=[
            pl.BlockSpec((tm, D_MODEL), lambda i, j: (i, 0)),
            pl.BlockSpec((1, D_MODEL), lambda i, j: (0, 0)),
            pl.BlockSpec((D_MODEL, tn), lambda i, j: (0, j)),
            pl.BlockSpec((D_MODEL, LANES), lambda i, j: (0, 0)),
            pl.BlockSpec((D_MODEL, LANES), lambda i, j: (0, 0)),
        ],
        out_specs=[
            pl.BlockSpec((tm, tn), lambda i, j: (i, j)),
            pl.BlockSpec((tm, LANES), lambda i, j: (i, 0)),
        ],
        out_shape=[
            jax.ShapeDtypeStruct((n, D_MAIN), BF16),
            jax.ShapeDtypeStruct((n, LANES), F32),
        ],
        scratch_shapes=[pltpu.VMEM((tm, D_MODEL), BF16)],
        compiler_params=pltpu.CompilerParams(
            dimension_semantics=("arbitrary", "arbitrary"), vmem_limit_bytes=VMEM_LIMIT),
        name="inproj",
    )(x2, norm_w, w_main, ws_hi, ws_lo)


def _gla_kernel(q_ref, k_ref, v_ref, gr_ref, ma_ref, small_ref, w2f_ref, w2b_ref, bf_ref, bb_ref,
                nw_ref, out_ref, laf_scr, lab_scr, o_scr, stf_scr, stb_scr, *, seq, chunk):
    c = chunk
    n = seq // c
    scale = GLA_DK ** -0.5

    sm = small_ref[...]
    laf_scr[...] = _log_sigmoid(_dot3(sm, w2f_ref[...]) + bf_ref[...]) * (1.0 / GLA_GATE_TAU)
    lab_scr[...] = _log_sigmoid(_dot3(sm, w2b_ref[...]) + bb_ref[...]) * (1.0 / GLA_GATE_TAU)
    stf_scr[...] = jnp.zeros_like(stf_scr)
    stb_scr[...] = jnp.zeros_like(stb_scr)

    row = _iota2((c, c), 0)
    col = _iota2((c, c), 1)
    low = row >= col
    upp = row <= col
    low_m = jnp.concatenate([jnp.where(low, 1.0, 0.0).astype(BF16)] * 2, axis=1)
    upp_m = jnp.concatenate([jnp.where(upp, 1.0, 0.0).astype(BF16)] * 2, axis=1)

    g = GLA_GROUP

    def finish(rows, o):
        y = o * lax.rsqrt(jnp.mean(o * o, axis=-1, keepdims=True) + EPS) * nw_ref[...]
        y = y * _silu(gr_ref[rows, :].astype(F32))
        y = y * _sigmoid(ma_ref[rows, :].astype(F32))
        out_ref[rows, :] = y.astype(BF16)

    def group(gi, second_touch):
        ids = [gi * g + j for j in range(g)] + [n - 1 - gi * g - j for j in range(g)]
        rows = [pl.ds(pl.multiple_of(i * c, c), c) for i in ids]
        la = [laf_scr[r, :] for r in rows[:g]] + [lab_scr[r, :] for r in rows[g:]]
        csum = [low_m] * g + [upp_m] * g
        mask = [low] * g + [upp] * g
        tot_row = [c - 1] * g + [0] * g
        qf = [q_ref[r, :].astype(F32) * scale for r in rows]
        kf = [k_ref[r, :].astype(F32) for r in rows]
        vc = [v_ref[r, :] for r in rows]

        cum = _each(_dot_lhs2_wide, csum, la)
        tot = _each(lambda x, r: x[r:r + 1, :], cum, tot_row)
        q_dec = _each(lambda q, x: (q * jnp.exp(x)).astype(BF16), qf, cum)
        k_inv = _each(lambda k, x: (k * jnp.exp(-x)).astype(BF16), kf, cum)
        k_tail = _each(lambda k, t, x: (k * jnp.exp(t - x)).astype(BF16), kf, tot, cum)
        s = _each(lambda m, q, k: jnp.where(m, _dot_nt(q, k), 0.0).astype(BF16), mask, q_dec, k_inv)
        o = _each(_dot, s, vc)
        kv = _each(_dot_tn, vc, k_tail)
        dec = _each(jnp.exp, tot)

        for st_scr, probs in ((stf_scr, range(g)), (stb_scr, range(g, 2 * g))):
            st = st_scr[...]
            for p in probs:
                o[p] = o[p] + _dot_nt(q_dec[p], st.astype(BF16))
                st = dec[p] * st + kv[p]
            st_scr[...] = st

        for r, o_p in zip(rows, o):
            if second_touch:
                finish(r, o_scr[r, :] + o_p)
            else:
                o_scr[r, :] = o_p

    def first_half(gi, carry):
        group(gi, False)
        return carry

    def second_half(gi, carry):
        group(gi, True)
        return carry

    n_groups = n // g
    lax.fori_loop(0, n_groups // 2, first_half, 0)
    lax.fori_loop(n_groups // 2, n_groups, second_half, 0)


def _gla(main, small, w2f_pad, w2b_pad, b_f, b_b, norm_w, batch, seq):
    n = batch * seq
    h = GLA_HEADS
    kern = functools.partial(_gla_kernel, seq=seq, chunk=GLA_CHUNK)
    qk_blk = lambda off: pl.BlockSpec((seq, GLA_DK), lambda b, hh, off=off: (b, off // GLA_DK + hh))
    v_blk = lambda off: pl.BlockSpec((seq, GLA_DV), lambda b, hh, off=off: (b, off // GLA_DV + hh))
    return pl.pallas_call(
        kern,
        grid=(batch, h),
        in_specs=[
            qk_blk(COL_GQ), qk_blk(COL_GK), v_blk(COL_GV), v_blk(COL_GR), v_blk(COL_MA),
            pl.BlockSpec((seq, LANES), lambda b, hh: (b, 0)),
            pl.BlockSpec((LANES, GLA_DK), lambda b, hh: (0, hh)),
            pl.BlockSpec((LANES, GLA_DK), lambda b, hh: (0, hh)),
            pl.BlockSpec((1, GLA_DK), lambda b, hh: (0, hh)),
            pl.BlockSpec((1, GLA_DK), lambda b, hh: (0, hh)),
            pl.BlockSpec((1, GLA_DV), lambda b, hh: (0, 0)),
        ],
        out_specs=pl.BlockSpec((seq, GLA_DV), lambda b, hh: (b, hh)),
        out_shape=jax.ShapeDtypeStruct((n, D_MODEL), BF16),
        scratch_shapes=[
            pltpu.VMEM((seq, GLA_DK), F32), pltpu.VMEM((seq, GLA_DK), F32),
            pltpu.VMEM((seq, GLA_DV), F32),
            pltpu.VMEM((GLA_DV, GLA_DK), F32), pltpu.VMEM((GLA_DV, GLA_DK), F32),
        ],
        compiler_params=pltpu.CompilerParams(
            dimension_semantics=("arbitrary", "arbitrary"), vmem_limit_bytes=VMEM_LIMIT),
        name="gla",
    )(main, main, main, main, main, small, w2f_pad, w2b_pad, b_f, b_b, norm_w)


TRI_BLOCK = 16


def _mm(a, b):
    return _dot(a.astype(BF16), b.astype(BF16))


def _nilpotent_inverse(a_list, eye, index, tick):
    t_list = _each(lambda a: eye - a, a_list)
    p_list = a_list
    power = 2
    while power < index:
        p_list = _each(lambda p: _mm(p, p), p_list)
        tick()
        t_list = _each(lambda t, p: t + _mm(t, p), t_list, p_list)
        tick()
        power *= 2
    return t_list


def _tri_inverse(a_list, eye, diag_blocks, chunk, tick):
    ad_list = _each(lambda a: jnp.where(diag_blocks, a, 0.0), a_list)
    ao_list = _each(lambda a: jnp.where(diag_blocks, 0.0, a), a_list)
    d_list = _nilpotent_inverse(ad_list, eye, TRI_BLOCK, tick)
    n_list = _each(_mm, d_list, ao_list)
    tick()
    t_list = _nilpotent_inverse(n_list, eye, chunk // TRI_BLOCK, tick)
    out = _each(_mm, t_list, d_list)
    tick()
    return out


def _gdn_kernel(gate_ref, q_ref, k_ref, v_ref, z_ref, mb_ref, small_ref, cwq_ref, cwk_ref, cwv_ref,
                nw_ref, out_ref, pad_scr, qs_scr, ks_scr, vs_scr, gf_scr, gb_scr, btf_scr, btb_scr,
                o_scr, nmat_scr, bmat_scr, qp_scr, cd_scr, sf_scr, sb_scr, *, seq, chunk, n_heads_total):
    c = chunk
    n = seq // c
    step = pl.program_id(0)
    hh = lax.rem(jnp.minimum(step, n_heads_total - 1), GDN_HEADS)
    cur = lax.rem(step, 2)
    prev = 1 - cur
    scale = GDN_DK ** -0.5

    @pl.when(step == 0)
    def _():
        nmat_scr[...] = jnp.zeros_like(nmat_scr)
        bmat_scr[...] = jnp.zeros_like(bmat_scr)
        qp_scr[...] = jnp.zeros_like(qp_scr)
        cd_scr[...] = jnp.zeros_like(cd_scr)
        o_scr[...] = jnp.zeros_like(o_scr)

    zeros8 = jnp.zeros((SUBLANES, LANES), F32)
    pad_scr[0:SUBLANES, :] = zeros8
    pad_scr[seq + SUBLANES:seq + 2 * SUBLANES, :] = zeros8
    half = GDN_CONV // 2

    def conv_into(src_ref, cw_ref, dst_ref, normalise, mult):
        pad_scr[SUBLANES:seq + SUBLANES, :] = src_ref[...].astype(F32)
        w = cw_ref[...]

        def body(i, carry):
            r0 = pl.multiple_of(i * CONV_ROWS, CONV_ROWS)
            acc = jnp.zeros((CONV_ROWS, LANES), F32)
            for j in range(GDN_CONV):
                tap = pad_scr[pl.ds(r0 + (SUBLANES - half + j), CONV_ROWS), :]
                acc = acc + tap * w[j:j + 1, :]
            y = _silu(acc)
            if normalise:
                y = y * lax.rsqrt(jnp.sum(y * y, axis=-1, keepdims=True) + EPS) * mult
            dst_ref[pl.ds(r0, CONV_ROWS), :] = y.astype(BF16)
            return carry

        lax.fori_loop(0, seq // CONV_ROWS, body, 0)

    conv_into(q_ref, cwq_ref, qs_scr, True, scale)
    conv_into(k_ref, cwk_ref, ks_scr, True, 1.0)
    conv_into(v_ref, cwv_ref, vs_scr, False, 1.0)

    sm = small_ref[...]
    lane = _iota2(sm.shape, 1)
    log_decay = -jnp.exp(gate_ref[0:1, :]) * _softplus(sm + gate_ref[1:2, :])
    gate_vals = jnp.where(lane < SMALL_BF, log_decay, _sigmoid(sm))
    gate_hl = jnp.concatenate(_split2(gate_vals), axis=1)
    sel_lane = jnp.bitwise_and(_iota2((2 * LANES, 4 * LANES), 0), LANES - 1)
    sel_gate = jnp.right_shift(_iota2((2 * LANES, 4 * LANES), 1), LANES.bit_length() - 1)
    sel = jnp.where(sel_lane == SMALL_AF + GDN_HEADS * sel_gate + hh, 1.0, 0.0).astype(BF16)
    spread = _dot(gate_hl, sel)
    gf_scr[...] = spread[:, 0 * LANES:1 * LANES]
    gb_scr[...] = spread[:, 1 * LANES:2 * LANES]
    btf_scr[...] = spread[:, 2 * LANES:3 * LANES]
    btb_scr[...] = spread[:, 3 * LANES:4 * LANES]
    sf_scr[...] = jnp.zeros_like(sf_scr)
    sb_scr[...] = jnp.zeros_like(sb_scr)

    row = _iota2((c, c), 0)
    col = _iota2((c, c), 1)
    eye = jnp.where(row == col, 1.0, 0.0).astype(F32)
    low, slow = row >= col, row > col
    upp, supp = row <= col, row < col
    assert c == LANES, "the decay matrix is formed from a [c, 128] lane-broadcast column and its transpose"

    tri_shift = TRI_BLOCK.bit_length() - 1
    diag_blocks = jnp.right_shift(row, tri_shift) == jnp.right_shift(col, tri_shift)

    cur_slot, prev_slot = cur * (2 * n), prev * (2 * n)
    cur_row, prev_row = cur * seq, prev * seq

    def scan_step(i):
        j = n - 1 - i
        slots = [prev_slot + i, prev_slot + n + j]
        rows = [pl.ds(pl.multiple_of(prev_row + i * c, c), c), pl.ds(pl.multiple_of(prev_row + j * c, c), c)]
        states = [sf_scr, sb_scr]
        s = [ref[...] for ref in states]
        s_b = _each(lambda x: x.astype(BF16), s)
        ns = _each(lambda sl, x: _dot(nmat_scr[sl], x), slots, s_b)
        qs = _each(lambda sl, x: _dot(qp_scr[sl], x), slots, s_b)
        for ref, sl, s_, ns_ in zip(states, slots, s, ns):
            ref[...] = cd_scr[sl][0:1, :] * s_ + (bmat_scr[sl] - ns_)
        for r, q in zip(rows, qs):
            o_scr[r, :] += q

    n_groups = n // GDN_PREP_GROUP
    scans_per_group = n // n_groups

    def prep_group(gi, carry):
        pending = [gi * scans_per_group + j for j in range(scans_per_group)]

        def tick():
            if pending:
                scan_step(pending.pop(0))

        chunk_ids = [gi * GDN_PREP_GROUP + j for j in range(GDN_PREP_GROUP)]
        rows = [pl.ds(pl.multiple_of(i * c, c), c) for i in chunk_ids]
        qc = [qs_scr[r, :] for r in rows]
        kc = [ks_scr[r, :] for r in rows]
        vc = [vs_scr[r, :] for r in rows]
        gl = [ref[r, :] for r in rows for ref in (gf_scr, gb_scr)]
        bt = [ref[r, :] for r in rows for ref in (btf_scr, btb_scr)]
        reverse = [False, True] * GDN_PREP_GROUP
        incl = [low, upp] * GDN_PREP_GROUP
        strict = [slow, supp] * GDN_PREP_GROUP
        tot_row = [c - 1, 0] * GDN_PREP_GROUP

        def both(per_chunk):
            return [x for x in per_chunk for _ in range(2)]

        kk = both(_each(_dot_nt, kc, kc))
        qk = both(_each(_dot_nt, qc, kc))
        tick()
        qf = both(_each(lambda x: x.astype(F32), qc))
        kf = both(_each(lambda x: x.astype(F32), kc))
        vf = both(_each(lambda x: x.astype(F32), vc))

        gc = _each(_cumsum_rows, gl, reverse)
        tot = _each(lambda g, r: g[r:r + 1, :], gc, tot_row)
        e = _each(lambda g, inc: jnp.exp(jnp.where(inc, g - jnp.transpose(g), 0.0)), gc, incl)
        a = _each(lambda kk_, b, e_, st: kk_ * b[:, :c] * jnp.where(st, e_, 0.0), kk, bt, e, strict)
        t_inv = _tri_inverse(a, eye, diag_blocks, c, tick)
        egc = _each(jnp.exp, gc)
        wu = _each(lambda t, k, v, b, eg: _mm(t, jnp.concatenate([k * b * eg, v * b], axis=1)).astype(BF16),
                   t_inv, kf, vf, bt, egc)
        tick()
        attn = _each(lambda qk_, e_, inc: (qk_ * jnp.where(inc, e_, 0.0)).astype(BF16), qk, e, incl)
        k_tail = _each(lambda k, t, g: (k * jnp.exp(t - g)).astype(BF16), kf, tot, gc)
        kwu = _each(_dot_tn, k_tail, wu)
        tick()
        awu = _each(_dot, attn, wu)
        while pending:
            tick()

        for p in range(2 * GDN_PREP_GROUP):
            slot = cur_slot + chunk_ids[p // 2] + (p % 2) * n
            nmat_scr[slot] = kwu[p][:, :GDN_DK].astype(BF16)
            bmat_scr[slot] = kwu[p][:, GDN_DK:]
            qp_scr[slot] = (qf[p] * egc[p] - awu[p][:, :GDN_DK]).astype(BF16)
            cd_scr[slot] = jnp.broadcast_to(jnp.exp(tot[p]), (SUBLANES, LANES))
        for j, i in enumerate(chunk_ids):
            r = pl.ds(pl.multiple_of(cur_row + i * c, c), c)
            o_scr[r, :] = awu[2 * j][:, GDN_DK:] + awu[2 * j + 1][:, GDN_DK:]
        return carry

    lax.fori_loop(0, n_groups, prep_group, 0)

    def finish(i, carry):
        r0 = pl.multiple_of(i * CONV_ROWS, CONV_ROWS)
        o = o_scr[pl.ds(pl.multiple_of(prev_row + r0, CONV_ROWS), CONV_ROWS), :]
        y = o * lax.rsqrt(jnp.mean(o * o, axis=-1, keepdims=True) + EPS) * nw_ref[...]
        y = y * _silu(z_ref[pl.ds(r0, CONV_ROWS), :].astype(F32))
        y = y * _sigmoid(mb_ref[pl.ds(r0, CONV_ROWS), :].astype(F32))
        out_ref[pl.ds(r0, CONV_ROWS), :] = y.astype(BF16)
        return carry

    lax.fori_loop(0, seq // CONV_ROWS, finish, 0)


def _gdn(main, small, gates, conv_w, norm_w, batch, seq):
    n = batch * seq
    total = batch * GDN_HEADS
    kern = functools.partial(_gdn_kernel, seq=seq, chunk=GDN_CHUNK, n_heads_total=total)
    n_chunks = seq // GDN_CHUNK

    def head_of(step):
        idx = jnp.minimum(step, total - 1)
        return idx // GDN_HEADS, idx % GDN_HEADS

    def prev_head_of(step):
        idx = jnp.maximum(step - 1, 0)
        return idx // GDN_HEADS, idx % GDN_HEADS

    def blk(off, which):
        def index(s):
            b, hh = which(s)
            return b, off // LANES + hh
        return pl.BlockSpec((seq, LANES), index)

    def cw(part):
        return pl.BlockSpec((GDN_CONV, LANES), lambda s: (0, part * GDN_HEADS + head_of(s)[1]))

    seq_f32 = lambda: pltpu.VMEM((seq, LANES), F32)
    seq_bf16 = lambda: pltpu.VMEM((seq, LANES), BF16)
    return pl.pallas_call(
        kern,
        grid=(total + 1,),
        in_specs=[
            pl.BlockSpec((SUBLANES, LANES), lambda s: (0, 0)),
            blk(COL_DQ, head_of), blk(COL_DK, head_of), blk(COL_DV, head_of),
            blk(COL_DZ, prev_head_of), blk(COL_MB, prev_head_of),
            pl.BlockSpec((seq, LANES), lambda s: (head_of(s)[0], 0)),
            cw(0), cw(1), cw(2),
            pl.BlockSpec((1, GDN_DV), lambda s: (0, 0)),
        ],
        out_specs=pl.BlockSpec((seq, GDN_DV), lambda s: prev_head_of(s)),
        out_shape=jax.ShapeDtypeStruct((n, D_MODEL), BF16),
        scratch_shapes=[
            pltpu.VMEM((seq + 2 * SUBLANES, LANES), F32),
            seq_bf16(), seq_bf16(), seq_bf16(),
            seq_f32(), seq_f32(), seq_f32(), seq_f32(),
            pltpu.VMEM((2 * seq, LANES), F32),
            pltpu.VMEM((4 * n_chunks, GDN_DK, GDN_DK), BF16),
            pltpu.VMEM((4 * n_chunks, GDN_DK, GDN_DV), F32),
            pltpu.VMEM((4 * n_chunks, GDN_CHUNK, GDN_DK), BF16),
            pltpu.VMEM((4 * n_chunks, SUBLANES, LANES), F32),
            pltpu.VMEM((GDN_DK, GDN_DV), F32), pltpu.VMEM((GDN_DK, GDN_DV), F32),
        ],
        compiler_params=pltpu.CompilerParams(
            dimension_semantics=("arbitrary",), vmem_limit_bytes=VMEM_LIMIT),
        name="gdn",
    )(gates, main, main, main, main, main, small, conv_w, conv_w, conv_w, norm_w)


def _outproj_kernel(ga_ref, gb_ref, x_ref, wo_ref, nw_ref, wrh_ref, wrl_ref,
                    x1_ref, h2_ref, ri_ref, rw_ref, cnt_ref, carry_scr, *, tm):
    @pl.when(pl.program_id(0) == 0)
    def _():
        carry_scr[...] = jnp.zeros_like(carry_scr)

    sub = OUTPROJ_SUB
    subs = [pl.ds(j * sub, sub) for j in range(tm // sub)]
    lane_i = _iota2((sub, LANES), 1)
    lane = lane_i.astype(F32)

    mixed = [(ga_ref[r, :].astype(F32) + gb_ref[r, :].astype(F32)).astype(BF16) for r in subs]
    x1 = _each(lambda r, m: x_ref[r, :] + _dot(m, wo_ref[...]), subs, mixed)
    for r, v in zip(subs, x1):
        x1_ref[r, :] = v
    h2 = _each(lambda v: v * lax.rsqrt(jnp.mean(v * v, axis=-1, keepdims=True) + EPS) * nw_ref[...], x1)
    for j, v in enumerate(h2):
        _store_row_tiles(h2_ref.at[pl.ds(j * sub * ROW_TILE, sub * ROW_TILE)], v)
    hl = _each(_split2, h2)
    lg = _each(lambda p: _dot(p[0], wrh_ref[...]) + _dot(p[1], wrh_ref[...]) + _dot(p[0], wrl_ref[...]), hl)

    def row_max(vals):
        return _each(lambda v: jnp.max(v, axis=-1, keepdims=True), vals)

    def first_lane_of(vals, maxima):
        return _each(lambda v, m: jnp.min(jnp.where(v == m, lane, float(LANES)), axis=-1, keepdims=True),
                     vals, maxima)

    is_g = (lane_i >= ROUTE_GROUP_LANE) & (lane_i < ROUTE_GROUP_LANE + N_GROUPS)
    gl = _each(lambda v: jnp.where(is_g, v, NEG_INF), lg)
    gmax = row_max(gl)
    gidx = _each(lambda i: i - float(ROUTE_GROUP_LANE), first_lane_of(gl, gmax))
    gsum = _each(lambda v, m: jnp.sum(jnp.where(is_g, jnp.exp(v - m), 0.0), axis=-1, keepdims=True), lg, gmax)
    lane_group = jnp.right_shift(lane_i, EXPERTS_PER_GROUP.bit_length() - 1).astype(F32)
    el = _each(lambda v, g: jnp.where((lane_i < N_EXPERTS) & (lane_group == g), v, NEG_INF), lg, gidx)
    m1 = row_max(el)
    e0 = first_lane_of(el, m1)
    el2 = _each(lambda v, i: jnp.where(lane == i, NEG_INF, v), el, e0)
    m2 = row_max(el2)
    e1 = first_lane_of(el2, m2)
    ratio = _each(lambda a, b: jnp.exp(b - a), m1, m2)
    w0 = _each(lambda s_, r: 1.0 / (s_ * (1.0 + r)), gsum, ratio)
    w1 = _each(lambda w, r: w * r, w0, ratio)

    pick0 = _each(lambda i: lane == i, e0)
    pick1 = _each(lambda i: lane == i, e1)
    onehot = jnp.concatenate(_each(lambda p, q: jnp.where(p | q, 1.0, 0.0), pick0, pick1), axis=0)
    trow = _iota2((tm, tm), 0)
    tcol = _iota2((tm, tm), 1)
    before = jnp.where(trow > tcol, 1.0, 0.0).astype(BF16)
    cnt = _dot(before, onehot.astype(BF16)) + carry_scr[0:1, :]
    cnts = [cnt[j * sub:(j + 1) * sub, :] for j in range(tm // sub)]
    rank0 = _each(lambda p, c_: jnp.sum(jnp.where(p, c_, 0.0), axis=-1, keepdims=True), pick0, cnts)
    rank1 = _each(lambda p, c_: jnp.sum(jnp.where(p, c_, 0.0), axis=-1, keepdims=True), pick1, cnts)
    total = carry_scr[0:1, :] + jnp.sum(onehot, axis=0, keepdims=True)
    carry_scr[...] = jnp.broadcast_to(total, carry_scr.shape)
    cnt_ref[...] = jnp.broadcast_to(total, cnt_ref.shape).astype(jnp.int32)

    for j, r in enumerate(subs):
        ri = jnp.where(lane_i == 0, e0[j], jnp.where(lane_i == 1, e1[j], 0.0))
        ri = jnp.where(lane_i == 2, rank0[j], jnp.where(lane_i == 3, rank1[j], ri))
        ri_ref[:, r] = jnp.transpose(ri)[0:SUBLANES, :].astype(jnp.int32)
        rw_ref[r, :] = jnp.where(lane_i == 0, w0[j], jnp.where(lane_i == 1, w1[j], 0.0))


def _outproj(ga, gb, x2, w_out, norm_w, wr_hi, wr_lo):
    n = x2.shape[0]
    tm = OUTPROJ_TM
    kern = functools.partial(_outproj_kernel, tm=tm)
    row_blk = lambda w: pl.BlockSpec((tm, w), lambda i: (i, 0))
    const = lambda shape: pl.BlockSpec(shape, lambda i: (0, 0))
    return pl.pallas_call(
        kern,
        grid=(n // tm,),
        in_specs=[
            row_blk(D_MODEL), row_blk(D_MODEL), row_blk(D_MODEL),
            const((D_MODEL, D_MODEL)), const((1, D_MODEL)),
            const((D_MODEL, LANES)), const((D_MODEL, LANES)),
        ],
        out_specs=[row_blk(D_MODEL),
                   pl.BlockSpec((tm * ROW_TILE, LANES), lambda i: (i, 0)),
                   pl.BlockSpec((SUBLANES, tm), lambda i: (0, i)),
                   row_blk(LANES),
                   const((SUBLANES, LANES))],
        out_shape=[
            jax.ShapeDtypeStruct((n, D_MODEL), F32),
            jax.ShapeDtypeStruct((n * ROW_TILE, LANES), U32),
            jax.ShapeDtypeStruct((SUBLANES, n), jnp.int32),
            jax.ShapeDtypeStruct((n, LANES), F32),
            jax.ShapeDtypeStruct((SUBLANES, LANES), jnp.int32),
        ],
        scratch_shapes=[pltpu.VMEM((SUBLANES, LANES), F32)],
        compiler_params=pltpu.CompilerParams(
            dimension_semantics=("arbitrary",), vmem_limit_bytes=VMEM_LIMIT),
        name="outproj",
    )(ga, gb, x2, w_out, norm_w, wr_hi, wr_lo)


def _row_copy(src_ref, src_row, dst_ref, dst_row, sem):
    src = src_ref.at[pl.ds(pl.multiple_of(src_row * ROW_TILE, ROW_TILE), ROW_TILE)]
    dst = dst_ref.at[pl.ds(pl.multiple_of(dst_row * ROW_TILE, ROW_TILE), ROW_TILE)]
    return pltpu.make_async_copy(src, dst, sem)


def _scatter_kernel(seg_ref, d0_ref, d1_ref, h2_ref, xs_ref, zero_scr, sem, zsem, *, tile):
    i = pl.program_id(0)

    def issue(t, carry):
        _row_copy(h2_ref, t, xs_ref, d0_ref[t], sem).start(priority=0)
        _row_copy(h2_ref, t, xs_ref, d1_ref[t], sem).start(priority=1)
        return carry

    lax.fori_loop(0, tile, issue, 0, unroll=DMA_UNROLL)

    @pl.when(i == 0)
    def _():
        zero_scr[...] = jnp.zeros_like(zero_scr)

        def per_expert(e, carry):
            lo, hi = seg_ref[0, e], seg_ref[1, e]

            def start(r, c2):
                _row_copy(zero_scr, 0, xs_ref, r, zsem).start()
                return c2

            def wait(r, c2):
                _row_copy(zero_scr, 0, xs_ref, r, zsem).wait()
                return c2

            lax.fori_loop(lo, hi, start, 0)
            lax.fori_loop(lo, hi, wait, 0)
            return carry

        lax.fori_loop(0, N_EXPERTS + 1, per_expert, 0)

    def drain(t, carry):
        _row_copy(h2_ref, 0, xs_ref, 0, sem).wait()
        _row_copy(h2_ref, 0, xs_ref, 0, sem).wait()
        return carry

    lax.fori_loop(0, tile, drain, 0, unroll=DMA_UNROLL)


def _scatter(seg, dest0, dest1, h2t, n_rows):
    n = dest0.shape[0]
    tile = SCATTER_T
    kern = functools.partial(_scatter_kernel, tile=tile)
    return pl.pallas_call(
        kern,
        grid=(n // tile,),
        in_specs=[
            pl.BlockSpec(memory_space=pltpu.SMEM),
            pl.BlockSpec((tile,), lambda i: (i,), memory_space=pltpu.SMEM),
            pl.BlockSpec((tile,), lambda i: (i,), memory_space=pltpu.SMEM),
            pl.BlockSpec((tile * ROW_TILE, LANES), lambda i: (i, 0)),
        ],
        out_specs=pl.BlockSpec(memory_space=pl.ANY),
        out_shape=jax.ShapeDtypeStruct((n_rows * ROW_TILE, LANES), U32),
        scratch_shapes=[pltpu.VMEM((SUBLANES, LANES), U32),
                        pltpu.SemaphoreType.DMA, pltpu.SemaphoreType.DMA],
        compiler_params=pltpu.CompilerParams(
            dimension_semantics=("arbitrary",), vmem_limit_bytes=VMEM_LIMIT),
        name="scatter",
    )(seg, dest0, dest1, h2t)


def _expert_kernel(be_ref, nv_ref, xs_ref, wg_ref, wu_ref, wd_ref, y_ref):
    i = pl.program_id(0)

    @pl.when(i < nv_ref[0])
    def _():
        x = _load_row_tiles(xs_ref).astype(BF16)
        g = _dot(x, wg_ref[0].astype(BF16))
        u = _dot(x, wu_ref[0].astype(BF16))
        hid = (_silu(g) * u).astype(BF16)
        _store_row_tiles(y_ref, _dot(hid, wd_ref[0].astype(BF16)))

    @pl.when(i >= nv_ref[0])
    def _():
        y_ref[...] = jnp.zeros_like(y_ref)


def _experts(block_expert, n_valid, xs, w_gate, w_up, w_down):
    blk = MOE_BLOCK
    n_rows = xs.shape[0] // ROW_TILE
    grid_spec = pltpu.PrefetchScalarGridSpec(
        num_scalar_prefetch=2,
        grid=(n_rows // blk,),
        in_specs=[
            pl.BlockSpec((blk * ROW_TILE, LANES), lambda i, be, nv: (jnp.minimum(i, nv[0] - 1), 0)),
            pl.BlockSpec((1, D_MODEL, D_EXPERT), lambda i, be, nv: (be[i], 0, 0)),
            pl.BlockSpec((1, D_MODEL, D_EXPERT), lambda i, be, nv: (be[i], 0, 0)),
            pl.BlockSpec((1, D_EXPERT, D_MODEL), lambda i, be, nv: (be[i], 0, 0)),
        ],
        out_specs=pl.BlockSpec((blk * ROW_TILE, LANES), lambda i, be, nv: (i, 0)),
    )
    return pl.pallas_call(
        _expert_kernel,
        grid_spec=grid_spec,
        out_shape=jax.ShapeDtypeStruct((n_rows * ROW_TILE, LANES), U32),
        compiler_params=pltpu.CompilerParams(
            dimension_semantics=("arbitrary",), vmem_limit_bytes=VMEM_LIMIT),
        name="experts",
    )(block_expert, n_valid, xs, w_gate, w_up, w_down)


def _combine_kernel(d0_ref, d1_ref, x1_ref, rw_ref, nw_ref, y_ref, out_ref, ya_scr, yb_scr, sem, *, tile):
    def issue(t, carry):
        _row_copy(y_ref, d0_ref[t], ya_scr, t, sem).start(priority=0)
        _row_copy(y_ref, d1_ref[t], yb_scr, t, sem).start(priority=1)
        return carry

    def drain(t, carry):
        _row_copy(y_ref, 0, ya_scr, 0, sem).wait()
        _row_copy(y_ref, 0, yb_scr, 0, sem).wait()
        return carry

    lax.fori_loop(0, tile, issue, 0, unroll=DMA_UNROLL)
    lax.fori_loop(0, tile, drain, 0, unroll=DMA_UNROLL)

    rw = rw_ref[...]
    moe = rw[:, 0:1] * _load_row_tiles(ya_scr) + rw[:, 1:2] * _load_row_tiles(yb_scr)
    x2 = x1_ref[...] + moe
    out_ref[...] = x2 * lax.rsqrt(jnp.mean(x2 * x2, axis=-1, keepdims=True) + EPS) * nw_ref[...]


def _combine(dest0, dest1, x1, rw, norm_w, y):
    n = x1.shape[0]
    tile = COMBINE_T
    kern = functools.partial(_combine_kernel, tile=tile)
    return pl.pallas_call(
        kern,
        grid=(n // tile,),
        in_specs=[
            pl.BlockSpec((tile,), lambda i: (i,), memory_space=pltpu.SMEM),
            pl.BlockSpec((tile,), lambda i: (i,), memory_space=pltpu.SMEM),
            pl.BlockSpec((tile, D_MODEL), lambda i: (i, 0)),
            pl.BlockSpec((tile, LANES), lambda i: (i, 0)),
            pl.BlockSpec((1, D_MODEL), lambda i: (0, 0)),
            pl.BlockSpec(memory_space=pl.ANY),
        ],
        out_specs=pl.BlockSpec((tile, D_MODEL), lambda i: (i, 0)),
        out_shape=jax.ShapeDtypeStruct((n, D_MODEL), F32),
        scratch_shapes=[
            pltpu.VMEM((tile * ROW_TILE, LANES), U32), pltpu.VMEM((tile * ROW_TILE, LANES), U32),
            pltpu.SemaphoreType.DMA,
        ],
        compiler_params=pltpu.CompilerParams(
            dimension_semantics=("arbitrary",), vmem_limit_bytes=VMEM_LIMIT),
        name="combine",
    )(dest0, dest1, x1, rw, norm_w, y)


def _pad_cols(w, width):
    return jnp.pad(w, ((0, 0), (0, width - w.shape[1])))


def _token_mixer_and_moe(x, norm1_w, w_in, w2_f, b_f, w2_b, b_b, gla_norm_w, conv_w, a_log_f, dt_bias_f,
                         a_log_b, dt_bias_b, gdn_norm_w, w_out, norm2_w, w_group, w_router, w_gate, w_up,
                         w_down, out_norm_w):
    batch, seq, d = x.shape
    n = batch * seq
    x2 = x.reshape(n, d)

    w_main = jnp.concatenate([w_in[:, :3072], w_in[:, 3104:7200], w_in[:, 7232:]], axis=1).astype(BF16)
    w_small = _pad_cols(jnp.concatenate([w_in[:, 3072:3104], w_in[:, 7200:7232]], axis=1), LANES)
    ws_hi, ws_lo = _split2(w_small)
    main, small = _inproj(x2, norm1_w.reshape(1, d), w_main, ws_hi, ws_lo)

    w2f_pad = jnp.zeros((LANES, GLA_HEADS * GLA_DK), F32).at[0:GLA_GATE_RANK].set(w2_f)
    w2b_pad = jnp.zeros((LANES, GLA_HEADS * GLA_DK), F32).at[GLA_GATE_RANK:2 * GLA_GATE_RANK].set(w2_b)
    ga = _gla(main, small, w2f_pad, w2b_pad, b_f.reshape(1, -1), b_b.reshape(1, -1),
              gla_norm_w.reshape(1, -1), batch, seq)

    gates = jnp.zeros((SUBLANES, LANES), F32)
    gates = gates.at[0, SMALL_AF:SMALL_BF].set(jnp.concatenate([a_log_f, a_log_b]))
    gates = gates.at[1, SMALL_AF:SMALL_BF].set(jnp.concatenate([dt_bias_f, dt_bias_b]))
    gb = _gdn(main, small, gates, conv_w, gdn_norm_w.reshape(1, -1), batch, seq)

    w_route = _pad_cols(jnp.concatenate([w_router, w_group], axis=1), LANES)
    wr_hi, wr_lo = _split2(w_route)
    x1, h2t, rt, rw, counts = _outproj(ga, gb, x2, w_out.astype(BF16), norm2_w.reshape(1, d), wr_hi, wr_lo)

    blk = MOE_BLOCK
    cnt = counts[0, :N_EXPERTS]
    padded = (cnt + blk - 1) // blk * blk
    ends = jnp.cumsum(padded)
    pstart = (ends - padded).astype(jnp.int32)
    n_blocks = -(-(2 * n + N_EXPERTS * (blk - 1)) // blk)
    n_rows = n_blocks * blk
    block_row = jnp.arange(n_blocks, dtype=jnp.int32) * blk
    block_expert = jnp.minimum(
        jnp.sum((ends[None, :] <= block_row[:, None]).astype(jnp.int32), axis=1), N_EXPERTS - 1)
    n_valid = (ends[-1:] // blk).astype(jnp.int32)
    seg = jnp.stack([jnp.append(pstart + cnt, ends[-1]), jnp.append(ends, n_rows)]).astype(jnp.int32)

    experts = jnp.arange(N_EXPERTS, dtype=jnp.int32)
    seg_start = jnp.sum(jnp.where(rt[0:2, :, None] == experts, pstart, 0), axis=-1)
    dest = seg_start + rt[2:4]
    dest0, dest1 = dest[0], dest[1]

    xs = _scatter(seg, dest0, dest1, h2t, n_rows)
    y = _experts(block_expert, n_valid, xs, w_gate, w_up, w_down)
    out = _combine(dest0, dest1, x1, rw, out_norm_w.reshape(1, d), y)
    return out.reshape(batch, seq, d)


def kernel(x, norm1_w, w_in, gla_gate_w2_fwd, gla_gate_b_fwd, gla_gate_w2_bwd, gla_gate_b_bwd, gla_norm_w,
           gdn_conv_w, gdn_a_log_fwd, gdn_dt_bias_fwd, gdn_a_log_bwd, gdn_dt_bias_bwd, gdn_norm_w, w_out,
           norm2_w, moe_w_group, moe_w_router, moe_w_gate, moe_w_up, moe_w_down, norm_f_w):
    assert norm1_w.shape[0] == 1, "single-layer block"
    return _token_mixer_and_moe(
        x, norm1_w[0], w_in[0], gla_gate_w2_fwd[0], gla_gate_b_fwd[0], gla_gate_w2_bwd[0], gla_gate_b_bwd[0],
        gla_norm_w[0], gdn_conv_w[0], gdn_a_log_fwd[0], gdn_dt_bias_fwd[0], gdn_a_log_bwd[0],
        gdn_dt_bias_bwd[0], gdn_norm_w[0], w_out[0], norm2_w[0], moe_w_group[0], moe_w_router[0],
        moe_w_gate[0], moe_w_up[0], moe_w_down[0], norm_f_w)
```

```python
import functools

import jax
import jax.numpy as jnp
import numpy as np
from jax import lax
from jax.experimental import pallas as pl
from jax.experimental.pallas import tpu as pltpu

F32 = jnp.float32
BF16 = jnp.bfloat16
U32 = jnp.uint32

D_MODEL = 1024
GLA_HEADS = 4
GLA_DK = 128
GLA_DV = 256
GLA_GATE_RANK = 16
GLA_GATE_TAU = 16.0
GLA_CHUNK = 64
GLA_GROUP = 8
GDN_HEADS = 8
GDN_DK = 128
GDN_DV = 128
GDN_CONV = 5
GDN_CHUNK = 128
GDN_PREP_GROUP = 8
N_GROUPS = 4
EXPERTS_PER_GROUP = 8
N_EXPERTS = N_GROUPS * EXPERTS_PER_GROUP
D_EXPERT = 256
EPS = 1e-6

LANES = 128
SUBLANES = 8
VMEM_LIMIT = 48 * 1024 * 1024

COL_GQ, COL_GK, COL_GV, COL_GR = 0, 512, 1024, 2048
COL_DQ, COL_DK, COL_DV, COL_DZ = 3072, 4096, 5120, 6144
COL_MA, COL_MB = 7168, 8192
D_MAIN = 9216
SMALL_AF, SMALL_AB, SMALL_BF, SMALL_BB = 32, 40, 48, 56
ROUTE_GROUP_LANE = 32

MOE_BLOCK = 256
ROW_TILE = D_MODEL // 2 // LANES
HIGH_HALF = np.uint32(0xFFFF0000)
DMA_UNROLL = 8
INPROJ_TM, INPROJ_TN = 1024, 1024
OUTPROJ_TM = 512
OUTPROJ_SUB = 128
SCATTER_T = 512
COMBINE_T = 256
CONV_ROWS = 256
NEG_INF = float("-inf")


def _dot(a, b):
    return jnp.dot(a, b, preferred_element_type=F32)


def _dot_nt(a, b):
    return lax.dot_general(a, b, (((1,), (1,)), ((), ())), preferred_element_type=F32)


def _dot_tn(a, b):
    return lax.dot_general(a, b, (((0,), (0,)), ((), ())), preferred_element_type=F32)


def _split2(x):
    hi = x.astype(BF16)
    lo = (x - hi.astype(F32)).astype(BF16)
    return hi, lo


def _split3(x):
    hi = x.astype(BF16)
    r = x - hi.astype(F32)
    mid = r.astype(BF16)
    lo = (r - mid.astype(F32)).astype(BF16)
    return hi, mid, lo


def _dot_exact_rhs(x, m_bf16):
    hi, mid, lo = _split3(x)
    return _dot(hi, m_bf16) + _dot(mid, m_bf16) + _dot(lo, m_bf16)


def _dot_exact_lhs(m_bf16, x):
    hi, mid, lo = _split3(x)
    return _dot(m_bf16, hi) + _dot(m_bf16, mid) + _dot(m_bf16, lo)


def _dot_lhs2(m_bf16, x):
    hi, lo = _split2(x)
    return _dot(m_bf16, hi) + _dot(m_bf16, lo)


def _dot_lhs2_wide(m2_bf16, x):
    return _dot(m2_bf16, jnp.concatenate(_split2(x), axis=0))


def _cumsum_rows(x, reverse):
    rows = x.shape[0]
    row = _iota2(x.shape, 0)
    shift = 1
    while shift < rows:
        if reverse:
            x = x + jnp.where(row < rows - shift, pltpu.roll(x, rows - shift, axis=0), 0.0)
        else:
            x = x + jnp.where(row >= shift, pltpu.roll(x, shift, axis=0), 0.0)
        shift *= 2
    return x


def _dot3(a, b):
    ah, al = _split2(a)
    bh, bl = _split2(b)
    return _dot(ah, bh) + _dot(al, bh) + _dot(ah, bl)


def _store_row_tiles(ref, x):
    rows = x.shape[0]
    half = D_MODEL // 2
    hi = lax.bitcast_convert_type(x[:, :half].astype(BF16).astype(F32), U32)
    lo = lax.bitcast_convert_type(x[:, half:].astype(BF16).astype(F32), U32)
    packed = jnp.bitwise_or(jnp.bitwise_and(hi, HIGH_HALF), jnp.right_shift(lo, 16))
    for j in range(ROW_TILE):
        ref[pl.ds(j, rows, stride=ROW_TILE), :] = packed[:, j * LANES:(j + 1) * LANES]


def _load_row_tiles(ref):
    rows = ref.shape[0] // ROW_TILE
    packed = jnp.concatenate([ref[pl.ds(j, rows, stride=ROW_TILE), :] for j in range(ROW_TILE)], axis=1)
    hi = lax.bitcast_convert_type(jnp.bitwise_and(packed, HIGH_HALF), F32)
    lo = lax.bitcast_convert_type(jnp.left_shift(packed, 16), F32)
    return jnp.concatenate([hi, lo], axis=1)


def _each(fn, *lists):
    return [fn(*args) for args in zip(*lists)]


def _sigmoid(x):
    return 1.0 / (1.0 + jnp.exp(-x))


def _silu(x):
    return x * _sigmoid(x)


def _softplus(x):
    return jnp.maximum(x, 0.0) + jnp.log(1.0 + jnp.exp(-jnp.abs(x)))


def _log_sigmoid(x):
    return jnp.minimum(x, 0.0) - jnp.log(1.0 + jnp.exp(-jnp.abs(x)))


def _iota2(shape, dim):
    return lax.broadcasted_iota(jnp.int32, shape, dim)


def _inproj_kernel(x_ref, nw_ref, w_ref, wsh_ref, wsl_ref, main_ref, small_ref, h_scr):
    @pl.when(pl.program_id(1) == 0)
    def _():
        x = x_ref[...]
        h = x * lax.rsqrt(jnp.mean(x * x, axis=-1, keepdims=True) + EPS) * nw_ref[...]
        hh, hl = _split2(h)
        h_scr[...] = hh
        small_ref[...] = _dot(hh, wsh_ref[...]) + _dot(hl, wsh_ref[...]) + _dot(hh, wsl_ref[...])

    main_ref[...] = _dot(h_scr[...], w_ref[...]).astype(BF16)


def _inproj(x2, norm_w, w_main, ws_hi, ws_lo):
    n = x2.shape[0]
    tm, tn = INPROJ_TM, INPROJ_TN
    return pl.pallas_call(
        _inproj_kernel,
        grid=(n // tm, D_MAIN // tn),
        in_specs=[
            pl.BlockSpec((tm, D_MODEL), lambda i, j: (i, 0)),
            pl.BlockSpec((1, D_MODEL), lambda i, j: (0, 0)),
            pl.BlockSpec((D_MODEL, tn), lambda i, j: (0, j)),
            pl.BlockSpec((D_MODEL, LANES), lambda i, j: (0, 0)),
            pl.BlockSpec((D_MODEL, LANES), lambda i, j: (0, 0)),
        ],
        out_specs=[
            pl.BlockSpec((tm, tn), lambda i, j: (i, j)),
            pl.BlockSpec((tm, LANES), lambda i, j: (i, 0)),
        ],
        out_shape=[
            jax.ShapeDtypeStruct((n, D_MAIN), BF16),
            jax.ShapeDtypeStruct((n, LANES), F32),
        ],
        scratch_shapes=[pltpu.VMEM((tm, D_MODEL), BF16)],
        compiler_params=pltpu.CompilerParams(
            dimension_semantics=("arbitrary", "arbitrary"), vmem_limit_bytes=VMEM_LIMIT),
        name="inproj",
    )(x2, norm_w, w_main, ws_hi, ws_lo)


def _gla_kernel(q_ref, k_ref, v_ref, gr_ref, ma_ref, small_ref, w2f_ref, w2b_ref, bf_ref, bb_ref,
                nw_ref, out_ref, laf_scr, lab_scr, o_scr, stf_scr, stb_scr, *, seq, chunk):
    c = chunk
    n = seq // c
    scale = GLA_DK ** -0.5

    sm = small_ref[...].astype(BF16)
    laf_scr[...] = _log_sigmoid(_dot(sm, w2f_ref[...].astype(BF16)) + bf_ref[...]) * (1.0 / GLA_GATE_TAU)
    lab_scr[...] = _log_sigmoid(_dot(sm, w2b_ref[...].astype(BF16)) + bb_ref[...]) * (1.0 / GLA_GATE_TAU)
    stf_scr[...] = jnp.zeros_like(stf_scr)
    stb_scr[...] = jnp.zeros_like(stb_scr)

    row = _iota2((c, c), 0)
    col = _iota2((c, c), 1)
    low = row >= col
    upp = row <= col
    low_m = jnp.concatenate([jnp.where(low, 1.0, 0.0).astype(BF16)] * 2, axis=1)
    upp_m = jnp.concatenate([jnp.where(upp, 1.0, 0.0).astype(BF16)] * 2, axis=1)

    g = GLA_GROUP

    def finish(rows, o):
        y = o * lax.rsqrt(jnp.mean(o * o, axis=-1, keepdims=True) + EPS) * nw_ref[...]
        y = y * _silu(gr_ref[rows, :].astype(F32))
        y = y * _sigmoid(ma_ref[rows, :].astype(F32))
        out_ref[rows, :] = y.astype(BF16)

    def group(gi, second_touch):
        ids = [gi * g + j for j in range(g)] + [n - 1 - gi * g - j for j in range(g)]
        rows = [pl.ds(pl.multiple_of(i * c, c), c) for i in ids]
        la = [laf_scr[r, :] for r in rows[:g]] + [lab_scr[r, :] for r in rows[g:]]
        csum = [low_m] * g + [upp_m] * g
        mask = [low] * g + [upp] * g
        tot_row = [c - 1] * g + [0] * g
        qf = [q_ref[r, :].astype(F32) * scale for r in rows]
        kf = [k_ref[r, :].astype(F32) for r in rows]
        vc = [v_ref[r, :] for r in rows]

        cum = _each(_dot_lhs2_wide, csum, la)
        tot = _each(lambda x, r: x[r:r + 1, :], cum, tot_row)
        q_dec = _each(lambda q, x: (q * jnp.exp(x)).astype(BF16), qf, cum)
        k_inv = _each(lambda k, x: (k * jnp.exp(-x)).astype(BF16), kf, cum)
        k_tail = _each(lambda k, t, x: (k * jnp.exp(t - x)).astype(BF16), kf, tot, cum)
        s = _each(lambda m, q, k: jnp.where(m, _dot_nt(q, k), 0.0).astype(BF16), mask, q_dec, k_inv)
        o = _each(_dot, s, vc)
        kv = _each(_dot_tn, vc, k_tail)
        dec = _each(jnp.exp, tot)

        for st_scr, probs in ((stf_scr, range(g)), (stb_scr, range(g, 2 * g))):
            st = st_scr[...]
            for p in probs:
                o[p] = o[p] + _dot_nt(q_dec[p], st.astype(BF16))
                st = dec[p] * st + kv[p]
            st_scr[...] = st

        for r, o_p in zip(rows, o):
            if second_touch:
                finish(r, o_scr[r, :] + o_p)
            else:
                o_scr[r, :] = o_p

    def first_half(gi, carry):
        group(gi, False)
        return carry

    def second_half(gi, carry):
        group(gi, True)
        return carry

    n_groups = n // g
    lax.fori_loop(0, n_groups // 2, first_half, 0)
    lax.fori_loop(n_groups // 2, n_groups, second_half, 0)


def _gla(main, small, w2f_pad, w2b_pad, b_f, b_b, norm_w, batch, seq):
    n = batch * seq
    h = GLA_HEADS
    kern = functools.partial(_gla_kernel, seq=seq, chunk=GLA_CHUNK)
    qk_blk = lambda off: pl.BlockSpec((seq, GLA_DK), lambda b, hh, off=off: (b, off // GLA_DK + hh))
    v_blk = lambda off: pl.BlockSpec((seq, GLA_DV), lambda b, hh, off=off: (b, off // GLA_DV + hh))
    return pl.pallas_call(
        kern,
        grid=(batch, h),
        in_specs=[
            qk_blk(COL_GQ), qk_blk(COL_GK), v_blk(COL_GV), v_blk(COL_GR), v_blk(COL_MA),
            pl.BlockSpec((seq, LANES), lambda b, hh: (b, 0)),
            pl.BlockSpec((LANES, GLA_DK), lambda b, hh: (0, hh)),
            pl.BlockSpec((LANES, GLA_DK), lambda b, hh: (0, hh)),
            pl.BlockSpec((1, GLA_DK), lambda b, hh: (0, hh)),
            pl.BlockSpec((1, GLA_DK), lambda b, hh: (0, hh)),
            pl.BlockSpec((1, GLA_DV), lambda b, hh: (0, 0)),
        ],
        out_specs=pl.BlockSpec((seq, GLA_DV), lambda b, hh: (b, hh)),
        out_shape=jax.ShapeDtypeStruct((n, D_MODEL), BF16),
        scratch_shapes=[
            pltpu.VMEM((seq, GLA_DK), F32), pltpu.VMEM((seq, GLA_DK), F32),
            pltpu.VMEM((seq, GLA_DV), F32),
            pltpu.VMEM((GLA_DV, GLA_DK), F32), pltpu.VMEM((GLA_DV, GLA_DK), F32),
        ],
        compiler_params=pltpu.CompilerParams(
            dimension_semantics=("arbitrary", "arbitrary"), vmem_limit_bytes=VMEM_LIMIT),
        name="gla",
    )(main, main, main, main, main, small, w2f_pad, w2b_pad, b_f, b_b, norm_w)


TRI_BLOCK = 16


def _mm(a, b):
    return _dot(a.astype(BF16), b.astype(BF16))


def _nilpotent_inverse(a_list, eye, index, tick):
    t_list = _each(lambda a: eye - a, a_list)
    p_list = a_list
    power = 2
    while power < index:
        p_list = _each(lambda p: _mm(p, p), p_list)
        tick()
        t_list = _each(lambda t, p: t + _mm(t, p), t_list, p_list)
        tick()
        power *= 2
    return t_list


def _tri_inverse(a_list, eye, diag_blocks, chunk, tick):
    ad_list = _each(lambda a: jnp.where(diag_blocks, a, 0.0), a_list)
    ao_list = _each(lambda a: jnp.where(diag_blocks, 0.0, a), a_list)
    d_list = _nilpotent_inverse(ad_list, eye, TRI_BLOCK, tick)
    n_list = _each(_mm, d_list, ao_list)
    tick()
    t_list = _nilpotent_inverse(n_list, eye, chunk // TRI_BLOCK, tick)
    out = _each(_mm, t_list, d_list)
    tick()
    return out


def _gdn_kernel(gate_ref, q_ref, k_ref, v_ref, z_ref, mb_ref, small_ref, cwq_ref, cwk_ref, cwv_ref,
                nw_ref, out_ref, pad_scr, qs_scr, ks_scr, vs_scr, gf_scr, gb_scr, btf_scr, btb_scr,
                o_scr, nmat_scr, bmat_scr, qp_scr, cd_scr, sf_scr, sb_scr, *, seq, chunk, n_heads_total):
    c = chunk
    n = seq // c
    step = pl.program_id(0)
    hh = lax.rem(jnp.minimum(step, n_heads_total - 1), GDN_HEADS)
    cur = lax.rem(step, 2)
    prev = 1 - cur
    scale = GDN_DK ** -0.5

    @pl.when(step == 0)
    def _():
        nmat_scr[...] = jnp.zeros_like(nmat_scr)
        bmat_scr[...] = jnp.zeros_like(bmat_scr)
        qp_scr[...] = jnp.zeros_like(qp_scr)
        cd_scr[...] = jnp.zeros_like(cd_scr)
        o_scr[...] = jnp.zeros_like(o_scr)

    zeros8 = jnp.zeros((SUBLANES, LANES), F32)
    pad_scr[0:SUBLANES, :] = zeros8
    pad_scr[seq + SUBLANES:seq + 2 * SUBLANES, :] = zeros8
    half = GDN_CONV // 2

    def conv_into(src_ref, cw_ref, dst_ref, normalise, mult):
        pad_scr[SUBLANES:seq + SUBLANES, :] = src_ref[...].astype(F32)
        w = cw_ref[...]

        def body(i, carry):
            r0 = pl.multiple_of(i * CONV_ROWS, CONV_ROWS)
            acc = jnp.zeros((CONV_ROWS, LANES), F32)
            for j in range(GDN_CONV):
                tap = pad_scr[pl.ds(r0 + (SUBLANES - half + j), CONV_ROWS), :]
                acc = acc + tap * w[j:j + 1, :]
            y = _silu(acc)
            if normalise:
                y = y * lax.rsqrt(jnp.sum(y * y, axis=-1, keepdims=True) + EPS) * mult
            dst_ref[pl.ds(r0, CONV_ROWS), :] = y.astype(BF16)
            return carry

        lax.fori_loop(0, seq // CONV_ROWS, body, 0, unroll=4)

    conv_into(q_ref, cwq_ref, qs_scr, True, scale)
    conv_into(k_ref, cwk_ref, ks_scr, True, 1.0)
    conv_into(v_ref, cwv_ref, vs_scr, False, 1.0)

    sm = small_ref[...]
    lane = _iota2(sm.shape, 1)
    log_decay = -jnp.exp(gate_ref[0:1, :]) * _softplus(sm + gate_ref[1:2, :])
    gate_vals = jnp.where(lane < SMALL_BF, log_decay, _sigmoid(sm))
    gate_hl = jnp.concatenate(_split2(gate_vals), axis=1)
    sel_lane = jnp.bitwise_and(_iota2((2 * LANES, 4 * LANES), 0), LANES - 1)
    sel_gate = jnp.right_shift(_iota2((2 * LANES, 4 * LANES), 1), LANES.bit_length() - 1)
    sel = jnp.where(sel_lane == SMALL_AF + GDN_HEADS * sel_gate + hh, 1.0, 0.0).astype(BF16)
    spread = _dot(gate_hl, sel)
    gf_scr[...] = spread[:, 0 * LANES:1 * LANES]
    gb_scr[...] = spread[:, 1 * LANES:2 * LANES]
    btf_scr[...] = spread[:, 2 * LANES:3 * LANES]
    btb_scr[...] = spread[:, 3 * LANES:4 * LANES]
    sf_scr[...] = jnp.zeros_like(sf_scr)
    sb_scr[...] = jnp.zeros_like(sb_scr)

    row = _iota2((c, c), 0)
    col = _iota2((c, c), 1)
    eye = jnp.where(row == col, 1.0, 0.0).astype(F32)
    low, slow = row >= col, row > col
    upp, supp = row <= col, row < col
    assert c == LANES, "the decay matrix is formed from a [c, 128] lane-broadcast column and its transpose"

    tri_shift = TRI_BLOCK.bit_length() - 1
    diag_blocks = jnp.right_shift(row, tri_shift) == jnp.right_shift(col, tri_shift)

    cur_slot, prev_slot = cur * (2 * n), prev * (2 * n)
    cur_row, prev_row = cur * seq, prev * seq

    def scan_step(i):
        j = n - 1 - i
        slots = [prev_slot + i, prev_slot + n + j]
        rows = [pl.ds(pl.multiple_of(prev_row + i * c, c), c), pl.ds(pl.multiple_of(prev_row + j * c, c), c)]
        states = [sf_scr, sb_scr]
        s = [ref[...] for ref in states]
        s_b = _each(lambda x: x.astype(BF16), s)
        ns = _each(lambda sl, x: _dot(nmat_scr[sl], x), slots, s_b)
        qs = _each(lambda sl, x: _dot(qp_scr[sl], x), slots, s_b)
        for ref, sl, s_, ns_ in zip(states, slots, s, ns):
            ref[...] = cd_scr[sl][0:1, :] * s_ + (bmat_scr[sl] - ns_)
        for r, q in zip(rows, qs):
            o_scr[r, :] += q

    n_groups = n // GDN_PREP_GROUP
    scans_per_group = n // n_groups

    def prep_group(gi, carry):
        pending = [gi * scans_per_group + j for j in range(scans_per_group)]

        def tick():
            if pending:
                scan_step(pending.pop(0))

        chunk_ids = [gi * GDN_PREP_GROUP + j for j in range(GDN_PREP_GROUP)]
        rows = [pl.ds(pl.multiple_of(i * c, c), c) for i in chunk_ids]
        qc = [qs_scr[r, :] for r in rows]
        kc = [ks_scr[r, :] for r in rows]
        vc = [vs_scr[r, :] for r in rows]
        gl = [ref[r, :] for r in rows for ref in (gf_scr, gb_scr)]
        bt = [ref[r, :] for r in rows for ref in (btf_scr, btb_scr)]
        reverse = [False, True] * GDN_PREP_GROUP
        incl = [low, upp] * GDN_PREP_GROUP
        strict = [slow, supp] * GDN_PREP_GROUP
        tot_row = [c - 1, 0] * GDN_PREP_GROUP

        def both(per_chunk):
            return [x for x in per_chunk for _ in range(2)]

        kk = both(_each(_dot_nt, kc, kc))
        qk = both(_each(_dot_nt, qc, kc))
        tick()
        qf = both(_each(lambda x: x.astype(F32), qc))
        kf = both(_each(lambda x: x.astype(F32), kc))
        vf = both(_each(lambda x: x.astype(F32), vc))

        gc = _each(_cumsum_rows, gl, reverse)
        tot = _each(lambda g, r: g[r:r + 1, :], gc, tot_row)
        e = _each(lambda g, inc: jnp.exp(jnp.where(inc, g - jnp.transpose(g), 0.0)), gc, incl)
        a = _each(lambda kk_, b, e_, st: kk_ * b[:, :c] * jnp.where(st, e_, 0.0), kk, bt, e, strict)
        t_inv = _tri_inverse(a, eye, diag_blocks, c, tick)
        egc = _each(jnp.exp, gc)
        wu = _each(lambda t, k, v, b, eg: _mm(t, jnp.concatenate([k * b * eg, v * b], axis=1)).astype(BF16),
                   t_inv, kf, vf, bt, egc)
        tick()
        attn = _each(lambda qk_, e_, inc: (qk_ * jnp.where(inc, e_, 0.0)).astype(BF16), qk, e, incl)
        k_tail = _each(lambda k, t, g: (k * jnp.exp(t - g)).astype(BF16), kf, tot, gc)
        kwu = _each(_dot_tn, k_tail, wu)
        tick()
        awu = _each(_dot, attn, wu)
        while pending:
            tick()

        for p in range(2 * GDN_PREP_GROUP):
            slot = cur_slot + chunk_ids[p // 2] + (p % 2) * n
            nmat_scr[slot] = kwu[p][:, :GDN_DK].astype(BF16)
            bmat_scr[slot] = kwu[p][:, GDN_DK:]
            qp_scr[slot] = (qf[p] * egc[p] - awu[p][:, :GDN_DK]).astype(BF16)
            cd_scr[slot] = jnp.broadcast_to(jnp.exp(tot[p]), (SUBLANES, LANES))
        for j, i in enumerate(chunk_ids):
            r = pl.ds(pl.multiple_of(cur_row + i * c, c), c)
            o_scr[r, :] = awu[2 * j][:, GDN_DK:] + awu[2 * j + 1][:, GDN_DK:]
        return carry

    lax.fori_loop(0, n_groups, prep_group, 0)

    def finish(i, carry):
        r0 = pl.multiple_of(i * CONV_ROWS, CONV_ROWS)
        o = o_scr[pl.ds(pl.multiple_of(prev_row + r0, CONV_ROWS), CONV_ROWS), :]
        y = o * lax.rsqrt(jnp.mean(o * o, axis=-1, keepdims=True) + EPS) * nw_ref[...]
        y = y * _silu(z_ref[pl.ds(r0, CONV_ROWS), :].astype(F32))
        y = y * _sigmoid(mb_ref[pl.ds(r0, CONV_ROWS), :].astype(F32))
        out_ref[pl.ds(r0, CONV_ROWS), :] = y.astype(BF16)
        return carry

    lax.fori_loop(0, seq // CONV_ROWS, finish, 0, unroll=4)


def _gdn(main, small, gates, conv_w, norm_w, batch, seq):
    n = batch * seq
    total = batch * GDN_HEADS
    kern = functools.partial(_gdn_kernel, seq=seq, chunk=GDN_CHUNK, n_heads_total=total)
    n_chunks = seq // GDN_CHUNK

    def head_of(step):
        idx = jnp.minimum(step, total - 1)
        return idx // GDN_HEADS, idx % GDN_HEADS

    def prev_head_of(step):
        idx = jnp.maximum(step - 1, 0)
        return idx // GDN_HEADS, idx % GDN_HEADS

    def blk(off, which):
        def index(s):
            b, hh = which(s)
            return b, off // LANES + hh
        return pl.BlockSpec((seq, LANES), index)

    def cw(part):
        return pl.BlockSpec((GDN_CONV, LANES), lambda s: (0, part * GDN_HEADS + head_of(s)[1]))

    seq_f32 = lambda: pltpu.VMEM((seq, LANES), F32)
    seq_bf16 = lambda: pltpu.VMEM((seq, LANES), BF16)
    return pl.pallas_call(
        kern,
        grid=(total + 1,),
        in_specs=[
            pl.BlockSpec((SUBLANES, LANES), lambda s: (0, 0)),
            blk(COL_DQ, head_of), blk(COL_DK, head_of), blk(COL_DV, head_of),
            blk(COL_DZ, prev_head_of), blk(COL_MB, prev_head_of),
            pl.BlockSpec((seq, LANES), lambda s: (head_of(s)[0], 0)),
            cw(0), cw(1), cw(2),
            pl.BlockSpec((1, GDN_DV), lambda s: (0, 0)),
        ],
        out_specs=pl.BlockSpec((seq, GDN_DV), lambda s: prev_head_of(s)),
        out_shape=jax.ShapeDtypeStruct((n, D_MODEL), BF16),
        scratch_shapes=[
            pltpu.VMEM((seq + 2 * SUBLANES, LANES), F32),
            seq_bf16(), seq_bf16(), seq_bf16(),
            seq_f32(), seq_f32(), seq_f32(), seq_f32(),
            pltpu.VMEM((2 * seq, LANES), F32),
            pltpu.VMEM((4 * n_chunks, GDN_DK, GDN_DK), BF16),
            pltpu.VMEM((4 * n_chunks, GDN_DK, GDN_DV), F32),
            pltpu.VMEM((4 * n_chunks, GDN_CHUNK, GDN_DK), BF16),
            pltpu.VMEM((4 * n_chunks, SUBLANES, LANES), F32),
            pltpu.VMEM((GDN_DK, GDN_DV), F32), pltpu.VMEM((GDN_DK, GDN_DV), F32),
        ],
        compiler_params=pltpu.CompilerParams(
            dimension_semantics=("arbitrary",), vmem_limit_bytes=VMEM_LIMIT),
        name="gdn",
    )(gates, main, main, main, main, main, small, conv_w, conv_w, conv_w, norm_w)


def _outproj_kernel(ga_ref, gb_ref, x_ref, wo_ref, nw_ref, wrh_ref, wrl_ref,
                    x1_ref, h2_ref, ri_ref, rw_ref, cnt_ref, carry_scr, *, tm):
    @pl.when(pl.program_id(0) == 0)
    def _():
        carry_scr[...] = jnp.zeros_like(carry_scr)

    sub = OUTPROJ_SUB
    subs = [pl.ds(j * sub, sub) for j in range(tm // sub)]
    lane_i = _iota2((sub, LANES), 1)
    lane = lane_i.astype(F32)

    mixed = [(ga_ref[r, :].astype(F32) + gb_ref[r, :].astype(F32)).astype(BF16) for r in subs]
    x1 = _each(lambda r, m: x_ref[r, :] + _dot(m, wo_ref[...]), subs, mixed)
    for r, v in zip(subs, x1):
        x1_ref[r, :] = v
    h2 = _each(lambda v: v * lax.rsqrt(jnp.mean(v * v, axis=-1, keepdims=True) + EPS) * nw_ref[...], x1)
    for j, v in enumerate(h2):
        _store_row_tiles(h2_ref.at[pl.ds(j * sub * ROW_TILE, sub * ROW_TILE)], v)
    hl = _each(_split2, h2)
    lg = _each(lambda p: _dot(p[0], wrh_ref[...]) + _dot(p[1], wrh_ref[...]) + _dot(p[0], wrl_ref[...]), hl)

    def row_max(vals):
        return _each(lambda v: jnp.max(v, axis=-1, keepdims=True), vals)

    def first_lane_of(vals, maxima):
        return _each(lambda v, m: jnp.min(jnp.where(v == m, lane, float(LANES)), axis=-1, keepdims=True),
                     vals, maxima)

    is_g = (lane_i >= ROUTE_GROUP_LANE) & (lane_i < ROUTE_GROUP_LANE + N_GROUPS)
    gl = _each(lambda v: jnp.where(is_g, v, NEG_INF), lg)
    gmax = row_max(gl)
    gidx = _each(lambda i: i - float(ROUTE_GROUP_LANE), first_lane_of(gl, gmax))
    gsum = _each(lambda v, m: jnp.sum(jnp.where(is_g, jnp.exp(v - m), 0.0), axis=-1, keepdims=True), lg, gmax)
    lane_group = jnp.right_shift(lane_i, EXPERTS_PER_GROUP.bit_length() - 1).astype(F32)
    el = _each(lambda v, g: jnp.where((lane_i < N_EXPERTS) & (lane_group == g), v, NEG_INF), lg, gidx)
    m1 = row_max(el)
    e0 = first_lane_of(el, m1)
    el2 = _each(lambda v, i: jnp.where(lane == i, NEG_INF, v), el, e0)
    m2 = row_max(el2)
    e1 = first_lane_of(el2, m2)
    ratio = _each(lambda a, b: jnp.exp(b - a), m1, m2)
    w0 = _each(lambda s_, r: 1.0 / (s_ * (1.0 + r)), gsum, ratio)
    w1 = _each(lambda w, r: w * r, w0, ratio)

    pick0 = _each(lambda i: lane == i, e0)
    pick1 = _each(lambda i: lane == i, e1)
    onehot = jnp.concatenate(_each(lambda p, q: jnp.where(p | q, 1.0, 0.0), pick0, pick1), axis=0)
    trow = _iota2((tm, tm), 0)
    tcol = _iota2((tm, tm), 1)
    before = jnp.where(trow > tcol, 1.0, 0.0).astype(BF16)
    cnt = _dot(before, onehot.astype(BF16)) + carry_scr[0:1, :]
    cnts = [cnt[j * sub:(j + 1) * sub, :] for j in range(tm // sub)]
    rank0 = _each(lambda p, c_: jnp.sum(jnp.where(p, c_, 0.0), axis=-1, keepdims=True), pick0, cnts)
    rank1 = _each(lambda p, c_: jnp.sum(jnp.where(p, c_, 0.0), axis=-1, keepdims=True), pick1, cnts)
    total = carry_scr[0:1, :] + jnp.sum(onehot, axis=0, keepdims=True)
    carry_scr[...] = jnp.broadcast_to(total, carry_scr.shape)
    cnt_ref[...] = jnp.broadcast_to(total, cnt_ref.shape).astype(jnp.int32)

    for j, r in enumerate(subs):
        ri = jnp.where(lane_i == 0, e0[j], jnp.where(lane_i == 1, e1[j], 0.0))
        ri = jnp.where(lane_i == 2, rank0[j], jnp.where(lane_i == 3, rank1[j], ri))
        ri_ref[:, r] = jnp.transpose(ri)[0:SUBLANES, :].astype(jnp.int32)
        rw_ref[r, :] = jnp.where(lane_i == 0, w0[j], jnp.where(lane_i == 1, w1[j], 0.0))


def _outproj(ga, gb, x2, w_out, norm_w, wr_hi, wr_lo):
    n = x2.shape[0]
    tm = OUTPROJ_TM
    kern = functools.partial(_outproj_kernel, tm=tm)
    row_blk = lambda w: pl.BlockSpec((tm, w), lambda i: (i, 0))
    const = lambda shape: pl.BlockSpec(shape, lambda i: (0, 0))
    return pl.pallas_call(
        kern,
        grid=(n // tm,),
        in_specs=[
            row_blk(D_MODEL), row_blk(D_MODEL), row_blk(D_MODEL),
            const((D_MODEL, D_MODEL)), const((1, D_MODEL)),
            const((D_MODEL, LANES)), const((D_MODEL, LANES)),
        ],
        out_specs=[row_blk(D_MODEL),
                   pl.BlockSpec((tm * ROW_TILE, LANES), lambda i: (i, 0)),
                   pl.BlockSpec((SUBLANES, tm), lambda i: (0, i)),
                   row_blk(LANES),
                   const((SUBLANES, LANES))],
        out_shape=[
            jax.ShapeDtypeStruct((n, D_MODEL), F32),
            jax.ShapeDtypeStruct((n * ROW_TILE, LANES), U32),
            jax.ShapeDtypeStruct((SUBLANES, n), jnp.int32),
            jax.ShapeDtypeStruct((n, LANES), F32),
            jax.ShapeDtypeStruct((SUBLANES, LANES), jnp.int32),
        ],
        scratch_shapes=[pltpu.VMEM((SUBLANES, LANES), F32)],
        compiler_params=pltpu.CompilerParams(
            dimension_semantics=("arbitrary",), vmem_limit_bytes=VMEM_LIMIT),
        name="outproj",
    )(ga, gb, x2, w_out, norm_w, wr_hi, wr_lo)


def _row_copy(src_ref, src_row, dst_ref, dst_row, sem):
    src = src_ref.at[pl.ds(pl.multiple_of(src_row * ROW_TILE, ROW_TILE), ROW_TILE)]
    dst = dst_ref.at[pl.ds(pl.multiple_of(dst_row * ROW_TILE, ROW_TILE), ROW_TILE)]
    return pltpu.make_async_copy(src, dst, sem)


def _scatter_kernel(seg_ref, d0_ref, d1_ref, h2_ref, xs_ref, zero_scr, sem, zsem, *, tile):
    i = pl.program_id(0)

    def issue(t, carry):
        _row_copy(h2_ref, t, xs_ref, d0_ref[t], sem).start(priority=0)
        _row_copy(h2_ref, t, xs_ref, d1_ref[t], sem).start(priority=1)
        return carry

    lax.fori_loop(0, tile, issue, 0, unroll=DMA_UNROLL)

    @pl.when(i == 0)
    def _():
        zero_scr[...] = jnp.zeros_like(zero_scr)

        def per_expert(e, carry):
            lo, hi = seg_ref[0, e], seg_ref[1, e]

            def start(r, c2):
                _row_copy(zero_scr, 0, xs_ref, r, zsem).start()
                return c2

            def wait(r, c2):
                _row_copy(zero_scr, 0, xs_ref, r, zsem).wait()
                return c2

            lax.fori_loop(lo, hi, start, 0)
            lax.fori_loop(lo, hi, wait, 0)
            return carry

        lax.fori_loop(0, N_EXPERTS + 1, per_expert, 0)

    def drain(t, carry):
        _row_copy(h2_ref, 0, xs_ref, 0, sem).wait()
        _row_copy(h2_ref, 0, xs_ref, 0, sem).wait()
        return carry

    lax.fori_loop(0, tile, drain, 0, unroll=DMA_UNROLL)


def _scatter(seg, dest0, dest1, h2t, n_rows):
    n = dest0.shape[0]
    tile = SCATTER_T
    kern = functools.partial(_scatter_kernel, tile=tile)
    return pl.pallas_call(
        kern,
        grid=(n // tile,),
        in_specs=[
            pl.BlockSpec(memory_space=pltpu.SMEM),
            pl.BlockSpec((tile,), lambda i: (i,), memory_space=pltpu.SMEM),
            pl.BlockSpec((tile,), lambda i: (i,), memory_space=pltpu.SMEM),
            pl.BlockSpec((tile * ROW_TILE, LANES), lambda i: (i, 0)),
        ],
        out_specs=pl.BlockSpec(memory_space=pl.ANY),
        out_shape=jax.ShapeDtypeStruct((n_rows * ROW_TILE, LANES), U32),
        scratch_shapes=[pltpu.VMEM((SUBLANES, LANES), U32),
                        pltpu.SemaphoreType.DMA, pltpu.SemaphoreType.DMA],
        compiler_params=pltpu.CompilerParams(
            dimension_semantics=("arbitrary",), vmem_limit_bytes=VMEM_LIMIT),
        name="scatter",
    )(seg, dest0, dest1, h2t)


def _expert_kernel(be_ref, nv_ref, xs_ref, wg_ref, wu_ref, wd_ref, y_ref):
    i = pl.program_id(0)

    @pl.when(i < nv_ref[0])
    def _():
        x = _load_row_tiles(xs_ref).astype(BF16)
        g = _dot(x, wg_ref[0].astype(BF16))
        u = _dot(x, wu_ref[0].astype(BF16))
        hid = (_silu(g) * u).astype(BF16)
        _store_row_tiles(y_ref, _dot(hid, wd_ref[0].astype(BF16)))

    @pl.when(i >= nv_ref[0])
    def _():
        y_ref[...] = jnp.zeros_like(y_ref)


def _experts(block_expert, n_valid, xs, w_gate, w_up, w_down):
    blk = MOE_BLOCK
    n_rows = xs.shape[0] // ROW_TILE
    grid_spec = pltpu.PrefetchScalarGridSpec(
        num_scalar_prefetch=2,
        grid=(n_rows // blk,),
        in_specs=[
            pl.BlockSpec((blk * ROW_TILE, LANES), lambda i, be, nv: (jnp.minimum(i, nv[0] - 1), 0)),
            pl.BlockSpec((1, D_MODEL, D_EXPERT), lambda i, be, nv: (be[i], 0, 0)),
            pl.BlockSpec((1, D_MODEL, D_EXPERT), lambda i, be, nv: (be[i], 0, 0)),
            pl.BlockSpec((1, D_EXPERT, D_MODEL), lambda i, be, nv: (be[i], 0, 0)),
        ],
        out_specs=pl.BlockSpec((blk * ROW_TILE, LANES), lambda i, be, nv: (i, 0)),
    )
    return pl.pallas_call(
        _expert_kernel,
        grid_spec=grid_spec,
        out_shape=jax.ShapeDtypeStruct((n_rows * ROW_TILE, LANES), U32),
        compiler_params=pltpu.CompilerParams(
            dimension_semantics=("arbitrary",), vmem_limit_bytes=VMEM_LIMIT),
        name="experts",
    )(block_expert, n_valid, xs, w_gate, w_up, w_down)


def _combine_kernel(d0_ref, d1_ref, x1_ref, rw_ref, nw_ref, y_ref, out_ref, ya_scr, yb_scr, sem, *, tile):
    def issue(t, carry):
        _row_copy(y_ref, d0_ref[t], ya_scr, t, sem).start(priority=0)
        _row_copy(y_ref, d1_ref[t], yb_scr, t, sem).start(priority=1)
        return carry

    def drain(t, carry):
        _row_copy(y_ref, 0, ya_scr, 0, sem).wait()
        _row_copy(y_ref, 0, yb_scr, 0, sem).wait()
        return carry

    lax.fori_loop(0, tile, issue, 0, unroll=DMA_UNROLL)
    lax.fori_loop(0, tile, drain, 0, unroll=DMA_UNROLL)

    rw = rw_ref[...]
    moe = rw[:, 0:1] * _load_row_tiles(ya_scr) + rw[:, 1:2] * _load_row_tiles(yb_scr)
    x2 = x1_ref[...] + moe
    out_ref[...] = x2 * lax.rsqrt(jnp.mean(x2 * x2, axis=-1, keepdims=True) + EPS) * nw_ref[...]


def _combine(dest0, dest1, x1, rw, norm_w, y):
    n = x1.shape[0]
    tile = COMBINE_T
    kern = functools.partial(_combine_kernel, tile=tile)
    return pl.pallas_call(
        kern,
        grid=(n // tile,),
        in_specs=[
            pl.BlockSpec((tile,), lambda i: (i,), memory_space=pltpu.SMEM),
            pl.BlockSpec((tile,), lambda i: (i,), memory_space=pltpu.SMEM),
            pl.BlockSpec((tile, D_MODEL), lambda i: (i, 0)),
            pl.BlockSpec((tile, LANES), lambda i: (i, 0)),
            pl.BlockSpec((1, D_MODEL), lambda i: (0, 0)),
            pl.BlockSpec(memory_space=pl.ANY),
        ],
        out_specs=pl.BlockSpec((tile, D_MODEL), lambda i: (i, 0)),
        out_shape=jax.ShapeDtypeStruct((n, D_MODEL), F32),
        scratch_shapes=[
            pltpu.VMEM((tile * ROW_TILE, LANES), U32), pltpu.VMEM((tile * ROW_TILE, LANES), U32),
            pltpu.SemaphoreType.DMA,
        ],
        compiler_params=pltpu.CompilerParams(
            dimension_semantics=("arbitrary",), vmem_limit_bytes=VMEM_LIMIT),
        name="combine",
    )(dest0, dest1, x1, rw, norm_w, y)


def _pad_cols(w, width):
    return jnp.pad(w, ((0, 0), (0, width - w.shape[1])))


def _token_mixer_and_moe(x, norm1_w, w_in, w2_f, b_f, w2_b, b_b, gla_norm_w, conv_w, a_log_f, dt_bias_f,
                         a_log_b, dt_bias_b, gdn_norm_w, w_out, norm2_w, w_group, w_router, w_gate, w_up,
                         w_down, out_norm_w):
    batch, seq, d = x.shape
    n = batch * seq
    x2 = x.reshape(n, d)

    w_main = jnp.concatenate([w_in[:, :3072], w_in[:, 3104:7200], w_in[:, 7232:]], axis=1).astype(BF16)
    w_small = _pad_cols(jnp.concatenate([w_in[:, 3072:3104], w_in[:, 7200:7232]], axis=1), LANES)
    ws_hi, ws_lo = _split2(w_small)
    main, small = _inproj(x2, norm1_w.reshape(1, d), w_main, ws_hi, ws_lo)

    w2f_pad = jnp.zeros((LANES, GLA_HEADS * GLA_DK), F32).at[0:GLA_GATE_RANK].set(w2_f)
    w2b_pad = jnp.zeros((LANES, GLA_HEADS * GLA_DK), F32).at[GLA_GATE_RANK:2 * GLA_GATE_RANK].set(w2_b)
    ga = _gla(main, small, w2f_pad, w2b_pad, b_f.reshape(1, -1), b_b.reshape(1, -1),
              gla_norm_w.reshape(1, -1), batch, seq)

    gates = jnp.zeros((SUBLANES, LANES), F32)
    gates = gates.at[0, SMALL_AF:SMALL_BF].set(jnp.concatenate([a_log_f, a_log_b]))
    gates = gates.at[1, SMALL_AF:SMALL_BF].set(jnp.concatenate([dt_bias_f, dt_bias_b]))
    gb = _gdn(main, small, gates, conv_w, gdn_norm_w.reshape(1, -1), batch, seq)

    w_route = _pad_cols(jnp.concatenate([w_router, w_group], axis=1), LANES)
    wr_hi, wr_lo = _split2(w_route)
    x1, h2t, rt, rw, counts = _outproj(ga, gb, x2, w_out.astype(BF16), norm2_w.reshape(1, d), wr_hi, wr_lo)

    blk = MOE_BLOCK
    cnt = counts[0, :N_EXPERTS]
    padded = (cnt + blk - 1) // blk * blk
    ends = jnp.cumsum(padded)
    pstart = (ends - padded).astype(jnp.int32)
    n_blocks = -(-(2 * n + N_EXPERTS * (blk - 1)) // blk)
    n_rows = n_blocks * blk
    block_row = jnp.arange(n_blocks, dtype=jnp.int32) * blk
    block_expert = jnp.minimum(
        jnp.sum((ends[None, :] <= block_row[:, None]).astype(jnp.int32), axis=1), N_EXPERTS - 1)
    n_valid = (ends[-1:] // blk).astype(jnp.int32)
    seg = jnp.stack([jnp.append(pstart + cnt, ends[-1]), jnp.append(ends, n_rows)]).astype(jnp.int32)

    experts = jnp.arange(N_EXPERTS, dtype=jnp.int32)
    seg_start = jnp.sum(jnp.where(rt[0:2, :, None] == experts, pstart, 0), axis=-1)
    dest = seg_start + rt[2:4]
    dest0, dest1 = dest[0], dest[1]

    xs = _scatter(seg, dest0, dest1, h2t, n_rows)
    y = _experts(block_expert, n_valid, xs, w_gate, w_up, w_down)
    out = _combine(dest0, dest1, x1, rw, out_norm_w.reshape(1, d), y)
    return out.reshape(batch, seq, d)


def kernel(x, norm1_w, w_in, gla_gate_w2_fwd, gla_gate_b_fwd, gla_gate_w2_bwd, gla_gate_b_bwd, gla_norm_w,
           gdn_conv_w, gdn_a_log_fwd, gdn_dt_bias_fwd, gdn_a_log_bwd, gdn_dt_bias_bwd, gdn_norm_w, w_out,
           norm2_w, moe_w_group, moe_w_router, moe_w_gate, moe_w_up, moe_w_down, norm_f_w):
    assert norm1_w.shape[0] == 1, "single-layer block"
    return _token_mixer_and_moe(
        x, norm1_w[0], w_in[0], gla_gate_w2_fwd[0], gla_gate_b_fwd[0], gla_gate_w2_bwd[0], gla_gate_b_bwd[0],
        gla_norm_w[0], gdn_conv_w[0], gdn_a_log_fwd[0], gdn_dt_bias_fwd[0], gdn_a_log_bwd[0],
        gdn_dt_bias_bwd[0], gdn_norm_w[0], w_out[0], norm2_w[0], moe_w_group[0], moe_w_router[0],
        moe_w_gate[0], moe_w_up[0], moe_w_down[0], norm_f_w)
```

```python
import functools

import jax
import jax.numpy as jnp
import numpy as np
from jax import lax
from jax.experimental import pallas as pl
from jax.experimental.pallas import tpu as pltpu

F32 = jnp.float32
BF16 = jnp.bfloat16
U32 = jnp.uint32

D_MODEL = 1024
GLA_HEADS = 4
GLA_DK = 128
GLA_DV = 256
GLA_GATE_RANK = 16
GLA_GATE_TAU = 16.0
GLA_CHUNK = 64
GLA_GROUP = 8
GDN_HEADS = 8
GDN_DK = 128
GDN_DV = 128
GDN_CONV = 5
GDN_CHUNK = 128
GDN_PREP_GROUP = 8
N_GROUPS = 4
EXPERTS_PER_GROUP = 8
N_EXPERTS = N_GROUPS * EXPERTS_PER_GROUP
D_EXPERT = 256
EPS = 1e-6

LANES = 128
SUBLANES = 8
VMEM_LIMIT = 48 * 1024 * 1024

COL_GQ, COL_GK, COL_GV, COL_GR = 0, 512, 1024, 2048
COL_DQ, COL_DK, COL_DV, COL_DZ = 3072, 4096, 5120, 6144
COL_MA, COL_MB = 7168, 8192
D_MAIN = 9216
SMALL_AF, SMALL_AB, SMALL_BF, SMALL_BB = 32, 40, 48, 56
ROUTE_GROUP_LANE = 32

MOE_BLOCK = 256
ROW_TILE = D_MODEL // 2 // LANES
HIGH_HALF = np.uint32(0xFFFF0000)
DMA_UNROLL = 8
INPROJ_TM, INPROJ_TN = 1024, 1024
OUTPROJ_TM = 512
OUTPROJ_SUB = 128
SCATTER_T = 512
COMBINE_T = 256
CONV_ROWS = 256
NEG_INF = float("-inf")


def _dot(a, b):
    return jnp.dot(a, b, preferred_element_type=F32)


def _dot_nt(a, b):
    return lax.dot_general(a, b, (((1,), (1,)), ((), ())), preferred_element_type=F32)


def _dot_tn(a, b):
    return lax.dot_general(a, b, (((0,), (0,)), ((), ())), preferred_element_type=F32)


def _split2(x):
    hi = x.astype(BF16)
    lo = (x - hi.astype(F32)).astype(BF16)
    return hi, lo


def _split3(x):
    hi = x.astype(BF16)
    r = x - hi.astype(F32)
    mid = r.astype(BF16)
    lo = (r - mid.astype(F32)).astype(BF16)
    return hi, mid, lo


def _dot_exact_rhs(x, m_bf16):
    hi, mid, lo = _split3(x)
    return _dot(hi, m_bf16) + _dot(mid, m_bf16) + _dot(lo, m_bf16)


def _dot_exact_lhs(m_bf16, x):
    hi, mid, lo = _split3(x)
    return _dot(m_bf16, hi) + _dot(m_bf16, mid) + _dot(m_bf16, lo)


def _dot_lhs2(m_bf16, x):
    hi, lo = _split2(x)
    return _dot(m_bf16, hi) + _dot(m_bf16, lo)


def _dot_lhs2_wide(m2_bf16, x):
    return _dot(m2_bf16, jnp.concatenate(_split2(x), axis=0))


def _cumsum_rows(x, reverse):
    rows = x.shape[0]
    row = _iota2(x.shape, 0)
    shift = 1
    while shift < rows:
        if reverse:
            x = x + jnp.where(row < rows - shift, pltpu.roll(x, rows - shift, axis=0), 0.0)
        else:
            x = x + jnp.where(row >= shift, pltpu.roll(x, shift, axis=0), 0.0)
        shift *= 2
    return x


def _dot3(a, b):
    ah, al = _split2(a)
    bh, bl = _split2(b)
    return _dot(ah, bh) + _dot(al, bh) + _dot(ah, bl)


def _store_row_tiles(ref, x):
    rows = x.shape[0]
    half = D_MODEL // 2
    hi = lax.bitcast_convert_type(x[:, :half].astype(BF16).astype(F32), U32)
    lo = lax.bitcast_convert_type(x[:, half:].astype(BF16).astype(F32), U32)
    packed = jnp.bitwise_or(jnp.bitwise_and(hi, HIGH_HALF), jnp.right_shift(lo, 16))
    for j in range(ROW_TILE):
        ref[pl.ds(j, rows, stride=ROW_TILE), :] = packed[:, j * LANES:(j + 1) * LANES]


def _load_row_tiles(ref):
    rows = ref.shape[0] // ROW_TILE
    packed = jnp.concatenate([ref[pl.ds(j, rows, stride=ROW_TILE), :] for j in range(ROW_TILE)], axis=1)
    hi = lax.bitcast_convert_type(jnp.bitwise_and(packed, HIGH_HALF), F32)
    lo = lax.bitcast_convert_type(jnp.left_shift(packed, 16), F32)
    return jnp.concatenate([hi, lo], axis=1)


def _each(fn, *lists):
    return [fn(*args) for args in zip(*lists)]


def _sigmoid(x):
    return 1.0 / (1.0 + jnp.exp(-x))


def _silu(x):
    return x * _sigmoid(x)


def _softplus(x):
    return jnp.maximum(x, 0.0) + jnp.log(1.0 + jnp.exp(-jnp.abs(x)))


def _log_sigmoid(x):
    return jnp.minimum(x, 0.0) - jnp.log(1.0 + jnp.exp(-jnp.abs(x)))


def _iota2(shape, dim):
    return lax.broadcasted_iota(jnp.int32, shape, dim)


def _inproj_kernel(x_ref, nw_ref, w_ref, wsh_ref, wsl_ref, main_ref, small_ref, h_scr):
    @pl.when(pl.program_id(1) == 0)
    def _():
        x = x_ref[...]
        h = x * lax.rsqrt(jnp.mean(x * x, axis=-1, keepdims=True) + EPS) * nw_ref[...]
        hh, hl = _split2(h)
        h_scr[...] = hh
        small_ref[...] = _dot(hh, wsh_ref[...]) + _dot(hl, wsh_ref[...]) + _dot(hh, wsl_ref[...])

    main_ref[...] = _dot(h_scr[...], w_ref[...]).astype(BF16)


def _inproj(x2, norm_w, w_main, ws_hi, ws_lo):
    n = x2.shape[0]
    tm, tn = INPROJ_TM, INPROJ_TN
    return pl.pallas_call(
        _inproj_kernel,
        grid=(n // tm, D_MAIN // tn),
        in_specs=[
            pl.BlockSpec((tm, D_MODEL), lambda i, j: (i, 0)),
            pl.BlockSpec((1, D_MODEL), lambda i, j: (0, 0)),
            pl.BlockSpec((D_MODEL, tn), lambda i, j: (0, j)),
            pl.BlockSpec((D_MODEL, LANES), lambda i, j: (0, 0)),
            pl.BlockSpec((D_MODEL, LANES), lambda i, j: (0, 0)),
        ],
        out_specs=[
            pl.BlockSpec((tm, tn), lambda i, j: (i, j)),
            pl.BlockSpec((tm, LANES), lambda i, j: (i, 0)),
        ],
        out_shape=[
            jax.ShapeDtypeStruct((n, D_MAIN), BF16),
            jax.ShapeDtypeStruct((n, LANES), F32),
        ],
        scratch_shapes=[pltpu.VMEM((tm, D_MODEL), BF16)],
        compiler_params=pltpu.CompilerParams(
            dimension_semantics=("arbitrary", "arbitrary"), vmem_limit_bytes=VMEM_LIMIT),
        name="inproj",
    )(x2, norm_w, w_main, ws_hi, ws_lo)


def _gla_kernel(q_ref, k_ref, v_ref, gr_ref, ma_ref, small_ref, w2f_ref, w2b_ref, bf_ref, bb_ref,
                nw_ref, out_ref, laf_scr, lab_scr, o_scr, stf_scr, stb_scr, *, seq, chunk):
    c = chunk
    n = seq // c
    scale = GLA_DK ** -0.5

    sm = small_ref[...].astype(BF16)
    laf_scr[...] = _log_sigmoid(_dot(sm, w2f_ref[...].astype(BF16)) + bf_ref[...]) * (1.0 / GLA_GATE_TAU)
    lab_scr[...] = _log_sigmoid(_dot(sm, w2b_ref[...].astype(BF16)) + bb_ref[...]) * (1.0 / GLA_GATE_TAU)
    stf_scr[...] = jnp.zeros_like(stf_scr)
    stb_scr[...] = jnp.zeros_like(stb_scr)

    row = _iota2((c, c), 0)
    col = _iota2((c, c), 1)
    low = row >= col
    upp = row <= col
    low_m = jnp.concatenate([jnp.where(low, 1.0, 0.0).astype(BF16)] * 2, axis=1)
    upp_m = jnp.concatenate([jnp.where(upp, 1.0, 0.0).astype(BF16)] * 2, axis=1)

    g = GLA_GROUP

    def finish(rows, o):
        y = o * lax.rsqrt(jnp.mean(o * o, axis=-1, keepdims=True) + EPS) * nw_ref[...]
        y = y * _silu(gr_ref[rows, :].astype(F32))
        y = y * _sigmoid(ma_ref[rows, :].astype(F32))
        out_ref[rows, :] = y.astype(BF16)

    def group(gi, second_touch):
        ids = [gi * g + j for j in range(g)] + [n - 1 - gi * g - j for j in range(g)]
        rows = [pl.ds(pl.multiple_of(i * c, c), c) for i in ids]
        la = [laf_scr[r, :] for r in rows[:g]] + [lab_scr[r, :] for r in rows[g:]]
        csum = [low_m] * g + [upp_m] * g
        mask = [low] * g + [upp] * g
        tot_row = [c - 1] * g + [0] * g
        qf = [q_ref[r, :].astype(F32) * scale for r in rows]
        kf = [k_ref[r, :].astype(F32) for r in rows]
        vc = [v_ref[r, :] for r in rows]

        cum = _each(_dot_lhs2_wide, csum, la)
        tot = _each(lambda x, r: x[r:r + 1, :], cum, tot_row)
        q_dec = _each(lambda q, x: (q * jnp.exp(x)).astype(BF16), qf, cum)
        k_inv = _each(lambda k, x: (k * jnp.exp(-x)).astype(BF16), kf, cum)
        k_tail = _each(lambda k, t, x: (k * jnp.exp(t - x)).astype(BF16), kf, tot, cum)
        s = _each(lambda m, q, k: jnp.where(m, _dot_nt(q, k), 0.0).astype(BF16), mask, q_dec, k_inv)
        o = _each(_dot, s, vc)
        kv = _each(_dot_tn, vc, k_tail)
        dec = _each(jnp.exp, tot)

        for st_scr, probs in ((stf_scr, range(g)), (stb_scr, range(g, 2 * g))):
            st = st_scr[...]
            for p in probs:
                o[p] = o[p] + _dot_nt(q_dec[p], st.astype(BF16))
                st = dec[p] * st + kv[p]
            st_scr[...] = st

        for r, o_p in zip(rows, o):
            if second_touch:
                finish(r, o_scr[r, :] + o_p)
            else:
                o_scr[r, :] = o_p

    def first_half(gi, carry):
        group(gi, False)
        return carry

    def second_half(gi, carry):
        group(gi, True)
        return carry

    n_groups = n // g
    lax.fori_loop(0, n_groups // 2, first_half, 0)
    lax.fori_loop(n_groups // 2, n_groups, second_half, 0)


def _gla(main, small, w2f_pad, w2b_pad, b_f, b_b, norm_w, batch, seq):
    n = batch * seq
    h = GLA_HEADS
    kern = functools.partial(_gla_kernel, seq=seq, chunk=GLA_CHUNK)
    qk_blk = lambda off: pl.BlockSpec((seq, GLA_DK), lambda b, hh, off=off: (b, off // GLA_DK + hh))
    v_blk = lambda off: pl.BlockSpec((seq, GLA_DV), lambda b, hh, off=off: (b, off // GLA_DV + hh))
    return pl.pallas_call(
        kern,
        grid=(batch, h),
        in_specs=[
            qk_blk(COL_GQ), qk_blk(COL_GK), v_blk(COL_GV), v_blk(COL_GR), v_blk(COL_MA),
            pl.BlockSpec((seq, LANES), lambda b, hh: (b, 0)),
            pl.BlockSpec((LANES, GLA_DK), lambda b, hh: (0, hh)),
            pl.BlockSpec((LANES, GLA_DK), lambda b, hh: (0, hh)),
            pl.BlockSpec((1, GLA_DK), lambda b, hh: (0, hh)),
            pl.BlockSpec((1, GLA_DK), lambda b, hh: (0, hh)),
            pl.BlockSpec((1, GLA_DV), lambda b, hh: (0, 0)),
        ],
        out_specs=pl.BlockSpec((seq, GLA_DV), lambda b, hh: (b, hh)),
        out_shape=jax.ShapeDtypeStruct((n, D_MODEL), BF16),
        scratch_shapes=[
            pltpu.VMEM((seq, GLA_DK), F32), pltpu.VMEM((seq, GLA_DK), F32),
            pltpu.VMEM((seq, GLA_DV), F32),
            pltpu.VMEM((GLA_DV, GLA_DK), F32), pltpu.VMEM((GLA_DV, GLA_DK), F32),
        ],
        compiler_params=pltpu.CompilerParams(
            dimension_semantics=("arbitrary", "arbitrary"), vmem_limit_bytes=VMEM_LIMIT),
        name="gla",
    )(main, main, main, main, main, small, w2f_pad, w2b_pad, b_f, b_b, norm_w)


TRI_BLOCK = 16


def _mm(a, b):
    return _dot(a.astype(BF16), b.astype(BF16))


def _nilpotent_inverse(a_list, eye, index, tick):
    t_list = _each(lambda a: eye - a, a_list)
    p_list = a_list
    power = 2
    while power < index:
        p_list = _each(lambda p: _mm(p, p), p_list)
        tick()
        t_list = _each(lambda t, p: t + _mm(t, p), t_list, p_list)
        tick()
        power *= 2
    return t_list


def _tri_inverse(a_list, eye, diag_blocks, chunk, tick):
    ad_list = _each(lambda a: jnp.where(diag_blocks, a, 0.0), a_list)
    ao_list = _each(lambda a: jnp.where(diag_blocks, 0.0, a), a_list)
    d_list = _nilpotent_inverse(ad_list, eye, TRI_BLOCK, tick)
    n_list = _each(_mm, d_list, ao_list)
    tick()
    t_list = _nilpotent_inverse(n_list, eye, chunk // TRI_BLOCK, tick)
    out = _each(_mm, t_list, d_list)
    tick()
    return out


def _gdn_kernel(gate_ref, q_ref, k_ref, v_ref, z_ref, mb_ref, small_ref, cwq_ref, cwk_ref, cwv_ref,
                nw_ref, out_ref, pad_scr, qs_scr, ks_scr, vs_scr, gf_scr, gb_scr, btf_scr, btb_scr,
                o_scr, nmat_scr, bmat_scr, qp_scr, cd_scr, sf_scr, sb_scr, *, seq, chunk, n_heads_total):
    c = chunk
    n = seq // c
    step = pl.program_id(0)
    hh = lax.rem(jnp.minimum(step, n_heads_total - 1), GDN_HEADS)
    cur = lax.rem(step, 2)
    prev = 1 - cur
    scale = GDN_DK ** -0.5

    @pl.when(step == 0)
    def _():
        nmat_scr[...] = jnp.zeros_like(nmat_scr)
        bmat_scr[...] = jnp.zeros_like(bmat_scr)
        qp_scr[...] = jnp.zeros_like(qp_scr)
        cd_scr[...] = jnp.zeros_like(cd_scr)
        o_scr[...] = jnp.zeros_like(o_scr)

    zeros8 = jnp.zeros((SUBLANES, LANES), F32)
    pad_scr[0:SUBLANES, :] = zeros8
    pad_scr[seq + SUBLANES:seq + 2 * SUBLANES, :] = zeros8
    half = GDN_CONV // 2

    def conv_into(src_ref, cw_ref, dst_ref, normalise, mult):
        pad_scr[SUBLANES:seq + SUBLANES, :] = src_ref[...].astype(F32)
        w = cw_ref[...]

        def body(i, carry):
            r0 = pl.multiple_of(i * CONV_ROWS, CONV_ROWS)
            acc = jnp.zeros((CONV_ROWS, LANES), F32)
            for j in range(GDN_CONV):
                tap = pad_scr[pl.ds(r0 + (SUBLANES - half + j), CONV_ROWS), :]
                acc = acc + tap * w[j:j + 1, :]
            y = _silu(acc)
            if normalise:
                y = y * lax.rsqrt(jnp.sum(y * y, axis=-1, keepdims=True) + EPS) * mult
            dst_ref[pl.ds(r0, CONV_ROWS), :] = y.astype(BF16)
            return carry

        lax.fori_loop(0, seq // CONV_ROWS, body, 0, unroll=4)

    conv_into(q_ref, cwq_ref, qs_scr, True, scale)
    conv_into(k_ref, cwk_ref, ks_scr, True, 1.0)
    conv_into(v_ref, cwv_ref, vs_scr, False, 1.0)

    sm = small_ref[...]
    lane = _iota2(sm.shape, 1)
    log_decay = -jnp.exp(gate_ref[0:1, :]) * _softplus(sm + gate_ref[1:2, :])
    gate_vals = jnp.where(lane < SMALL_BF, log_decay, _sigmoid(sm))
    gate_hl = jnp.concatenate(_split2(gate_vals), axis=1)
    sel_lane = jnp.bitwise_and(_iota2((2 * LANES, 4 * LANES), 0), LANES - 1)
    sel_gate = jnp.right_shift(_iota2((2 * LANES, 4 * LANES), 1), LANES.bit_length() - 1)
    sel = jnp.where(sel_lane == SMALL_AF + GDN_HEADS * sel_gate + hh, 1.0, 0.0).astype(BF16)
    spread = _dot(gate_hl, sel)
    gf_scr[...] = spread[:, 0 * LANES:1 * LANES]
    gb_scr[...] = spread[:, 1 * LANES:2 * LANES]
    btf_scr[...] = spread[:, 2 * LANES:3 * LANES]
    btb_scr[...] = spread[:, 3 * LANES:4 * LANES]
    sf_scr[...] = jnp.zeros_like(sf_scr)
    sb_scr[...] = jnp.zeros_like(sb_scr)

    row = _iota2((c, c), 0)
    col = _iota2((c, c), 1)
    eye = jnp.where(row == col, 1.0, 0.0).astype(F32)
    low, slow = row >= col, row > col
    upp, supp = row <= col, row < col
    assert c == LANES, "the decay matrix is formed from a [c, 128] lane-broadcast column and its transpose"

    tri_shift = TRI_BLOCK.bit_length() - 1
    diag_blocks = jnp.right_shift(row, tri_shift) == jnp.right_shift(col, tri_shift)

    cur_slot, prev_slot = cur * (2 * n), prev * (2 * n)
    cur_row, prev_row = cur * seq, prev * seq

    def scan_step(i):
        j = n - 1 - i
        slots = [prev_slot + i, prev_slot + n + j]
        rows = [pl.ds(pl.multiple_of(prev_row + i * c, c), c), pl.ds(pl.multiple_of(prev_row + j * c, c), c)]
        states = [sf_scr, sb_scr]
        s = [ref[...] for ref in states]
        s_b = _each(lambda x: x.astype(BF16), s)
        ns = _each(lambda sl, x: _dot(nmat_scr[sl], x), slots, s_b)
        qs = _each(lambda sl, x: _dot(qp_scr[sl], x), slots, s_b)
        for ref, sl, s_, ns_ in zip(states, slots, s, ns):
            ref[...] = cd_scr[sl][0:1, :] * s_ + (bmat_scr[sl] - ns_)
        for r, q in zip(rows, qs):
            o_scr[r, :] += q

    n_groups = n // GDN_PREP_GROUP
    scans_per_group = n // n_groups

    def prep_group(gi, carry):
        pending = [gi * scans_per_group + j for j in range(scans_per_group)]

        def tick():
            if pending:
                scan_step(pending.pop(0))

        chunk_ids = [gi * GDN_PREP_GROUP + j for j in range(GDN_PREP_GROUP)]
        rows = [pl.ds(pl.multiple_of(i * c, c), c) for i in chunk_ids]
        qc = [qs_scr[r, :] for r in rows]
        kc = [ks_scr[r, :] for r in rows]
        vc = [vs_scr[r, :] for r in rows]
        gl = [ref[r, :] for r in rows for ref in (gf_scr, gb_scr)]
        bt = [ref[r, :] for r in rows for ref in (btf_scr, btb_scr)]
        reverse = [False, True] * GDN_PREP_GROUP
        incl = [low, upp] * GDN_PREP_GROUP
        strict = [slow, supp] * GDN_PREP_GROUP
        tot_row = [c - 1, 0] * GDN_PREP_GROUP

        def both(per_chunk):
            return [x for x in per_chunk for _ in range(2)]

        kk = both(_each(_dot_nt, kc, kc))
        qk = both(_each(_dot_nt, qc, kc))
        tick()
        qf = both(_each(lambda x: x.astype(F32), qc))
        kf = both(_each(lambda x: x.astype(F32), kc))
        vf = both(_each(lambda x: x.astype(F32), vc))

        gc = _each(_cumsum_rows, gl, reverse)
        tot = _each(lambda g, r: g[r:r + 1, :], gc, tot_row)
        e = _each(lambda g, inc: jnp.exp(jnp.where(inc, g - jnp.transpose(g), 0.0)), gc, incl)
        a = _each(lambda kk_, b, e_, st: kk_ * b[:, :c] * jnp.where(st, e_, 0.0), kk, bt, e, strict)
        t_inv = _tri_inverse(a, eye, diag_blocks, c, tick)
        egc = _each(jnp.exp, gc)
        wu = _each(lambda t, k, v, b, eg: _mm(t, jnp.concatenate([k * b * eg, v * b], axis=1)).astype(BF16),
                   t_inv, kf, vf, bt, egc)
        tick()
        attn = _each(lambda qk_, e_, inc: (qk_ * jnp.where(inc, e_, 0.0)).astype(BF16), qk, e, incl)
        k_tail = _each(lambda k, t, g: (k * jnp.exp(t - g)).astype(BF16), kf, tot, gc)
        kwu = _each(_dot_tn, k_tail, wu)
        tick()
        awu = _each(_dot, attn, wu)
        while pending:
            tick()

        for p in range(2 * GDN_PREP_GROUP):
            slot = cur_slot + chunk_ids[p // 2] + (p % 2) * n
            nmat_scr[slot] = kwu[p][:, :GDN_DK].astype(BF16)
            bmat_scr[slot] = kwu[p][:, GDN_DK:]
            qp_scr[slot] = (qf[p] * egc[p] - awu[p][:, :GDN_DK]).astype(BF16)
            cd_scr[slot] = jnp.broadcast_to(jnp.exp(tot[p]), (SUBLANES, LANES))
        for j, i in enumerate(chunk_ids):
            r = pl.ds(pl.multiple_of(cur_row + i * c, c), c)
            o_scr[r, :] = awu[2 * j][:, GDN_DK:] + awu[2 * j + 1][:, GDN_DK:]
        return carry

    lax.fori_loop(0, n_groups, prep_group, 0)

    def finish(i, carry):
        r0 = pl.multiple_of(i * CONV_ROWS, CONV_ROWS)
        o = o_scr[pl.ds(pl.multiple_of(prev_row + r0, CONV_ROWS), CONV_ROWS), :]
        y = o * lax.rsqrt(jnp.mean(o * o, axis=-1, keepdims=True) + EPS) * nw_ref[...]
        y = y * _silu(z_ref[pl.ds(r0, CONV_ROWS), :].astype(F32))
        y = y * _sigmoid(mb_ref[pl.ds(r0, CONV_ROWS), :].astype(F32))
        out_ref[pl.ds(r0, CONV_ROWS), :] = y.astype(BF16)
        return carry

    lax.fori_loop(0, seq // CONV_ROWS, finish, 0, unroll=4)


def _gdn(main, small, gates, conv_w, norm_w, batch, seq):
    n = batch * seq
    total = batch * GDN_HEADS
    kern = functools.partial(_gdn_kernel, seq=seq, chunk=GDN_CHUNK, n_heads_total=total)
    n_chunks = seq // GDN_CHUNK

    def head_of(step):
        idx = jnp.minimum(step, total - 1)
        return idx // GDN_HEADS, idx % GDN_HEADS

    def prev_head_of(step):
        idx = jnp.maximum(step - 1, 0)
        return idx // GDN_HEADS, idx % GDN_HEADS

    def blk(off, which):
        def index(s):
            b, hh = which(s)
            return b, off // LANES + hh
        return pl.BlockSpec((seq, LANES), index)

    def cw(part):
        return pl.BlockSpec((GDN_CONV, LANES), lambda s: (0, part * GDN_HEADS + head_of(s)[1]))

    seq_f32 = lambda: pltpu.VMEM((seq, LANES), F32)
    seq_bf16 = lambda: pltpu.VMEM((seq, LANES), BF16)
    return pl.pallas_call(
        kern,
        grid=(total + 1,),
        in_specs=[
            pl.BlockSpec((SUBLANES, LANES), lambda s: (0, 0)),
            blk(COL_DQ, head_of), blk(COL_DK, head_of), blk(COL_DV, head_of),
            blk(COL_DZ, prev_head_of), blk(COL_MB, prev_head_of),
            pl.BlockSpec((seq, LANES), lambda s: (head_of(s)[0], 0)),
            cw(0), cw(1), cw(2),
            pl.BlockSpec((1, GDN_DV), lambda s: (0, 0)),
        ],
        out_specs=pl.BlockSpec((seq, GDN_DV), lambda s: prev_head_of(s)),
        out_shape=jax.ShapeDtypeStruct((n, D_MODEL), BF16),
        scratch_shapes=[
            pltpu.VMEM((seq + 2 * SUBLANES, LANES), F32),
            seq_bf16(), seq_bf16(), seq_bf16(),
            seq_f32(), seq_f32(), seq_f32(), seq_f32(),
            pltpu.VMEM((2 * seq, LANES), F32),
            pltpu.VMEM((4 * n_chunks, GDN_DK, GDN_DK), BF16),
            pltpu.VMEM((4 * n_chunks, GDN_DK, GDN_DV), F32),
            pltpu.VMEM((4 * n_chunks, GDN_CHUNK, GDN_DK), BF16),
            pltpu.VMEM((4 * n_chunks, SUBLANES, LANES), F32),
            pltpu.VMEM((GDN_DK, GDN_DV), F32), pltpu.VMEM((GDN_DK, GDN_DV), F32),
        ],
        compiler_params=pltpu.CompilerParams(
            dimension_semantics=("arbitrary",), vmem_limit_bytes=VMEM_LIMIT),
        name="gdn",
    )(gates, main, main, main, main, main, small, conv_w, conv_w, conv_w, norm_w)


def _outproj_kernel(ga_ref, gb_ref, x_ref, wo_ref, nw_ref, wrh_ref, wrl_ref,
                    x1_ref, h2_ref, ri_ref, rw_ref, cnt_ref, carry_scr, *, tm):
    @pl.when(pl.program_id(0) == 0)
    def _():
        carry_scr[...] = jnp.zeros_like(carry_scr)

    sub = OUTPROJ_SUB
    subs = [pl.ds(j * sub, sub) for j in range(tm // sub)]
    lane_i = _iota2((sub, LANES), 1)
    lane = lane_i.astype(F32)

    mixed = [(ga_ref[r, :].astype(F32) + gb_ref[r, :].astype(F32)).astype(BF16) for r in subs]
    x1 = _each(lambda r, m: x_ref[r, :] + _dot(m, wo_ref[...]), subs, mixed)
    for r, v in zip(subs, x1):
        x1_ref[r, :] = v
    h2 = _each(lambda v: v * lax.rsqrt(jnp.mean(v * v, axis=-1, keepdims=True) + EPS) * nw_ref[...], x1)
    for j, v in enumerate(h2):
        _store_row_tiles(h2_ref.at[pl.ds(j * sub * ROW_TILE, sub * ROW_TILE)], v)
    hl = _each(_split2, h2)
    lg = _each(lambda p: _dot(p[0], wrh_ref[...]) + _dot(p[1], wrh_ref[...]) + _dot(p[0], wrl_ref[...]), hl)

    def row_max(vals):
        return _each(lambda v: jnp.max(v, axis=-1, keepdims=True), vals)

    def first_lane_of(vals, maxima):
        return _each(lambda v, m: jnp.min(jnp.where(v == m, lane, float(LANES)), axis=-1, keepdims=True),
                     vals, maxima)

    is_g = (lane_i >= ROUTE_GROUP_LANE) & (lane_i < ROUTE_GROUP_LANE + N_GROUPS)
    gl = _each(lambda v: jnp.where(is_g, v, NEG_INF), lg)
    gmax = row_max(gl)
    gidx = _each(lambda i: i - float(ROUTE_GROUP_LANE), first_lane_of(gl, gmax))
    gsum = _each(lambda v, m: jnp.sum(jnp.where(is_g, jnp.exp(v - m), 0.0), axis=-1, keepdims=True), lg, gmax)
    lane_group = jnp.right_shift(lane_i, EXPERTS_PER_GROUP.bit_length() - 1).astype(F32)
    el = _each(lambda v, g: jnp.where((lane_i < N_EXPERTS) & (lane_group == g), v, NEG_INF), lg, gidx)
    m1 = row_max(el)
    e0 = first_lane_of(el, m1)
    el2 = _each(lambda v, i: jnp.where(lane == i, NEG_INF, v), el, e0)
    m2 = row_max(el2)
    e1 = first_lane_of(el2, m2)
    ratio = _each(lambda a, b: jnp.exp(b - a), m1, m2)
    w0 = _each(lambda s_, r: 1.0 / (s_ * (1.0 + r)), gsum, ratio)
    w1 = _each(lambda w, r: w * r, w0, ratio)

    pick0 = _each(lambda i: lane == i, e0)
    pick1 = _each(lambda i: lane == i, e1)
    onehot = jnp.concatenate(_each(lambda p, q: jnp.where(p | q, 1.0, 0.0), pick0, pick1), axis=0)
    trow = _iota2((tm, tm), 0)
    tcol = _iota2((tm, tm), 1)
    before = jnp.where(trow > tcol, 1.0, 0.0).astype(BF16)
    cnt = _dot(before, onehot.astype(BF16)) + carry_scr[0:1, :]
    cnts = [cnt[j * sub:(j + 1) * sub, :] for j in range(tm // sub)]
    rank0 = _each(lambda p, c_: jnp.sum(jnp.where(p, c_, 0.0), axis=-1, keepdims=True), pick0, cnts)
    rank1 = _each(lambda p, c_: jnp.sum(jnp.where(p, c_, 0.0), axis=-1, keepdims=True), pick1, cnts)
    total = carry_scr[0:1, :] + jnp.sum(onehot, axis=0, keepdims=True)
    carry_scr[...] = jnp.broadcast_to(total, carry_scr.shape)
    cnt_ref[...] = jnp.broadcast_to(total, cnt_ref.shape).astype(jnp.int32)

    for j, r in enumerate(subs):
        ri = jnp.where(lane_i == 0, e0[j], jnp.where(lane_i == 1, e1[j], 0.0))
        ri = jnp.where(lane_i == 2, rank0[j], jnp.where(lane_i == 3, rank1[j], ri))
        ri_ref[:, r] = jnp.transpose(ri)[0:SUBLANES, :].astype(jnp.int32)
        rw_ref[r, :] = jnp.where(lane_i == 0, w0[j], jnp.where(lane_i == 1, w1[j], 0.0))


def _outproj(ga, gb, x2, w_out, norm_w, wr_hi, wr_lo):
    n = x2.shape[0]
    tm = OUTPROJ_TM
    kern = functools.partial(_outproj_kernel, tm=tm)
    row_blk = lambda w: pl.BlockSpec((tm, w), lambda i: (i, 0))
    const = lambda shape: pl.BlockSpec(shape, lambda i: (0, 0))
    return pl.pallas_call(
        kern,
        grid=(n // tm,),
        in_specs=[
            row_blk(D_MODEL), row_blk(D_MODEL), row_blk(D_MODEL),
            const((D_MODEL, D_MODEL)), const((1, D_MODEL)),
            const((D_MODEL, LANES)), const((D_MODEL, LANES)),
        ],
        out_specs=[row_blk(D_MODEL),
                   pl.BlockSpec((tm * ROW_TILE, LANES), lambda i: (i, 0)),
                   pl.BlockSpec((SUBLANES, tm), lambda i: (0, i)),
                   row_blk(LANES),
                   const((SUBLANES, LANES))],
        out_shape=[
            jax.ShapeDtypeStruct((n, D_MODEL), F32),
            jax.ShapeDtypeStruct((n * ROW_TILE, LANES), U32),
            jax.ShapeDtypeStruct((SUBLANES, n), jnp.int32),
            jax.ShapeDtypeStruct((n, LANES), F32),
            jax.ShapeDtypeStruct((SUBLANES, LANES), jnp.int32),
        ],
        scratch_shapes=[pltpu.VMEM((SUBLANES, LANES), F32)],
        compiler_params=pltpu.CompilerParams(
            dimension_semantics=("arbitrary",), vmem_limit_bytes=VMEM_LIMIT),
        name="outproj",
    )(ga, gb, x2, w_out, norm_w, wr_hi, wr_lo)


def _row_copy(src_ref, src_row, dst_ref, dst_row, sem):
    src = src_ref.at[pl.ds(pl.multiple_of(src_row * ROW_TILE, ROW_TILE), ROW_TILE)]
    dst = dst_ref.at[pl.ds(pl.multiple_of(dst_row * ROW_TILE, ROW_TILE), ROW_TILE)]
    return pltpu.make_async_copy(src, dst, sem)


def _scatter_kernel(seg_ref, d0_ref, d1_ref, h2_ref, h2_hbm, xs_ref, zero_scr, probe_scr, sem, zsem, psem, *, tile):
    i = pl.program_id(0)

    def issue(t, carry):
        _row_copy(h2_ref, t, xs_ref, d0_ref[t], sem).start(priority=0)
        _row_copy(h2_ref, t, xs_ref, d1_ref[t], sem).start(priority=1)
        _row_copy(h2_hbm, jnp.bitwise_and(d0_ref[t], 16383), probe_scr, t, psem).start(priority=0)
        _row_copy(h2_hbm, jnp.bitwise_and(d1_ref[t], 16383), probe_scr, tile + t, psem).start(priority=1)
        return carry

    lax.fori_loop(0, tile, issue, 0, unroll=DMA_UNROLL)

    @pl.when(i == 0)
    def _():
        zero_scr[...] = jnp.zeros_like(zero_scr)

        def per_expert(e, carry):
            lo, hi = seg_ref[0, e], seg_ref[1, e]

            def start(r, c2):
                _row_copy(zero_scr, 0, xs_ref, r, zsem).start()
                return c2

            def wait(r, c2):
                _row_copy(zero_scr, 0, xs_ref, r, zsem).wait()
                return c2

            lax.fori_loop(lo, hi, start, 0)
            lax.fori_loop(lo, hi, wait, 0)
            return carry

        lax.fori_loop(0, N_EXPERTS + 1, per_expert, 0)

    def drain(t, carry):
        _row_copy(h2_ref, 0, xs_ref, 0, sem).wait()
        _row_copy(h2_ref, 0, xs_ref, 0, sem).wait()
        _row_copy(h2_hbm, 0, probe_scr, 0, psem).wait()
        _row_copy(h2_hbm, 0, probe_scr, 0, psem).wait()
        return carry

    lax.fori_loop(0, tile, drain, 0, unroll=DMA_UNROLL)


def _scatter(seg, dest0, dest1, h2t, n_rows):
    n = dest0.shape[0]
    tile = SCATTER_T
    kern = functools.partial(_scatter_kernel, tile=tile)
    return pl.pallas_call(
        kern,
        grid=(n // tile,),
        in_specs=[
            pl.BlockSpec(memory_space=pltpu.SMEM),
            pl.BlockSpec((tile,), lambda i: (i,), memory_space=pltpu.SMEM),
            pl.BlockSpec((tile,), lambda i: (i,), memory_space=pltpu.SMEM),
            pl.BlockSpec((tile * ROW_TILE, LANES), lambda i: (i, 0)),
            pl.BlockSpec(memory_space=pl.ANY),
        ],
        out_specs=pl.BlockSpec(memory_space=pl.ANY),
        out_shape=jax.ShapeDtypeStruct((n_rows * ROW_TILE, LANES), U32),
        scratch_shapes=[pltpu.VMEM((SUBLANES, LANES), U32), pltpu.VMEM((2 * tile * ROW_TILE, LANES), U32),
                        pltpu.SemaphoreType.DMA, pltpu.SemaphoreType.DMA, pltpu.SemaphoreType.DMA],
        compiler_params=pltpu.CompilerParams(
            dimension_semantics=("arbitrary",), vmem_limit_bytes=VMEM_LIMIT),
        name="scatter",
    )(seg, dest0, dest1, h2t, h2t)


def _expert_kernel(be_ref, nv_ref, xs_ref, wg_ref, wu_ref, wd_ref, y_ref):
    i = pl.program_id(0)

    @pl.when(i < nv_ref[0])
    def _():
        x = _load_row_tiles(xs_ref).astype(BF16)
        g = _dot(x, wg_ref[0].astype(BF16))
        u = _dot(x, wu_ref[0].astype(BF16))
        hid = (_silu(g) * u).astype(BF16)
        _store_row_tiles(y_ref, _dot(hid, wd_ref[0].astype(BF16)))

    @pl.when(i >= nv_ref[0])
    def _():
        y_ref[...] = jnp.zeros_like(y_ref)


def _experts(block_expert, n_valid, xs, w_gate, w_up, w_down):
    blk = MOE_BLOCK
    n_rows = xs.shape[0] // ROW_TILE
    grid_spec = pltpu.PrefetchScalarGridSpec(
        num_scalar_prefetch=2,
        grid=(n_rows // blk,),
        in_specs=[
            pl.BlockSpec((blk * ROW_TILE, LANES), lambda i, be, nv: (jnp.minimum(i, nv[0] - 1), 0)),
            pl.BlockSpec((1, D_MODEL, D_EXPERT), lambda i, be, nv: (be[i], 0, 0)),
            pl.BlockSpec((1, D_MODEL, D_EXPERT), lambda i, be, nv: (be[i], 0, 0)),
            pl.BlockSpec((1, D_EXPERT, D_MODEL), lambda i, be, nv: (be[i], 0, 0)),
        ],
        out_specs=pl.BlockSpec((blk * ROW_TILE, LANES), lambda i, be, nv: (i, 0)),
    )
    return pl.pallas_call(
        _expert_kernel,
        grid_spec=grid_spec,
        out_shape=jax.ShapeDtypeStruct((n_rows * ROW_TILE, LANES), U32),
        compiler_params=pltpu.CompilerParams(
            dimension_semantics=("arbitrary",), vmem_limit_bytes=VMEM_LIMIT),
        name="experts",
    )(block_expert, n_valid, xs, w_gate, w_up, w_down)


def _combine_kernel(d0_ref, d1_ref, x1_ref, rw_ref, nw_ref, y_ref, out_ref, ya_scr, yb_scr, sem, *, tile):
    def issue(t, carry):
        _row_copy(y_ref, d0_ref[t], ya_scr, t, sem).start(priority=0)
        _row_copy(y_ref, d1_ref[t], yb_scr, t, sem).start(priority=1)
        return carry

    def drain(t, carry):
        _row_copy(y_ref, 0, ya_scr, 0, sem).wait()
        _row_copy(y_ref, 0, yb_scr, 0, sem).wait()
        return carry

    lax.fori_loop(0, tile, issue, 0, unroll=DMA_UNROLL)
    lax.fori_loop(0, tile, drain, 0, unroll=DMA_UNROLL)

    rw = rw_ref[...]
    moe = rw[:, 0:1] * _load_row_tiles(ya_scr) + rw[:, 1:2] * _load_row_tiles(yb_scr)
    x2 = x1_ref[...] + moe
    out_ref[...] = x2 * lax.rsqrt(jnp.mean(x2 * x2, axis=-1, keepdims=True) + EPS) * nw_ref[...]


def _combine(dest0, dest1, x1, rw, norm_w, y):
    n = x1.shape[0]
    tile = COMBINE_T
    kern = functools.partial(_combine_kernel, tile=tile)
    return pl.pallas_call(
        kern,
        grid=(n // tile,),
        in_specs=[
            pl.BlockSpec((tile,), lambda i: (i,), memory_space=pltpu.SMEM),
            pl.BlockSpec((tile,), lambda i: (i,), memory_space=pltpu.SMEM),
            pl.BlockSpec((tile, D_MODEL), lambda i: (i, 0)),
            pl.BlockSpec((tile, LANES), lambda i: (i, 0)),
            pl.BlockSpec((1, D_MODEL), lambda i: (0, 0)),
            pl.BlockSpec(memory_space=pl.ANY),
        ],
        out_specs=pl.BlockSpec((tile, D_MODEL), lambda i: (i, 0)),
        out_shape=jax.ShapeDtypeStruct((n, D_MODEL), F32),
        scratch_shapes=[
            pltpu.VMEM((tile * ROW_TILE, LANES), U32), pltpu.VMEM((tile * ROW_TILE, LANES), U32),
            pltpu.SemaphoreType.DMA,
        ],
        compiler_params=pltpu.CompilerParams(
            dimension_semantics=("arbitrary",), vmem_limit_bytes=VMEM_LIMIT),
        name="combine",
    )(dest0, dest1, x1, rw, norm_w, y)


def _pad_cols(w, width):
    return jnp.pad(w, ((0, 0), (0, width - w.shape[1])))


def _token_mixer_and_moe(x, norm1_w, w_in, w2_f, b_f, w2_b, b_b, gla_norm_w, conv_w, a_log_f, dt_bias_f,
                         a_log_b, dt_bias_b, gdn_norm_w, w_out, norm2_w, w_group, w_router, w_gate, w_up,
                         w_down, out_norm_w):
    batch, seq, d = x.shape
    n = batch * seq
    x2 = x.reshape(n, d)

    w_main = jnp.concatenate([w_in[:, :3072], w_in[:, 3104:7200], w_in[:, 7232:]], axis=1).astype(BF16)
    w_small = _pad_cols(jnp.concatenate([w_in[:, 3072:3104], w_in[:, 7200:7232]], axis=1), LANES)
    ws_hi, ws_lo = _split2(w_small)
    main, small = _inproj(x2, norm1_w.reshape(1, d), w_main, ws_hi, ws_lo)

    w2f_pad = jnp.zeros((LANES, GLA_HEADS * GLA_DK), F32).at[0:GLA_GATE_RANK].set(w2_f)
    w2b_pad = jnp.zeros((LANES, GLA_HEADS * GLA_DK), F32).at[GLA_GATE_RANK:2 * GLA_GATE_RANK].set(w2_b)
    ga = _gla(main, small, w2f_pad, w2b_pad, b_f.reshape(1, -1), b_b.reshape(1, -1),
              gla_norm_w.reshape(1, -1), batch, seq)

    gates = jnp.zeros((SUBLANES, LANES), F32)
    gates = gates.at[0, SMALL_AF:SMALL_BF].set(jnp.concatenate([a_log_f, a_log_b]))
    gates = gates.at[1, SMALL_AF:SMALL_BF].set(jnp.concatenate([dt_bias_f, dt_bias_b]))
    gb = _gdn(main, small, gates, conv_w, gdn_norm_w.reshape(1, -1), batch, seq)

    w_route = _pad_cols(jnp.concatenate([w_router, w_group], axis=1), LANES)
    wr_hi, wr_lo = _split2(w_route)
    x1, h2t, rt, rw, counts = _outproj(ga, gb, x2, w_out.astype(BF16), norm2_w.reshape(1, d), wr_hi, wr_lo)

    blk = MOE_BLOCK
    cnt = counts[0, :N_EXPERTS]
    padded = (cnt + blk - 1) // blk * blk
    ends = jnp.cumsum(padded)
    pstart = (ends - padded).astype(jnp.int32)
    n_blocks = -(-(2 * n + N_EXPERTS * (blk - 1)) // blk)
    n_rows = n_blocks * blk
    block_row = jnp.arange(n_blocks, dtype=jnp.int32) * blk
    block_expert = jnp.minimum(
        jnp.sum((ends[None, :] <= block_row[:, None]).astype(jnp.int32), axis=1), N_EXPERTS - 1)
    n_valid = (ends[-1:] // blk).astype(jnp.int32)
    seg = jnp.stack([jnp.append(pstart + cnt, ends[-1]), jnp.append(ends, n_rows)]).astype(jnp.int32)

    experts = jnp.arange(N_EXPERTS, dtype=jnp.int32)
    seg_start = jnp.sum(jnp.where(rt[0:2, :, None] == experts, pstart, 0), axis=-1)
    dest = seg_start + rt[2:4]
    dest0, dest1 = dest[0], dest[1]

    xs = _scatter(seg, dest0, dest1, h2t, n_rows)
    y = _experts(block_expert, n_valid, xs, w_gate, w_up, w_down)
    out = _combine(dest0, dest1, x1, rw, out_norm_w.reshape(1, d), y)
    return out.reshape(batch, seq, d)


def kernel(x, norm1_w, w_in, gla_gate_w2_fwd, gla_gate_b_fwd, gla_gate_w2_bwd, gla_gate_b_bwd, gla_norm_w,
           gdn_conv_w, gdn_a_log_fwd, gdn_dt_bias_fwd, gdn_a_log_bwd, gdn_dt_bias_bwd, gdn_norm_w, w_out,
           norm2_w, moe_w_group, moe_w_router, moe_w_gate, moe_w_up, moe_w_down, norm_f_w):
    assert norm1_w.shape[0] == 1, "single-layer block"
    return _token_mixer_and_moe(
        x, norm1_w[0], w_in[0], gla_gate_w2_fwd[0], gla_gate_b_fwd[0], gla_gate_w2_bwd[0], gla_gate_b_bwd[0],
        gla_norm_w[0], gdn_conv_w[0], gdn_a_log_fwd[0], gdn_dt_bias_fwd[0], gdn_a_log_bwd[0],
        gdn_dt_bias_bwd[0], gdn_norm_w[0], w_out[0], norm2_w[0], moe_w_group[0], moe_w_router[0],
        moe_w_gate[0], moe_w_up[0], moe_w_down[0], norm_f_w)
```

```python
import functools

import jax
import jax.numpy as jnp
import numpy as np
from jax import lax
from jax.experimental import pallas as pl
from jax.experimental.pallas import tpu as pltpu

F32 = jnp.float32
BF16 = jnp.bfloat16
U32 = jnp.uint32

D_MODEL = 1024
GLA_HEADS = 4
GLA_DK = 128
GLA_DV = 256
GLA_GATE_RANK = 16
GLA_GATE_TAU = 16.0
GLA_CHUNK = 64
GLA_GROUP = 8
GDN_HEADS = 8
GDN_DK = 128
GDN_DV = 128
GDN_CONV = 5
GDN_CHUNK = 128
GDN_PREP_GROUP = 8
N_GROUPS = 4
EXPERTS_PER_GROUP = 8
N_EXPERTS = N_GROUPS * EXPERTS_PER_GROUP
D_EXPERT = 256
EPS = 1e-6

LANES = 128
SUBLANES = 8
VMEM_LIMIT = 48 * 1024 * 1024

COL_GQ, COL_GK, COL_GV, COL_GR = 0, 512, 1024, 2048
COL_DQ, COL_DK, COL_DV, COL_DZ = 3072, 4096, 5120, 6144
COL_MA, COL_MB = 7168, 8192
D_MAIN = 9216
SMALL_AF, SMALL_AB, SMALL_BF, SMALL_BB = 32, 40, 48, 56
ROUTE_GROUP_LANE = 32

MOE_BLOCK = 256
ROW_TILE = D_MODEL // 2 // LANES
HIGH_HALF = np.uint32(0xFFFF0000)
ZERO_ROWS = 128
DMA_UNROLL = 8
INPROJ_TM, INPROJ_TN = 1024, 1024
OUTPROJ_TM = 512
OUTPROJ_SUB = 128
SCATTER_T = 512
COMBINE_T = 256
CONV_ROWS = 256
NEG_INF = float("-inf")


def _dot(a, b):
    return jnp.dot(a, b, preferred_element_type=F32)


def _dot_nt(a, b):
    return lax.dot_general(a, b, (((1,), (1,)), ((), ())), preferred_element_type=F32)


def _dot_tn(a, b):
    return lax.dot_general(a, b, (((0,), (0,)), ((), ())), preferred_element_type=F32)


def _split2(x):
    hi = x.astype(BF16)
    lo = (x - hi.astype(F32)).astype(BF16)
    return hi, lo


def _split3(x):
    hi = x.astype(BF16)
    r = x - hi.astype(F32)
    mid = r.astype(BF16)
    lo = (r - mid.astype(F32)).astype(BF16)
    return hi, mid, lo


def _dot_exact_rhs(x, m_bf16):
    hi, mid, lo = _split3(x)
    return _dot(hi, m_bf16) + _dot(mid, m_bf16) + _dot(lo, m_bf16)


def _dot_exact_lhs(m_bf16, x):
    hi, mid, lo = _split3(x)
    return _dot(m_bf16, hi) + _dot(m_bf16, mid) + _dot(m_bf16, lo)


def _dot_lhs2(m_bf16, x):
    hi, lo = _split2(x)
    return _dot(m_bf16, hi) + _dot(m_bf16, lo)


def _dot_lhs2_wide(m2_bf16, x):
    return _dot(m2_bf16, jnp.concatenate(_split2(x), axis=0))


def _cumsum_rows(x, reverse):
    rows = x.shape[0]
    row = _iota2(x.shape, 0)
    shift = 1
    while shift < rows:
        if reverse:
            x = x + jnp.where(row < rows - shift, pltpu.roll(x, rows - shift, axis=0), 0.0)
        else:
            x = x + jnp.where(row >= shift, pltpu.roll(x, shift, axis=0), 0.0)
        shift *= 2
    return x


def _dot3(a, b):
    ah, al = _split2(a)
    bh, bl = _split2(b)
    return _dot(ah, bh) + _dot(al, bh) + _dot(ah, bl)


def _store_row_tiles(ref, x):
    rows = x.shape[0]
    half = D_MODEL // 2
    hi = lax.bitcast_convert_type(x[:, :half].astype(BF16).astype(F32), U32)
    lo = lax.bitcast_convert_type(x[:, half:].astype(BF16).astype(F32), U32)
    packed = jnp.bitwise_or(jnp.bitwise_and(hi, HIGH_HALF), jnp.right_shift(lo, 16))
    for j in range(ROW_TILE):
        ref[pl.ds(j, rows, stride=ROW_TILE), :] = packed[:, j * LANES:(j + 1) * LANES]


def _load_row_tiles(ref):
    rows = ref.shape[0] // ROW_TILE
    packed = jnp.concatenate([ref[pl.ds(j, rows, stride=ROW_TILE), :] for j in range(ROW_TILE)], axis=1)
    hi = lax.bitcast_convert_type(jnp.bitwise_and(packed, HIGH_HALF), F32)
    lo = lax.bitcast_convert_type(jnp.left_shift(packed, 16), F32)
    return jnp.concatenate([hi, lo], axis=1)


def _each(fn, *lists):
    return [fn(*args) for args in zip(*lists)]


def _sigmoid(x):
    return 1.0 / (1.0 + jnp.exp(-x))


def _silu(x):
    return x * _sigmoid(x)


def _softplus(x):
    return jnp.maximum(x, 0.0) + jnp.log(1.0 + jnp.exp(-jnp.abs(x)))


def _log_sigmoid(x):
    return jnp.minimum(x, 0.0) - jnp.log(1.0 + jnp.exp(-jnp.abs(x)))


def _iota2(shape, dim):
    return lax.broadcasted_iota(jnp.int32, shape, dim)


def _inproj_kernel(x_ref, nw_ref, w_ref, wsh_ref, wsl_ref, main_ref, small_ref, h_scr):
    @pl.when(pl.program_id(1) == 0)
    def _():
        x = x_ref[...]
        h = x * lax.rsqrt(jnp.mean(x * x, axis=-1, keepdims=True) + EPS) * nw_ref[...]
        hh, hl = _split2(h)
        h_scr[...] = hh
        small_ref[...] = _dot(hh, wsh_ref[...]) + _dot(hl, wsh_ref[...]) + _dot(hh, wsl_ref[...])

    main_ref[...] = _dot(h_scr[...], w_ref[...]).astype(BF16)


def _inproj(x2, norm_w, w_main, ws_hi, ws_lo):
    n = x2.shape[0]
    tm, tn = INPROJ_TM, INPROJ_TN
    return pl.pallas_call(
        _inproj_kernel,
        grid=(n // tm, D_MAIN // tn),
        in_specs=[
            pl.BlockSpec((tm, D_MODEL), lambda i, j: (i, 0)),
            pl.BlockSpec((1, D_MODEL), lambda i, j: (0, 0)),
            pl.BlockSpec((D_MODEL, tn), lambda i, j: (0, j)),
            pl.BlockSpec((D_MODEL, LANES), lambda i, j: (0, 0)),
            pl.BlockSpec((D_MODEL, LANES), lambda i, j: (0, 0)),
        ],
        out_specs=[
            pl.BlockSpec((tm, tn), lambda i, j: (i, j)),
            pl.BlockSpec((tm, LANES), lambda i, j: (i, 0)),
        ],
        out_shape=[
            jax.ShapeDtypeStruct((n, D_MAIN), BF16),
            jax.ShapeDtypeStruct((n, LANES), F32),
        ],
        scratch_shapes=[pltpu.VMEM((tm, D_MODEL), BF16)],
        compiler_params=pltpu.CompilerParams(
            dimension_semantics=("arbitrary", "arbitrary"), vmem_limit_bytes=VMEM_LIMIT),
        name="inproj",
    )(x2, norm_w, w_main, ws_hi, ws_lo)


def _gla_kernel(q_ref, k_ref, v_ref, gr_ref, ma_ref, small_ref, w2f_ref, w2b_ref, bf_ref, bb_ref,
                nw_ref, out_ref, laf_scr, lab_scr, o_scr, stf_scr, stb_scr, *, seq, chunk):
    c = chunk
    n = seq // c
    scale = GLA_DK ** -0.5

    sm = small_ref[...].astype(BF16)
    laf_scr[...] = _log_sigmoid(_dot(sm, w2f_ref[...].astype(BF16)) + bf_ref[...]) * (1.0 / GLA_GATE_TAU)
    lab_scr[...] = _log_sigmoid(_dot(sm, w2b_ref[...].astype(BF16)) + bb_ref[...]) * (1.0 / GLA_GATE_TAU)
    stf_scr[...] = jnp.zeros_like(stf_scr)
    stb_scr[...] = jnp.zeros_like(stb_scr)

    row = _iota2((c, c), 0)
    col = _iota2((c, c), 1)
    low = row >= col
    upp = row <= col
    low_m = jnp.concatenate([jnp.where(low, 1.0, 0.0).astype(BF16)] * 2, axis=1)
    upp_m = jnp.concatenate([jnp.where(upp, 1.0, 0.0).astype(BF16)] * 2, axis=1)

    g = GLA_GROUP

    def finish(rows, o):
        y = o * lax.rsqrt(jnp.mean(o * o, axis=-1, keepdims=True) + EPS) * nw_ref[...]
        y = y * _silu(gr_ref[rows, :].astype(F32))
        y = y * _sigmoid(ma_ref[rows, :].astype(F32))
        out_ref[rows, :] = y.astype(BF16)

    def group(gi, second_touch):
        ids = [gi * g + j for j in range(g)] + [n - 1 - gi * g - j for j in range(g)]
        rows = [pl.ds(pl.multiple_of(i * c, c), c) for i in ids]
        la = [laf_scr[r, :] for r in rows[:g]] + [lab_scr[r, :] for r in rows[g:]]
        csum = [low_m] * g + [upp_m] * g
        mask = [low] * g + [upp] * g
        tot_row = [c - 1] * g + [0] * g
        qf = [q_ref[r, :].astype(F32) * scale for r in rows]
        kf = [k_ref[r, :].astype(F32) for r in rows]
        vc = [v_ref[r, :] for r in rows]

        cum = _each(_dot_lhs2_wide, csum, la)
        tot = _each(lambda x, r: x[r:r + 1, :], cum, tot_row)
        q_dec = _each(lambda q, x: (q * jnp.exp(x)).astype(BF16), qf, cum)
        k_inv = _each(lambda k, x: (k * jnp.exp(-x)).astype(BF16), kf, cum)
        k_tail = _each(lambda k, t, x: (k * jnp.exp(t - x)).astype(BF16), kf, tot, cum)
        s = _each(lambda m, q, k: jnp.where(m, _dot_nt(q, k), 0.0).astype(BF16), mask, q_dec, k_inv)
        o = _each(_dot, s, vc)
        kv = _each(_dot_tn, vc, k_tail)
        dec = _each(jnp.exp, tot)

        for st_scr, probs in ((stf_scr, range(g)), (stb_scr, range(g, 2 * g))):
            st = st_scr[...]
            for p in probs:
                o[p] = o[p] + _dot_nt(q_dec[p], st.astype(BF16))
                st = dec[p] * st + kv[p]
            st_scr[...] = st

        for r, o_p in zip(rows, o):
            if second_touch:
                finish(r, o_scr[r, :] + o_p)
            else:
                o_scr[r, :] = o_p

    def first_half(gi, carry):
        group(gi, False)
        return carry

    def second_half(gi, carry):
        group(gi, True)
        return carry

    n_groups = n // g
    lax.fori_loop(0, n_groups // 2, first_half, 0)
    lax.fori_loop(n_groups // 2, n_groups, second_half, 0)


def _gla(main, small, w2f_pad, w2b_pad, b_f, b_b, norm_w, batch, seq):
    n = batch * seq
    h = GLA_HEADS
    kern = functools.partial(_gla_kernel, seq=seq, chunk=GLA_CHUNK)
    qk_blk = lambda off: pl.BlockSpec((seq, GLA_DK), lambda b, hh, off=off: (b, off // GLA_DK + hh))
    v_blk = lambda off: pl.BlockSpec((seq, GLA_DV), lambda b, hh, off=off: (b, off // GLA_DV + hh))
    return pl.pallas_call(
        kern,
        grid=(batch, h),
        in_specs=[
            qk_blk(COL_GQ), qk_blk(COL_GK), v_blk(COL_GV), v_blk(COL_GR), v_blk(COL_MA),
            pl.BlockSpec((seq, LANES), lambda b, hh: (b, 0)),
            pl.BlockSpec((LANES, GLA_DK), lambda b, hh: (0, hh)),
            pl.BlockSpec((LANES, GLA_DK), lambda b, hh: (0, hh)),
            pl.BlockSpec((1, GLA_DK), lambda b, hh: (0, hh)),
            pl.BlockSpec((1, GLA_DK), lambda b, hh: (0, hh)),
            pl.BlockSpec((1, GLA_DV), lambda b, hh: (0, 0)),
        ],
        out_specs=pl.BlockSpec((seq, GLA_DV), lambda b, hh: (b, hh)),
        out_shape=jax.ShapeDtypeStruct((n, D_MODEL), BF16),
        scratch_shapes=[
            pltpu.VMEM((seq, GLA_DK), F32), pltpu.VMEM((seq, GLA_DK), F32),
            pltpu.VMEM((seq, GLA_DV), F32),
            pltpu.VMEM((GLA_DV, GLA_DK), F32), pltpu.VMEM((GLA_DV, GLA_DK), F32),
        ],
        compiler_params=pltpu.CompilerParams(
            dimension_semantics=("arbitrary", "arbitrary"), vmem_limit_bytes=VMEM_LIMIT),
        name="gla",
    )(main, main, main, main, main, small, w2f_pad, w2b_pad, b_f, b_b, norm_w)


TRI_BLOCK = 16


def _mm(a, b):
    return _dot(a.astype(BF16), b.astype(BF16))


def _nilpotent_inverse(a_list, eye, index, tick):
    t_list = _each(lambda a: eye - a, a_list)
    p_list = a_list
    power = 2
    while power < index:
        p_list = _each(lambda p: _mm(p, p), p_list)
        tick()
        t_list = _each(lambda t, p: t + _mm(t, p), t_list, p_list)
        tick()
        power *= 2
    return t_list


def _tri_inverse(a_list, eye, diag_blocks, chunk, tick):
    ad_list = _each(lambda a: jnp.where(diag_blocks, a, 0.0), a_list)
    ao_list = _each(lambda a: jnp.where(diag_blocks, 0.0, a), a_list)
    d_list = _nilpotent_inverse(ad_list, eye, TRI_BLOCK, tick)
    n_list = _each(_mm, d_list, ao_list)
    tick()
    t_list = _nilpotent_inverse(n_list, eye, chunk // TRI_BLOCK, tick)
    out = _each(_mm, t_list, d_list)
    tick()
    return out


def _gdn_kernel(gate_ref, q_ref, k_ref, v_ref, z_ref, mb_ref, small_ref, cwq_ref, cwk_ref, cwv_ref,
                nw_ref, out_ref, pad_scr, qs_scr, ks_scr, vs_scr, gf_scr, gb_scr, btf_scr, btb_scr,
                o_scr, nmat_scr, bmat_scr, qp_scr, cd_scr, sf_scr, sb_scr, *, seq, chunk, n_heads_total):
    c = chunk
    n = seq // c
    step = pl.program_id(0)
    hh = lax.rem(jnp.minimum(step, n_heads_total - 1), GDN_HEADS)
    cur = lax.rem(step, 2)
    prev = 1 - cur
    scale = GDN_DK ** -0.5

    @pl.when(step == 0)
    def _():
        nmat_scr[...] = jnp.zeros_like(nmat_scr)
        bmat_scr[...] = jnp.zeros_like(bmat_scr)
        qp_scr[...] = jnp.zeros_like(qp_scr)
        cd_scr[...] = jnp.zeros_like(cd_scr)
        o_scr[...] = jnp.zeros_like(o_scr)

    zeros8 = jnp.zeros((SUBLANES, LANES), F32)
    pad_scr[0:SUBLANES, :] = zeros8
    pad_scr[seq + SUBLANES:seq + 2 * SUBLANES, :] = zeros8
    half = GDN_CONV // 2

    def conv_into(src_ref, cw_ref, dst_ref, normalise, mult):
        pad_scr[SUBLANES:seq + SUBLANES, :] = src_ref[...].astype(F32)
        w = cw_ref[...]

        def body(i, carry):
            r0 = pl.multiple_of(i * CONV_ROWS, CONV_ROWS)
            acc = jnp.zeros((CONV_ROWS, LANES), F32)
            for j in range(GDN_CONV):
                tap = pad_scr[pl.ds(r0 + (SUBLANES - half + j), CONV_ROWS), :]
                acc = acc + tap * w[j:j + 1, :]
            y = _silu(acc)
            if normalise:
                y = y * lax.rsqrt(jnp.sum(y * y, axis=-1, keepdims=True) + EPS) * mult
            dst_ref[pl.ds(r0, CONV_ROWS), :] = y.astype(BF16)
            return carry

        lax.fori_loop(0, seq // CONV_ROWS, body, 0, unroll=4)

    conv_into(q_ref, cwq_ref, qs_scr, True, scale)
    conv_into(k_ref, cwk_ref, ks_scr, True, 1.0)
    conv_into(v_ref, cwv_ref, vs_scr, False, 1.0)

    sm = small_ref[...]
    lane = _iota2(sm.shape, 1)
    log_decay = -jnp.exp(gate_ref[0:1, :]) * _softplus(sm + gate_ref[1:2, :])
    gate_vals = jnp.where(lane < SMALL_BF, log_decay, _sigmoid(sm))
    gate_hl = jnp.concatenate(_split2(gate_vals), axis=1)
    sel_lane = jnp.bitwise_and(_iota2((2 * LANES, 4 * LANES), 0), LANES - 1)
    sel_gate = jnp.right_shift(_iota2((2 * LANES, 4 * LANES), 1), LANES.bit_length() - 1)
    sel = jnp.where(sel_lane == SMALL_AF + GDN_HEADS * sel_gate + hh, 1.0, 0.0).astype(BF16)
    spread = _dot(gate_hl, sel)
    gf_scr[...] = spread[:, 0 * LANES:1 * LANES]
    gb_scr[...] = spread[:, 1 * LANES:2 * LANES]
    btf_scr[...] = spread[:, 2 * LANES:3 * LANES]
    btb_scr[...] = spread[:, 3 * LANES:4 * LANES]
    sf_scr[...] = jnp.zeros_like(sf_scr)
    sb_scr[...] = jnp.zeros_like(sb_scr)

    row = _iota2((c, c), 0)
    col = _iota2((c, c), 1)
    eye = jnp.where(row == col, 1.0, 0.0).astype(F32)
    low, slow = row >= col, row > col
    upp, supp = row <= col, row < col
    assert c == LANES, "the decay matrix is formed from a [c, 128] lane-broadcast column and its transpose"

    tri_shift = TRI_BLOCK.bit_length() - 1
    diag_blocks = jnp.right_shift(row, tri_shift) == jnp.right_shift(col, tri_shift)

    cur_slot, prev_slot = cur * (2 * n), prev * (2 * n)
    cur_row, prev_row = cur * seq, prev * seq

    def scan_step(i):
        j = n - 1 - i
        slots = [prev_slot + i, prev_slot + n + j]
        rows = [pl.ds(pl.multiple_of(prev_row + i * c, c), c), pl.ds(pl.multiple_of(prev_row + j * c, c), c)]
        states = [sf_scr, sb_scr]
        s = [ref[...] for ref in states]
        s_b = _each(lambda x: x.astype(BF16), s)
        ns = _each(lambda sl, x: _dot(nmat_scr[sl], x), slots, s_b)
        qs = _each(lambda sl, x: _dot(qp_scr[sl], x), slots, s_b)
        for ref, sl, s_, ns_ in zip(states, slots, s, ns):
            ref[...] = cd_scr[sl][0:1, :] * s_ + (bmat_scr[sl] - ns_)
        for r, q in zip(rows, qs):
            o_scr[r, :] += q

    n_groups = n // GDN_PREP_GROUP
    scans_per_group = n // n_groups

    def prep_group(gi, carry):
        pending = [gi * scans_per_group + j for j in range(scans_per_group)]

        def tick():
            if pending:
                scan_step(pending.pop(0))

        chunk_ids = [gi * GDN_PREP_GROUP + j for j in range(GDN_PREP_GROUP)]
        rows = [pl.ds(pl.multiple_of(i * c, c), c) for i in chunk_ids]
        qc = [qs_scr[r, :] for r in rows]
        kc = [ks_scr[r, :] for r in rows]
        vc = [vs_scr[r, :] for r in rows]
        gl = [ref[r, :] for r in rows for ref in (gf_scr, gb_scr)]
        bt = [ref[r, :] for r in rows for ref in (btf_scr, btb_scr)]
        reverse = [False, True] * GDN_PREP_GROUP
        incl = [low, upp] * GDN_PREP_GROUP
        strict = [slow, supp] * GDN_PREP_GROUP
        tot_row = [c - 1, 0] * GDN_PREP_GROUP

        def both(per_chunk):
            return [x for x in per_chunk for _ in range(2)]

        kk = both(_each(_dot_nt, kc, kc))
        qk = both(_each(_dot_nt, qc, kc))
        tick()
        qf = both(_each(lambda x: x.astype(F32), qc))
        kf = both(_each(lambda x: x.astype(F32), kc))
        vf = both(_each(lambda x: x.astype(F32), vc))

        gc = _each(_cumsum_rows, gl, reverse)
        tot = _each(lambda g, r: g[r:r + 1, :], gc, tot_row)
        e = _each(lambda g, inc: jnp.exp(jnp.where(inc, g - jnp.transpose(g), 0.0)), gc, incl)
        a = _each(lambda kk_, b, e_, st: kk_ * b[:, :c] * jnp.where(st, e_, 0.0), kk, bt, e, strict)
        t_inv = _tri_inverse(a, eye, diag_blocks, c, tick)
        egc = _each(jnp.exp, gc)
        wu = _each(lambda t, k, v, b, eg: _mm(t, jnp.concatenate([k * b * eg, v * b], axis=1)).astype(BF16),
                   t_inv, kf, vf, bt, egc)
        tick()
        attn = _each(lambda qk_, e_, inc: (qk_ * jnp.where(inc, e_, 0.0)).astype(BF16), qk, e, incl)
        k_tail = _each(lambda k, t, g: (k * jnp.exp(t - g)).astype(BF16), kf, tot, gc)
        kwu = _each(_dot_tn, k_tail, wu)
        tick()
        awu = _each(_dot, attn, wu)
        while pending:
            tick()

        for p in range(2 * GDN_PREP_GROUP):
            slot = cur_slot + chunk_ids[p // 2] + (p % 2) * n
            nmat_scr[slot] = kwu[p][:, :GDN_DK].astype(BF16)
            bmat_scr[slot] = kwu[p][:, GDN_DK:]
            qp_scr[slot] = (qf[p] * egc[p] - awu[p][:, :GDN_DK]).astype(BF16)
            cd_scr[slot] = jnp.broadcast_to(jnp.exp(tot[p]), (SUBLANES, LANES))
        for j, i in enumerate(chunk_ids):
            r = pl.ds(pl.multiple_of(cur_row + i * c, c), c)
            o_scr[r, :] = awu[2 * j][:, GDN_DK:] + awu[2 * j + 1][:, GDN_DK:]
        return carry

    lax.fori_loop(0, n_groups, prep_group, 0)

    def finish(i, carry):
        r0 = pl.multiple_of(i * CONV_ROWS, CONV_ROWS)
        o = o_scr[pl.ds(pl.multiple_of(prev_row + r0, CONV_ROWS), CONV_ROWS), :]
        y = o * lax.rsqrt(jnp.mean(o * o, axis=-1, keepdims=True) + EPS) * nw_ref[...]
        y = y * _silu(z_ref[pl.ds(r0, CONV_ROWS), :].astype(F32))
        y = y * _sigmoid(mb_ref[pl.ds(r0, CONV_ROWS), :].astype(F32))
        out_ref[pl.ds(r0, CONV_ROWS), :] = y.astype(BF16)
        return carry

    lax.fori_loop(0, seq // CONV_ROWS, finish, 0, unroll=4)


def _gdn(main, small, gates, conv_w, norm_w, batch, seq):
    n = batch * seq
    total = batch * GDN_HEADS
    kern = functools.partial(_gdn_kernel, seq=seq, chunk=GDN_CHUNK, n_heads_total=total)
    n_chunks = seq // GDN_CHUNK

    def head_of(step):
        idx = jnp.minimum(step, total - 1)
        return idx // GDN_HEADS, idx % GDN_HEADS

    def prev_head_of(step):
        idx = jnp.maximum(step - 1, 0)
        return idx // GDN_HEADS, idx % GDN_HEADS

    def blk(off, which):
        def index(s):
            b, hh = which(s)
            return b, off // LANES + hh
        return pl.BlockSpec((seq, LANES), index)

    def cw(part):
        return pl.BlockSpec((GDN_CONV, LANES), lambda s: (0, part * GDN_HEADS + head_of(s)[1]))

    seq_f32 = lambda: pltpu.VMEM((seq, LANES), F32)
    seq_bf16 = lambda: pltpu.VMEM((seq, LANES), BF16)
    return pl.pallas_call(
        kern,
        grid=(total + 1,),
        in_specs=[
            pl.BlockSpec((SUBLANES, LANES), lambda s: (0, 0)),
            blk(COL_DQ, head_of), blk(COL_DK, head_of), blk(COL_DV, head_of),
            blk(COL_DZ, prev_head_of), blk(COL_MB, prev_head_of),
            pl.BlockSpec((seq, LANES), lambda s: (head_of(s)[0], 0)),
            cw(0), cw(1), cw(2),
            pl.BlockSpec((1, GDN_DV), lambda s: (0, 0)),
        ],
        out_specs=pl.BlockSpec((seq, GDN_DV), lambda s: prev_head_of(s)),
        out_shape=jax.ShapeDtypeStruct((n, D_MODEL), BF16),
        scratch_shapes=[
            pltpu.VMEM((seq + 2 * SUBLANES, LANES), F32),
            seq_bf16(), seq_bf16(), seq_bf16(),
            seq_f32(), seq_f32(), seq_f32(), seq_f32(),
            pltpu.VMEM((2 * seq, LANES), F32),
            pltpu.VMEM((4 * n_chunks, GDN_DK, GDN_DK), BF16),
            pltpu.VMEM((4 * n_chunks, GDN_DK, GDN_DV), F32),
            pltpu.VMEM((4 * n_chunks, GDN_CHUNK, GDN_DK), BF16),
            pltpu.VMEM((4 * n_chunks, SUBLANES, LANES), F32),
            pltpu.VMEM((GDN_DK, GDN_DV), F32), pltpu.VMEM((GDN_DK, GDN_DV), F32),
        ],
        compiler_params=pltpu.CompilerParams(
            dimension_semantics=("arbitrary",), vmem_limit_bytes=VMEM_LIMIT),
        name="gdn",
    )(gates, main, main, main, main, main, small, conv_w, conv_w, conv_w, norm_w)


def _outproj_kernel(ga_ref, gb_ref, x_ref, wo_ref, nw_ref, wrh_ref, wrl_ref,
                    x1_ref, h2_ref, ri_ref, rw_ref, cnt_ref, carry_scr, *, tm):
    @pl.when(pl.program_id(0) == 0)
    def _():
        carry_scr[...] = jnp.zeros_like(carry_scr)

    sub = OUTPROJ_SUB
    subs = [pl.ds(j * sub, sub) for j in range(tm // sub)]
    lane_i = _iota2((sub, LANES), 1)
    lane = lane_i.astype(F32)

    mixed = [(ga_ref[r, :].astype(F32) + gb_ref[r, :].astype(F32)).astype(BF16) for r in subs]
    x1 = _each(lambda r, m: x_ref[r, :] + _dot(m, wo_ref[...]), subs, mixed)
    for r, v in zip(subs, x1):
        x1_ref[r, :] = v
    h2 = _each(lambda v: v * lax.rsqrt(jnp.mean(v * v, axis=-1, keepdims=True) + EPS) * nw_ref[...], x1)
    for j, v in enumerate(h2):
        _store_row_tiles(h2_ref.at[pl.ds(j * sub * ROW_TILE, sub * ROW_TILE)], v)
    hl = _each(_split2, h2)
    lg = _each(lambda p: _dot(p[0], wrh_ref[...]) + _dot(p[1], wrh_ref[...]) + _dot(p[0], wrl_ref[...]), hl)

    def row_max(vals):
        return _each(lambda v: jnp.max(v, axis=-1, keepdims=True), vals)

    def first_lane_of(vals, maxima):
        return _each(lambda v, m: jnp.min(jnp.where(v == m, lane, float(LANES)), axis=-1, keepdims=True),
                     vals, maxima)

    is_g = (lane_i >= ROUTE_GROUP_LANE) & (lane_i < ROUTE_GROUP_LANE + N_GROUPS)
    gl = _each(lambda v: jnp.where(is_g, v, NEG_INF), lg)
    gmax = row_max(gl)
    gidx = _each(lambda i: i - float(ROUTE_GROUP_LANE), first_lane_of(gl, gmax))
    gsum = _each(lambda v, m: jnp.sum(jnp.where(is_g, jnp.exp(v - m), 0.0), axis=-1, keepdims=True), lg, gmax)
    lane_group = jnp.right_shift(lane_i, EXPERTS_PER_GROUP.bit_length() - 1).astype(F32)
    el = _each(lambda v, g: jnp.where((lane_i < N_EXPERTS) & (lane_group == g), v, NEG_INF), lg, gidx)
    m1 = row_max(el)
    e0 = first_lane_of(el, m1)
    el2 = _each(lambda v, i: jnp.where(lane == i, NEG_INF, v), el, e0)
    m2 = row_max(el2)
    e1 = first_lane_of(el2, m2)
    ratio = _each(lambda a, b: jnp.exp(b - a), m1, m2)
    w0 = _each(lambda s_, r: 1.0 / (s_ * (1.0 + r)), gsum, ratio)
    w1 = _each(lambda w, r: w * r, w0, ratio)

    pick0 = _each(lambda i: lane == i, e0)
    pick1 = _each(lambda i: lane == i, e1)
    onehot = jnp.concatenate(_each(lambda p, q: jnp.where(p | q, 1.0, 0.0), pick0, pick1), axis=0)
    trow = _iota2((tm, tm), 0)
    tcol = _iota2((tm, tm), 1)
    before = jnp.where(trow > tcol, 1.0, 0.0).astype(BF16)
    cnt = _dot(before, onehot.astype(BF16)) + carry_scr[0:1, :]
    cnts = [cnt[j * sub:(j + 1) * sub, :] for j in range(tm // sub)]
    rank0 = _each(lambda p, c_: jnp.sum(jnp.where(p, c_, 0.0), axis=-1, keepdims=True), pick0, cnts)
    rank1 = _each(lambda p, c_: jnp.sum(jnp.where(p, c_, 0.0), axis=-1, keepdims=True), pick1, cnts)
    total = carry_scr[0:1, :] + jnp.sum(onehot, axis=0, keepdims=True)
    carry_scr[...] = jnp.broadcast_to(total, carry_scr.shape)
    cnt_ref[...] = jnp.broadcast_to(total, cnt_ref.shape).astype(jnp.int32)

    for j, r in enumerate(subs):
        ri = jnp.where(lane_i == 0, e0[j], jnp.where(lane_i == 1, e1[j], 0.0))
        ri = jnp.where(lane_i == 2, rank0[j], jnp.where(lane_i == 3, rank1[j], ri))
        ri_ref[:, r] = jnp.transpose(ri)[0:SUBLANES, :].astype(jnp.int32)
        rw_ref[r, :] = jnp.where(lane_i == 0, w0[j], jnp.where(lane_i == 1, w1[j], 0.0))


def _outproj(ga, gb, x2, w_out, norm_w, wr_hi, wr_lo):
    n = x2.shape[0]
    tm = OUTPROJ_TM
    kern = functools.partial(_outproj_kernel, tm=tm)
    row_blk = lambda w: pl.BlockSpec((tm, w), lambda i: (i, 0))
    const = lambda shape: pl.BlockSpec(shape, lambda i: (0, 0))
    return pl.pallas_call(
        kern,
        grid=(n // tm,),
        in_specs=[
            row_blk(D_MODEL), row_blk(D_MODEL), row_blk(D_MODEL),
            const((D_MODEL, D_MODEL)), const((1, D_MODEL)),
            const((D_MODEL, LANES)), const((D_MODEL, LANES)),
        ],
        out_specs=[row_blk(D_MODEL),
                   pl.BlockSpec((tm * ROW_TILE, LANES), lambda i: (i, 0)),
                   pl.BlockSpec((SUBLANES, tm), lambda i: (0, i)),
                   row_blk(LANES),
                   const((SUBLANES, LANES))],
        out_shape=[
            jax.ShapeDtypeStruct((n, D_MODEL), F32),
            jax.ShapeDtypeStruct((n * ROW_TILE, LANES), U32),
            jax.ShapeDtypeStruct((SUBLANES, n), jnp.int32),
            jax.ShapeDtypeStruct((n, LANES), F32),
            jax.ShapeDtypeStruct((SUBLANES, LANES), jnp.int32),
        ],
        scratch_shapes=[pltpu.VMEM((SUBLANES, LANES), F32)],
        compiler_params=pltpu.CompilerParams(
            dimension_semantics=("arbitrary",), vmem_limit_bytes=VMEM_LIMIT),
        name="outproj",
    )(ga, gb, x2, w_out, norm_w, wr_hi, wr_lo)


def _row_copy(src_ref, src_row, dst_ref, dst_row, sem):
    src = src_ref.at[pl.ds(pl.multiple_of(src_row * ROW_TILE, ROW_TILE), ROW_TILE)]
    dst = dst_ref.at[pl.ds(pl.multiple_of(dst_row * ROW_TILE, ROW_TILE), ROW_TILE)]
    return pltpu.make_async_copy(src, dst, sem)


def _rows_copy(src_ref, dst_ref, dst_row, n_rows, sem):
    src = src_ref.at[pl.ds(0, n_rows * ROW_TILE)]
    dst = dst_ref.at[pl.ds(pl.multiple_of(dst_row * ROW_TILE, ROW_TILE), n_rows * ROW_TILE)]
    return pltpu.make_async_copy(src, dst, sem)


def _zero_fill(zero_scr, xs_ref, lo, hi, sem, wait):
    def go(copy):
        if wait:
            copy.wait()
        else:
            copy.start()

    length = hi - lo
    n_full = lax.shift_right_logical(length, ZERO_ROWS.bit_length() - 1)

    def full(j, carry):
        go(_rows_copy(zero_scr, xs_ref, lo + j * ZERO_ROWS, ZERO_ROWS, sem))
        return carry

    lax.fori_loop(0, n_full, full, 0)
    pos = lo + n_full * ZERO_ROWS
    piece = ZERO_ROWS // 2
    while piece >= 1:
        has = jnp.bitwise_and(length, piece) != 0

        @pl.when(has)
        def _(pos=pos, piece=piece):
            go(_rows_copy(zero_scr, xs_ref, pos, piece, sem))

        pos = pos + jnp.where(has, piece, 0)
        piece //= 2


def _scatter_kernel(seg_ref, d0_ref, d1_ref, h2_ref, xs_ref, zero_scr, sem, zsem, *, tile):
    i = pl.program_id(0)

    def issue(t, carry):
        _row_copy(h2_ref, t, xs_ref, d0_ref[t], sem).start(priority=0)
        _row_copy(h2_ref, t, xs_ref, d1_ref[t], sem).start(priority=1)
        return carry

    lax.fori_loop(0, tile, issue, 0, unroll=DMA_UNROLL)

    @pl.when(i == 0)
    def _():
        zero_scr[...] = jnp.zeros_like(zero_scr)
        for wait in (False, True):
            def per_segment(e, carry, wait=wait):
                _zero_fill(zero_scr, xs_ref, seg_ref[0, e], seg_ref[1, e], zsem, wait)
                return carry

            lax.fori_loop(0, N_EXPERTS + 1, per_segment, 0)

    def drain(t, carry):
        _row_copy(h2_ref, 0, xs_ref, 0, sem).wait()
        _row_copy(h2_ref, 0, xs_ref, 0, sem).wait()
        return carry

    lax.fori_loop(0, tile, drain, 0, unroll=DMA_UNROLL)


def _scatter(seg, dest0, dest1, h2t, n_rows):
    n = dest0.shape[0]
    tile = SCATTER_T
    kern = functools.partial(_scatter_kernel, tile=tile)
    return pl.pallas_call(
        kern,
        grid=(n // tile,),
        in_specs=[
            pl.BlockSpec(memory_space=pltpu.SMEM),
            pl.BlockSpec((tile,), lambda i: (i,), memory_space=pltpu.SMEM),
            pl.BlockSpec((tile,), lambda i: (i,), memory_space=pltpu.SMEM),
            pl.BlockSpec((tile * ROW_TILE, LANES), lambda i: (i, 0)),
        ],
        out_specs=pl.BlockSpec(memory_space=pl.ANY),
        out_shape=jax.ShapeDtypeStruct((n_rows * ROW_TILE, LANES), U32),
        scratch_shapes=[pltpu.VMEM((ZERO_ROWS * ROW_TILE, LANES), U32),
                        pltpu.SemaphoreType.DMA, pltpu.SemaphoreType.DMA],
        compiler_params=pltpu.CompilerParams(
            dimension_semantics=("arbitrary",), vmem_limit_bytes=VMEM_LIMIT),
        name="scatter",
    )(seg, dest0, dest1, h2t)


def _expert_kernel(be_ref, nv_ref, xs_ref, wg_ref, wu_ref, wd_ref, y_ref):
    i = pl.program_id(0)

    @pl.when(i < nv_ref[0])
    def _():
        x = _load_row_tiles(xs_ref).astype(BF16)
        g = _dot(x, wg_ref[0].astype(BF16))
        u = _dot(x, wu_ref[0].astype(BF16))
        hid = (_silu(g) * u).astype(BF16)
        _store_row_tiles(y_ref, _dot(hid, wd_ref[0].astype(BF16)))

    @pl.when(i >= nv_ref[0])
    def _():
        y_ref[...] = jnp.zeros_like(y_ref)


def _experts(block_expert, n_valid, xs, w_gate, w_up, w_down):
    blk = MOE_BLOCK
    n_rows = xs.shape[0] // ROW_TILE
    grid_spec = pltpu.PrefetchScalarGridSpec(
        num_scalar_prefetch=2,
        grid=(n_rows // blk,),
        in_specs=[
            pl.BlockSpec((blk * ROW_TILE, LANES), lambda i, be, nv: (jnp.minimum(i, nv[0] - 1), 0)),
            pl.BlockSpec((1, D_MODEL, D_EXPERT), lambda i, be, nv: (be[i], 0, 0)),
            pl.BlockSpec((1, D_MODEL, D_EXPERT), lambda i, be, nv: (be[i], 0, 0)),
            pl.BlockSpec((1, D_EXPERT, D_MODEL), lambda i, be, nv: (be[i], 0, 0)),
        ],
        out_specs=pl.BlockSpec((blk * ROW_TILE, LANES), lambda i, be, nv: (i, 0)),
    )
    return pl.pallas_call(
        _expert_kernel,
        grid_spec=grid_spec,
        out_shape=jax.ShapeDtypeStruct((n_rows * ROW_TILE, LANES), U32),
        compiler_params=pltpu.CompilerParams(
            dimension_semantics=("arbitrary",), vmem_limit_bytes=VMEM_LIMIT),
        name="experts",
    )(block_expert, n_valid, xs, w_gate, w_up, w_down)


def _combine_kernel(d0_ref, d1_ref, x1_ref, rw_ref, nw_ref, y_ref, out_ref, ya_scr, yb_scr, sems, *, tile):
    half = tile // 2

    def issue(h):
        def body(t, carry):
            _row_copy(y_ref, d0_ref[t], ya_scr, t, sems.at[h]).start(priority=0)
            _row_copy(y_ref, d1_ref[t], yb_scr, t, sems.at[h]).start(priority=1)
            return carry
        lax.fori_loop(h * half, (h + 1) * half, body, 0, unroll=DMA_UNROLL)

    def drain(h):
        def body(t, carry):
            _row_copy(y_ref, 0, ya_scr, 0, sems.at[h]).wait()
            _row_copy(y_ref, 0, yb_scr, 0, sems.at[h]).wait()
            return carry
        lax.fori_loop(0, half, body, 0, unroll=DMA_UNROLL)

    def combine(h):
        rows = pl.ds(h * half, half)
        tiles = pl.ds(h * half * ROW_TILE, half * ROW_TILE)
        rw = rw_ref[rows, :]
        moe = rw[:, 0:1] * _load_row_tiles(ya_scr.at[tiles]) + rw[:, 1:2] * _load_row_tiles(yb_scr.at[tiles])
        x2 = x1_ref[rows, :] + moe
        out_ref[rows, :] = x2 * lax.rsqrt(jnp.mean(x2 * x2, axis=-1, keepdims=True) + EPS) * nw_ref[...]

    issue(0)
    issue(1)
    drain(0)
    combine(0)
    drain(1)
    combine(1)


def _combine(dest0, dest1, x1, rw, norm_w, y):
    n = x1.shape[0]
    tile = COMBINE_T
    kern = functools.partial(_combine_kernel, tile=tile)
    return pl.pallas_call(
        kern,
        grid=(n // tile,),
        in_specs=[
            pl.BlockSpec((tile,), lambda i: (i,), memory_space=pltpu.SMEM),
            pl.BlockSpec((tile,), lambda i: (i,), memory_space=pltpu.SMEM),
            pl.BlockSpec((tile, D_MODEL), lambda i: (i, 0)),
            pl.BlockSpec((tile, LANES), lambda i: (i, 0)),
            pl.BlockSpec((1, D_MODEL), lambda i: (0, 0)),
            pl.BlockSpec(memory_space=pl.ANY),
        ],
        out_specs=pl.BlockSpec((tile, D_MODEL), lambda i: (i, 0)),
        out_shape=jax.ShapeDtypeStruct((n, D_MODEL), F32),
        scratch_shapes=[
            pltpu.VMEM((tile * ROW_TILE, LANES), U32), pltpu.VMEM((tile * ROW_TILE, LANES), U32),
            pltpu.SemaphoreType.DMA((2,)),
        ],
        compiler_params=pltpu.CompilerParams(
            dimension_semantics=("arbitrary",), vmem_limit_bytes=VMEM_LIMIT),
        name="combine",
    )(dest0, dest1, x1, rw, norm_w, y)


def _pad_cols(w, width):
    return jnp.pad(w, ((0, 0), (0, width - w.shape[1])))


def _token_mixer_and_moe(x, norm1_w, w_in, w2_f, b_f, w2_b, b_b, gla_norm_w, conv_w, a_log_f, dt_bias_f,
                         a_log_b, dt_bias_b, gdn_norm_w, w_out, norm2_w, w_group, w_router, w_gate, w_up,
                         w_down, out_norm_w):
    batch, seq, d = x.shape
    n = batch * seq
    x2 = x.reshape(n, d)

    w_main = jnp.concatenate([w_in[:, :3072], w_in[:, 3104:7200], w_in[:, 7232:]], axis=1).astype(BF16)
    w_small = _pad_cols(jnp.concatenate([w_in[:, 3072:3104], w_in[:, 7200:7232]], axis=1), LANES)
    ws_hi, ws_lo = _split2(w_small)
    main, small = _inproj(x2, norm1_w.reshape(1, d), w_main, ws_hi, ws_lo)

    w2f_pad = jnp.zeros((LANES, GLA_HEADS * GLA_DK), F32).at[0:GLA_GATE_RANK].set(w2_f)
    w2b_pad = jnp.zeros((LANES, GLA_HEADS * GLA_DK), F32).at[GLA_GATE_RANK:2 * GLA_GATE_RANK].set(w2_b)
    ga = _gla(main, small, w2f_pad, w2b_pad, b_f.reshape(1, -1), b_b.reshape(1, -1),
              gla_norm_w.reshape(1, -1), batch, seq)

    gates = jnp.zeros((SUBLANES, LANES), F32)
    gates = gates.at[0, SMALL_AF:SMALL_BF].set(jnp.concatenate([a_log_f, a_log_b]))
    gates = gates.at[1, SMALL_AF:SMALL_BF].set(jnp.concatenate([dt_bias_f, dt_bias_b]))
    gb = _gdn(main, small, gates, conv_w, gdn_norm_w.reshape(1, -1), batch, seq)

    w_route = _pad_cols(jnp.concatenate([w_router, w_group], axis=1), LANES)
    wr_hi, wr_lo = _split2(w_route)
    x1, h2t, rt, rw, counts = _outproj(ga, gb, x2, w_out.astype(BF16), norm2_w.reshape(1, d), wr_hi, wr_lo)

    blk = MOE_BLOCK
    cnt = counts[0, :N_EXPERTS]
    padded = (cnt + blk - 1) // blk * blk
    ends = jnp.cumsum(padded)
    pstart = (ends - padded).astype(jnp.int32)
    n_blocks = -(-(2 * n + N_EXPERTS * (blk - 1)) // blk)
    n_rows = n_blocks * blk
    block_row = jnp.arange(n_blocks, dtype=jnp.int32) * blk
    block_expert = jnp.minimum(
        jnp.sum((ends[None, :] <= block_row[:, None]).astype(jnp.int32), axis=1), N_EXPERTS - 1)
    n_valid = (ends[-1:] // blk).astype(jnp.int32)
    seg = jnp.stack([jnp.append(pstart + cnt, ends[-1]), jnp.append(ends, n_rows)]).astype(jnp.int32)

    experts = jnp.arange(N_EXPERTS, dtype=jnp.int32)
    seg_start = jnp.sum(jnp.where(rt[0:2, :, None] == experts, pstart, 0), axis=-1)
    dest = seg_start + rt[2:4]
    dest0, dest1 = dest[0], dest[1]

    xs = _scatter(seg, dest0, dest1, h2t, n_rows)
    y = _experts(block_expert, n_valid, xs, w_gate, w_up, w_down)
    out = _combine(dest0, dest1, x1, rw, out_norm_w.reshape(1, d), y)
    return out.reshape(batch, seq, d)


def kernel(x, norm1_w, w_in, gla_gate_w2_fwd, gla_gate_b_fwd, gla_gate_w2_bwd, gla_gate_b_bwd, gla_norm_w,
           gdn_conv_w, gdn_a_log_fwd, gdn_dt_bias_fwd, gdn_a_log_bwd, gdn_dt_bias_bwd, gdn_norm_w, w_out,
           norm2_w, moe_w_group, moe_w_router, moe_w_gate, moe_w_up, moe_w_down, norm_f_w):
    assert norm1_w.shape[0] == 1, "single-layer block"
    return _token_mixer_and_moe(
        x, norm1_w[0], w_in[0], gla_gate_w2_fwd[0], gla_gate_b_fwd[0], gla_gate_w2_bwd[0], gla_gate_b_bwd[0],
        gla_norm_w[0], gdn_conv_w[0], gdn_a_log_fwd[0], gdn_dt_bias_fwd[0], gdn_a_log_bwd[0],
        gdn_dt_bias_bwd[0], gdn_norm_w[0], w_out[0], norm2_w[0], moe_w_group[0], moe_w_router[0],
        moe_w_gate[0], moe_w_up[0], moe_w_down[0], norm_f_w)
```

```python
import functools

import jax
import jax.numpy as jnp
import numpy as np
from jax import lax
from jax.experimental import pallas as pl
from jax.experimental.pallas import tpu as pltpu

F32 = jnp.float32
BF16 = jnp.bfloat16
U32 = jnp.uint32

D_MODEL = 1024
GLA_HEADS = 4
GLA_DK = 128
GLA_DV = 256
GLA_GATE_RANK = 16
GLA_GATE_TAU = 16.0
GLA_CHUNK = 64
GLA_GROUP = 8
GDN_HEADS = 8
GDN_DK = 128
GDN_DV = 128
GDN_CONV = 5
GDN_CHUNK = 128
GDN_PREP_GROUP = 8
N_GROUPS = 4
EXPERTS_PER_GROUP = 8
N_EXPERTS = N_GROUPS * EXPERTS_PER_GROUP
D_EXPERT = 256
EPS = 1e-6

LANES = 128
SUBLANES = 8
VMEM_LIMIT = 48 * 1024 * 1024

COL_GQ, COL_GK, COL_GV, COL_GR = 0, 512, 1024, 2048
COL_DQ, COL_DK, COL_DV, COL_DZ = 3072, 4096, 5120, 6144
COL_MA, COL_MB = 7168, 8192
D_MAIN = 9216
SMALL_AF, SMALL_AB, SMALL_BF, SMALL_BB = 32, 40, 48, 56
ROUTE_GROUP_LANE = 32

MOE_BLOCK = 512
ROW_TILE = D_MODEL // 2 // LANES
HIGH_HALF = np.uint32(0xFFFF0000)
ZERO_ROWS = 128
DMA_UNROLL = 8
INPROJ_TM, INPROJ_TN = 1024, 1024
OUTPROJ_TM = 512
OUTPROJ_SUB = 128
SCATTER_T = 512
COMBINE_T = 256
CONV_ROWS = 256
NEG_INF = float("-inf")


def _dot(a, b):
    return jnp.dot(a, b, preferred_element_type=F32)


def _dot_nt(a, b):
    return lax.dot_general(a, b, (((1,), (1,)), ((), ())), preferred_element_type=F32)


def _dot_tn(a, b):
    return lax.dot_general(a, b, (((0,), (0,)), ((), ())), preferred_element_type=F32)


def _split2(x):
    hi = x.astype(BF16)
    lo = (x - hi.astype(F32)).astype(BF16)
    return hi, lo


def _split3(x):
    hi = x.astype(BF16)
    r = x - hi.astype(F32)
    mid = r.astype(BF16)
    lo = (r - mid.astype(F32)).astype(BF16)
    return hi, mid, lo


def _dot_exact_rhs(x, m_bf16):
    hi, mid, lo = _split3(x)
    return _dot(hi, m_bf16) + _dot(mid, m_bf16) + _dot(lo, m_bf16)


def _dot_exact_lhs(m_bf16, x):
    hi, mid, lo = _split3(x)
    return _dot(m_bf16, hi) + _dot(m_bf16, mid) + _dot(m_bf16, lo)


def _dot_lhs2(m_bf16, x):
    hi, lo = _split2(x)
    return _dot(m_bf16, hi) + _dot(m_bf16, lo)


def _dot_lhs2_wide(m2_bf16, x):
    return _dot(m2_bf16, jnp.concatenate(_split2(x), axis=0))


def _cumsum_rows(x, reverse):
    rows = x.shape[0]
    row = _iota2(x.shape, 0)
    shift = 1
    while shift < rows:
        if reverse:
            x = x + jnp.where(row < rows - shift, pltpu.roll(x, rows - shift, axis=0), 0.0)
        else:
            x = x + jnp.where(row >= shift, pltpu.roll(x, shift, axis=0), 0.0)
        shift *= 2
    return x


def _dot3(a, b):
    ah, al = _split2(a)
    bh, bl = _split2(b)
    return _dot(ah, bh) + _dot(al, bh) + _dot(ah, bl)


def _store_row_tiles(ref, x):
    rows = x.shape[0]
    half = D_MODEL // 2
    hi = lax.bitcast_convert_type(x[:, :half].astype(BF16).astype(F32), U32)
    lo = lax.bitcast_convert_type(x[:, half:].astype(BF16).astype(F32), U32)
    packed = jnp.bitwise_or(jnp.bitwise_and(hi, HIGH_HALF), jnp.right_shift(lo, 16))
    for j in range(ROW_TILE):
        ref[pl.ds(j, rows, stride=ROW_TILE), :] = packed[:, j * LANES:(j + 1) * LANES]


def _load_row_tiles(ref):
    rows = ref.shape[0] // ROW_TILE
    packed = jnp.concatenate([ref[pl.ds(j, rows, stride=ROW_TILE), :] for j in range(ROW_TILE)], axis=1)
    hi = lax.bitcast_convert_type(jnp.bitwise_and(packed, HIGH_HALF), F32)
    lo = lax.bitcast_convert_type(jnp.left_shift(packed, 16), F32)
    return jnp.concatenate([hi, lo], axis=1)


def _each(fn, *lists):
    return [fn(*args) for args in zip(*lists)]


def _sigmoid(x):
    return 1.0 / (1.0 + jnp.exp(-x))


def _silu(x):
    return x * _sigmoid(x)


def _softplus(x):
    return jnp.maximum(x, 0.0) + jnp.log(1.0 + jnp.exp(-jnp.abs(x)))


def _log_sigmoid(x):
    return jnp.minimum(x, 0.0) - jnp.log(1.0 + jnp.exp(-jnp.abs(x)))


def _iota2(shape, dim):
    return lax.broadcasted_iota(jnp.int32, shape, dim)


def _inproj_kernel(x_ref, nw_ref, w_ref, wsh_ref, wsl_ref, main_ref, small_ref, h_scr):
    @pl.when(pl.program_id(1) == 0)
    def _():
        x = x_ref[...]
        h = x * lax.rsqrt(jnp.mean(x * x, axis=-1, keepdims=True) + EPS) * nw_ref[...]
        hh, hl = _split2(h)
        h_scr[...] = hh
        small_ref[...] = _dot(hh, wsh_ref[...]) + _dot(hl, wsh_ref[...]) + _dot(hh, wsl_ref[...])

    main_ref[...] = _dot(h_scr[...], w_ref[...]).astype(BF16)


def _inproj(x2, norm_w, w_main, ws_hi, ws_lo):
    n = x2.shape[0]
    tm, tn = INPROJ_TM, INPROJ_TN
    return pl.pallas_call(
        _inproj_kernel,
        grid=(n // tm, D_MAIN // tn),
        in_specs=[
            pl.BlockSpec((tm, D_MODEL), lambda i, j: (i, 0)),
            pl.BlockSpec((1, D_MODEL), lambda i, j: (0, 0)),
            pl.BlockSpec((D_MODEL, tn), lambda i, j: (0, j)),
            pl.BlockSpec((D_MODEL, LANES), lambda i, j: (0, 0)),
            pl.BlockSpec((D_MODEL, LANES), lambda i, j: (0, 0)),
        ],
        out_specs=[
            pl.BlockSpec((tm, tn), lambda i, j: (i, j)),
            pl.BlockSpec((tm, LANES), lambda i, j: (i, 0)),
        ],
        out_shape=[
            jax.ShapeDtypeStruct((n, D_MAIN), BF16),
            jax.ShapeDtypeStruct((n, LANES), F32),
        ],
        scratch_shapes=[pltpu.VMEM((tm, D_MODEL), BF16)],
        compiler_params=pltpu.CompilerParams(
            dimension_semantics=("arbitrary", "arbitrary"), vmem_limit_bytes=VMEM_LIMIT),
        name="inproj",
    )(x2, norm_w, w_main, ws_hi, ws_lo)


def _gla_kernel(q_ref, k_ref, v_ref, gr_ref, ma_ref, small_ref, w2f_ref, w2b_ref, bf_ref, bb_ref,
                nw_ref, out_ref, laf_scr, lab_scr, o_scr, stf_scr, stb_scr, *, seq, chunk):
    c = chunk
    n = seq // c
    scale = GLA_DK ** -0.5

    sm = small_ref[...].astype(BF16)
    laf_scr[...] = _log_sigmoid(_dot(sm, w2f_ref[...].astype(BF16)) + bf_ref[...]) * (1.0 / GLA_GATE_TAU)
    lab_scr[...] = _log_sigmoid(_dot(sm, w2b_ref[...].astype(BF16)) + bb_ref[...]) * (1.0 / GLA_GATE_TAU)
    stf_scr[...] = jnp.zeros_like(stf_scr)
    stb_scr[...] = jnp.zeros_like(stb_scr)

    row = _iota2((c, c), 0)
    col = _iota2((c, c), 1)
    low = row >= col
    upp = row <= col
    low_m = jnp.concatenate([jnp.where(low, 1.0, 0.0).astype(BF16)] * 2, axis=1)
    upp_m = jnp.concatenate([jnp.where(upp, 1.0, 0.0).astype(BF16)] * 2, axis=1)

    g = GLA_GROUP

    def finish(rows, o):
        y = o * lax.rsqrt(jnp.mean(o * o, axis=-1, keepdims=True) + EPS) * nw_ref[...]
        y = y * _silu(gr_ref[rows, :].astype(F32))
        y = y * _sigmoid(ma_ref[rows, :].astype(F32))
        out_ref[rows, :] = y.astype(BF16)

    def group(gi, second_touch):
        ids = [gi * g + j for j in range(g)] + [n - 1 - gi * g - j for j in range(g)]
        rows = [pl.ds(pl.multiple_of(i * c, c), c) for i in ids]
        la = [laf_scr[r, :] for r in rows[:g]] + [lab_scr[r, :] for r in rows[g:]]
        csum = [low_m] * g + [upp_m] * g
        mask = [low] * g + [upp] * g
        tot_row = [c - 1] * g + [0] * g
        qf = [q_ref[r, :].astype(F32) * scale for r in rows]
        kf = [k_ref[r, :].astype(F32) for r in rows]
        vc = [v_ref[r, :] for r in rows]

        cum = _each(_dot_lhs2_wide, csum, la)
        tot = _each(lambda x, r: x[r:r + 1, :], cum, tot_row)
        q_dec = _each(lambda q, x: (q * jnp.exp(x)).astype(BF16), qf, cum)
        k_inv = _each(lambda k, x: (k * jnp.exp(-x)).astype(BF16), kf, cum)
        k_tail = _each(lambda k, t, x: (k * jnp.exp(t - x)).astype(BF16), kf, tot, cum)
        s = _each(lambda m, q, k: jnp.where(m, _dot_nt(q, k), 0.0).astype(BF16), mask, q_dec, k_inv)
        o = _each(_dot, s, vc)
        kv = _each(_dot_tn, vc, k_tail)
        dec = _each(jnp.exp, tot)

        for st_scr, probs in ((stf_scr, range(g)), (stb_scr, range(g, 2 * g))):
            st = st_scr[...]
            for p in probs:
                o[p] = o[p] + _dot_nt(q_dec[p], st.astype(BF16))
                st = dec[p] * st + kv[p]
            st_scr[...] = st

        for r, o_p in zip(rows, o):
            if second_touch:
                finish(r, o_scr[r, :] + o_p)
            else:
                o_scr[r, :] = o_p

    def first_half(gi, carry):
        group(gi, False)
        return carry

    def second_half(gi, carry):
        group(gi, True)
        return carry

    n_groups = n // g
    lax.fori_loop(0, n_groups // 2, first_half, 0)
    lax.fori_loop(n_groups // 2, n_groups, second_half, 0)


def _gla(main, small, w2f_pad, w2b_pad, b_f, b_b, norm_w, batch, seq):
    n = batch * seq
    h = GLA_HEADS
    kern = functools.partial(_gla_kernel, seq=seq, chunk=GLA_CHUNK)
    qk_blk = lambda off: pl.BlockSpec((seq, GLA_DK), lambda b, hh, off=off: (b, off // GLA_DK + hh))
    v_blk = lambda off: pl.BlockSpec((seq, GLA_DV), lambda b, hh, off=off: (b, off // GLA_DV + hh))
    return pl.pallas_call(
        kern,
        grid=(batch, h),
        in_specs=[
            qk_blk(COL_GQ), qk_blk(COL_GK), v_blk(COL_GV), v_blk(COL_GR), v_blk(COL_MA),
            pl.BlockSpec((seq, LANES), lambda b, hh: (b, 0)),
            pl.BlockSpec((LANES, GLA_DK), lambda b, hh: (0, hh)),
            pl.BlockSpec((LANES, GLA_DK), lambda b, hh: (0, hh)),
            pl.BlockSpec((1, GLA_DK), lambda b, hh: (0, hh)),
            pl.BlockSpec((1, GLA_DK), lambda b, hh: (0, hh)),
            pl.BlockSpec((1, GLA_DV), lambda b, hh: (0, 0)),
        ],
        out_specs=pl.BlockSpec((seq, GLA_DV), lambda b, hh: (b, hh)),
        out_shape=jax.ShapeDtypeStruct((n, D_MODEL), BF16),
        scratch_shapes=[
            pltpu.VMEM((seq, GLA_DK), F32), pltpu.VMEM((seq, GLA_DK), F32),
            pltpu.VMEM((seq, GLA_DV), F32),
            pltpu.VMEM((GLA_DV, GLA_DK), F32), pltpu.VMEM((GLA_DV, GLA_DK), F32),
        ],
        compiler_params=pltpu.CompilerParams(
            dimension_semantics=("arbitrary", "arbitrary"), vmem_limit_bytes=VMEM_LIMIT),
        name="gla",
    )(main, main, main, main, main, small, w2f_pad, w2b_pad, b_f, b_b, norm_w)


TRI_BLOCK = 16


def _mm(a, b):
    return _dot(a.astype(BF16), b.astype(BF16))


def _nilpotent_inverse(a_list, eye, index, tick):
    t_list = _each(lambda a: eye - a, a_list)
    p_list = a_list
    power = 2
    while power < index:
        p_list = _each(lambda p: _mm(p, p), p_list)
        tick()
        t_list = _each(lambda t, p: t + _mm(t, p), t_list, p_list)
        tick()
        power *= 2
    return t_list


def _tri_inverse(a_list, eye, diag_blocks, chunk, tick):
    ad_list = _each(lambda a: jnp.where(diag_blocks, a, 0.0), a_list)
    ao_list = _each(lambda a: jnp.where(diag_blocks, 0.0, a), a_list)
    d_list = _nilpotent_inverse(ad_list, eye, TRI_BLOCK, tick)
    n_list = _each(_mm, d_list, ao_list)
    tick()
    t_list = _nilpotent_inverse(n_list, eye, chunk // TRI_BLOCK, tick)
    out = _each(_mm, t_list, d_list)
    tick()
    return out


def _gdn_kernel(gate_ref, q_ref, k_ref, v_ref, z_ref, mb_ref, small_ref, cwq_ref, cwk_ref, cwv_ref,
                nw_ref, out_ref, pad_scr, qs_scr, ks_scr, vs_scr, gf_scr, gb_scr, btf_scr, btb_scr,
                o_scr, nmat_scr, bmat_scr, qp_scr, cd_scr, sf_scr, sb_scr, *, seq, chunk, n_heads_total):
    c = chunk
    n = seq // c
    step = pl.program_id(0)
    hh = lax.rem(jnp.minimum(step, n_heads_total - 1), GDN_HEADS)
    cur = lax.rem(step, 2)
    prev = 1 - cur
    scale = GDN_DK ** -0.5

    @pl.when(step == 0)
    def _():
        nmat_scr[...] = jnp.zeros_like(nmat_scr)
        bmat_scr[...] = jnp.zeros_like(bmat_scr)
        qp_scr[...] = jnp.zeros_like(qp_scr)
        cd_scr[...] = jnp.zeros_like(cd_scr)
        o_scr[...] = jnp.zeros_like(o_scr)

    zeros8 = jnp.zeros((SUBLANES, LANES), F32)
    pad_scr[0:SUBLANES, :] = zeros8
    pad_scr[seq + SUBLANES:seq + 2 * SUBLANES, :] = zeros8
    half = GDN_CONV // 2

    def conv_into(src_ref, cw_ref, dst_ref, normalise, mult):
        pad_scr[SUBLANES:seq + SUBLANES, :] = src_ref[...].astype(F32)
        w = cw_ref[...]

        def body(i, carry):
            r0 = pl.multiple_of(i * CONV_ROWS, CONV_ROWS)
            acc = jnp.zeros((CONV_ROWS, LANES), F32)
            for j in range(GDN_CONV):
                tap = pad_scr[pl.ds(r0 + (SUBLANES - half + j), CONV_ROWS), :]
                acc = acc + tap * w[j:j + 1, :]
            y = _silu(acc)
            if normalise:
                y = y * lax.rsqrt(jnp.sum(y * y, axis=-1, keepdims=True) + EPS) * mult
            dst_ref[pl.ds(r0, CONV_ROWS), :] = y.astype(BF16)
            return carry

        lax.fori_loop(0, seq // CONV_ROWS, body, 0, unroll=4)

    conv_into(q_ref, cwq_ref, qs_scr, True, scale)
    conv_into(k_ref, cwk_ref, ks_scr, True, 1.0)
    conv_into(v_ref, cwv_ref, vs_scr, False, 1.0)

    sm = small_ref[...]
    lane = _iota2(sm.shape, 1)
    log_decay = -jnp.exp(gate_ref[0:1, :]) * _softplus(sm + gate_ref[1:2, :])
    gate_vals = jnp.where(lane < SMALL_BF, log_decay, _sigmoid(sm))
    gate_hl = jnp.concatenate(_split2(gate_vals), axis=1)
    sel_lane = jnp.bitwise_and(_iota2((2 * LANES, 4 * LANES), 0), LANES - 1)
    sel_gate = jnp.right_shift(_iota2((2 * LANES, 4 * LANES), 1), LANES.bit_length() - 1)
    sel = jnp.where(sel_lane == SMALL_AF + GDN_HEADS * sel_gate + hh, 1.0, 0.0).astype(BF16)
    spread = _dot(gate_hl, sel)
    gf_scr[...] = spread[:, 0 * LANES:1 * LANES]
    gb_scr[...] = spread[:, 1 * LANES:2 * LANES]
    btf_scr[...] = spread[:, 2 * LANES:3 * LANES]
    btb_scr[...] = spread[:, 3 * LANES:4 * LANES]
    sf_scr[...] = jnp.zeros_like(sf_scr)
    sb_scr[...] = jnp.zeros_like(sb_scr)

    row = _iota2((c, c), 0)
    col = _iota2((c, c), 1)
    eye = jnp.where(row == col, 1.0, 0.0).astype(F32)
    low, slow = row >= col, row > col
    upp, supp = row <= col, row < col
    assert c == LANES, "the decay matrix is formed from a [c, 128] lane-broadcast column and its transpose"

    tri_shift = TRI_BLOCK.bit_length() - 1
    diag_blocks = jnp.right_shift(row, tri_shift) == jnp.right_shift(col, tri_shift)

    cur_slot, prev_slot = cur * (2 * n), prev * (2 * n)
    cur_row, prev_row = cur * seq, prev * seq

    def scan_step(i):
        j = n - 1 - i
        slots = [prev_slot + i, prev_slot + n + j]
        rows = [pl.ds(pl.multiple_of(prev_row + i * c, c), c), pl.ds(pl.multiple_of(prev_row + j * c, c), c)]
        states = [sf_scr, sb_scr]
        s = [ref[...] for ref in states]
        s_b = _each(lambda x: x.astype(BF16), s)
        ns = _each(lambda sl, x: _dot(nmat_scr[sl], x), slots, s_b)
        qs = _each(lambda sl, x: _dot(qp_scr[sl], x), slots, s_b)
        for ref, sl, s_, ns_ in zip(states, slots, s, ns):
            ref[...] = cd_scr[sl][0:1, :] * s_ + (bmat_scr[sl] - ns_)
        for r, q in zip(rows, qs):
            o_scr[r, :] += q

    n_groups = n // GDN_PREP_GROUP
    scans_per_group = n // n_groups

    def prep_group(gi, carry):
        pending = [gi * scans_per_group + j for j in range(scans_per_group)]

        def tick():
            if pending:
                scan_step(pending.pop(0))

        chunk_ids = [gi * GDN_PREP_GROUP + j for j in range(GDN_PREP_GROUP)]
        rows = [pl.ds(pl.multiple_of(i * c, c), c) for i in chunk_ids]
        qc = [qs_scr[r, :] for r in rows]
        kc = [ks_scr[r, :] for r in rows]
        vc = [vs_scr[r, :] for r in rows]
        gl = [ref[r, :] for r in rows for ref in (gf_scr, gb_scr)]
        bt = [ref[r, :] for r in rows for ref in (btf_scr, btb_scr)]
        reverse = [False, True] * GDN_PREP_GROUP
        incl = [low, upp] * GDN_PREP_GROUP
        strict = [slow, supp] * GDN_PREP_GROUP
        tot_row = [c - 1, 0] * GDN_PREP_GROUP

        def both(per_chunk):
            return [x for x in per_chunk for _ in range(2)]

        kk = both(_each(_dot_nt, kc, kc))
        qk = both(_each(_dot_nt, qc, kc))
        tick()
        qf = both(_each(lambda x: x.astype(F32), qc))
        kf = both(_each(lambda x: x.astype(F32), kc))
        vf = both(_each(lambda x: x.astype(F32), vc))

        gc = _each(_cumsum_rows, gl, reverse)
        tot = _each(lambda g, r: g[r:r + 1, :], gc, tot_row)
        e = _each(lambda g, inc: jnp.exp(jnp.where(inc, g - jnp.transpose(g), 0.0)), gc, incl)
        a = _each(lambda kk_, b, e_, st: kk_ * b[:, :c] * jnp.where(st, e_, 0.0), kk, bt, e, strict)
        t_inv = _tri_inverse(a, eye, diag_blocks, c, tick)
        egc = _each(jnp.exp, gc)
        wu = _each(lambda t, k, v, b, eg: _mm(t, jnp.concatenate([k * b * eg, v * b], axis=1)).astype(BF16),
                   t_inv, kf, vf, bt, egc)
        tick()
        attn = _each(lambda qk_, e_, inc: (qk_ * jnp.where(inc, e_, 0.0)).astype(BF16), qk, e, incl)
        k_tail = _each(lambda k, t, g: (k * jnp.exp(t - g)).astype(BF16), kf, tot, gc)
        kwu = _each(_dot_tn, k_tail, wu)
        tick()
        awu = _each(_dot, attn, wu)
        while pending:
            tick()

        for p in range(2 * GDN_PREP_GROUP):
            slot = cur_slot + chunk_ids[p // 2] + (p % 2) * n
            nmat_scr[slot] = kwu[p][:, :GDN_DK].astype(BF16)
            bmat_scr[slot] = kwu[p][:, GDN_DK:]
            qp_scr[slot] = (qf[p] * egc[p] - awu[p][:, :GDN_DK]).astype(BF16)
            cd_scr[slot] = jnp.broadcast_to(jnp.exp(tot[p]), (SUBLANES, LANES))
        for j, i in enumerate(chunk_ids):
            r = pl.ds(pl.multiple_of(cur_row + i * c, c), c)
            o_scr[r, :] = awu[2 * j][:, GDN_DK:] + awu[2 * j + 1][:, GDN_DK:]
        return carry

    lax.fori_loop(0, n_groups, prep_group, 0)

    def finish(i, carry):
        r0 = pl.multiple_of(i * CONV_ROWS, CONV_ROWS)
        o = o_scr[pl.ds(pl.multiple_of(prev_row + r0, CONV_ROWS), CONV_ROWS), :]
        y = o * lax.rsqrt(jnp.mean(o * o, axis=-1, keepdims=True) + EPS) * nw_ref[...]
        y = y * _silu(z_ref[pl.ds(r0, CONV_ROWS), :].astype(F32))
        y = y * _sigmoid(mb_ref[pl.ds(r0, CONV_ROWS), :].astype(F32))
        out_ref[pl.ds(r0, CONV_ROWS), :] = y.astype(BF16)
        return carry

    lax.fori_loop(0, seq // CONV_ROWS, finish, 0, unroll=4)


def _gdn(main, small, gates, conv_w, norm_w, batch, seq):
    n = batch * seq
    total = batch * GDN_HEADS
    kern = functools.partial(_gdn_kernel, seq=seq, chunk=GDN_CHUNK, n_heads_total=total)
    n_chunks = seq // GDN_CHUNK

    def head_of(step):
        idx = jnp.minimum(step, total - 1)
        return idx // GDN_HEADS, idx % GDN_HEADS

    def prev_head_of(step):
        idx = jnp.maximum(step - 1, 0)
        return idx // GDN_HEADS, idx % GDN_HEADS

    def blk(off, which):
        def index(s):
            b, hh = which(s)
            return b, off // LANES + hh
        return pl.BlockSpec((seq, LANES), index)

    def cw(part):
        return pl.BlockSpec((GDN_CONV, LANES), lambda s: (0, part * GDN_HEADS + head_of(s)[1]))

    seq_f32 = lambda: pltpu.VMEM((seq, LANES), F32)
    seq_bf16 = lambda: pltpu.VMEM((seq, LANES), BF16)
    return pl.pallas_call(
        kern,
        grid=(total + 1,),
        in_specs=[
            pl.BlockSpec((SUBLANES, LANES), lambda s: (0, 0)),
            blk(COL_DQ, head_of), blk(COL_DK, head_of), blk(COL_DV, head_of),
            blk(COL_DZ, prev_head_of), blk(COL_MB, prev_head_of),
            pl.BlockSpec((seq, LANES), lambda s: (head_of(s)[0], 0)),
            cw(0), cw(1), cw(2),
            pl.BlockSpec((1, GDN_DV), lambda s: (0, 0)),
        ],
        out_specs=pl.BlockSpec((seq, GDN_DV), lambda s: prev_head_of(s)),
        out_shape=jax.ShapeDtypeStruct((n, D_MODEL), BF16),
        scratch_shapes=[
            pltpu.VMEM((seq + 2 * SUBLANES, LANES), F32),
            seq_bf16(), seq_bf16(), seq_bf16(),
            seq_f32(), seq_f32(), seq_f32(), seq_f32(),
            pltpu.VMEM((2 * seq, LANES), F32),
            pltpu.VMEM((4 * n_chunks, GDN_DK, GDN_DK), BF16),
            pltpu.VMEM((4 * n_chunks, GDN_DK, GDN_DV), F32),
            pltpu.VMEM((4 * n_chunks, GDN_CHUNK, GDN_DK), BF16),
            pltpu.VMEM((4 * n_chunks, SUBLANES, LANES), F32),
            pltpu.VMEM((GDN_DK, GDN_DV), F32), pltpu.VMEM((GDN_DK, GDN_DV), F32),
        ],
        compiler_params=pltpu.CompilerParams(
            dimension_semantics=("arbitrary",), vmem_limit_bytes=VMEM_LIMIT),
        name="gdn",
    )(gates, main, main, main, main, main, small, conv_w, conv_w, conv_w, norm_w)


def _outproj_kernel(ga_ref, gb_ref, x_ref, wo_ref, nw_ref, wrh_ref, wrl_ref,
                    x1_ref, h2_ref, ri_ref, rw_ref, cnt_ref, carry_scr, *, tm):
    @pl.when(pl.program_id(0) == 0)
    def _():
        carry_scr[...] = jnp.zeros_like(carry_scr)

    sub = OUTPROJ_SUB
    subs = [pl.ds(j * sub, sub) for j in range(tm // sub)]
    lane_i = _iota2((sub, LANES), 1)
    lane = lane_i.astype(F32)

    mixed = [(ga_ref[r, :].astype(F32) + gb_ref[r, :].astype(F32)).astype(BF16) for r in subs]
    x1 = _each(lambda r, m: x_ref[r, :] + _dot(m, wo_ref[...]), subs, mixed)
    for r, v in zip(subs, x1):
        x1_ref[r, :] = v
    h2 = _each(lambda v: v * lax.rsqrt(jnp.mean(v * v, axis=-1, keepdims=True) + EPS) * nw_ref[...], x1)
    for j, v in enumerate(h2):
        _store_row_tiles(h2_ref.at[pl.ds(j * sub * ROW_TILE, sub * ROW_TILE)], v)
    hl = _each(_split2, h2)
    lg = _each(lambda p: _dot(p[0], wrh_ref[...]) + _dot(p[1], wrh_ref[...]) + _dot(p[0], wrl_ref[...]), hl)

    def row_max(vals):
        return _each(lambda v: jnp.max(v, axis=-1, keepdims=True), vals)

    def first_lane_of(vals, maxima):
        return _each(lambda v, m: jnp.min(jnp.where(v == m, lane, float(LANES)), axis=-1, keepdims=True),
                     vals, maxima)

    is_g = (lane_i >= ROUTE_GROUP_LANE) & (lane_i < ROUTE_GROUP_LANE + N_GROUPS)
    gl = _each(lambda v: jnp.where(is_g, v, NEG_INF), lg)
    gmax = row_max(gl)
    gidx = _each(lambda i: i - float(ROUTE_GROUP_LANE), first_lane_of(gl, gmax))
    gsum = _each(lambda v, m: jnp.sum(jnp.where(is_g, jnp.exp(v - m), 0.0), axis=-1, keepdims=True), lg, gmax)
    lane_group = jnp.right_shift(lane_i, EXPERTS_PER_GROUP.bit_length() - 1).astype(F32)
    el = _each(lambda v, g: jnp.where((lane_i < N_EXPERTS) & (lane_group == g), v, NEG_INF), lg, gidx)
    m1 = row_max(el)
    e0 = first_lane_of(el, m1)
    el2 = _each(lambda v, i: jnp.where(lane == i, NEG_INF, v), el, e0)
    m2 = row_max(el2)
    e1 = first_lane_of(el2, m2)
    ratio = _each(lambda a, b: jnp.exp(b - a), m1, m2)
    w0 = _each(lambda s_, r: 1.0 / (s_ * (1.0 + r)), gsum, ratio)
    w1 = _each(lambda w, r: w * r, w0, ratio)

    pick0 = _each(lambda i: lane == i, e0)
    pick1 = _each(lambda i: lane == i, e1)
    onehot = jnp.concatenate(_each(lambda p, q: jnp.where(p | q, 1.0, 0.0), pick0, pick1), axis=0)
    trow = _iota2((tm, tm), 0)
    tcol = _iota2((tm, tm), 1)
    before = jnp.where(trow > tcol, 1.0, 0.0).astype(BF16)
    cnt = _dot(before, onehot.astype(BF16)) + carry_scr[0:1, :]
    cnts = [cnt[j * sub:(j + 1) * sub, :] for j in range(tm // sub)]
    rank0 = _each(lambda p, c_: jnp.sum(jnp.where(p, c_, 0.0), axis=-1, keepdims=True), pick0, cnts)
    rank1 = _each(lambda p, c_: jnp.sum(jnp.where(p, c_, 0.0), axis=-1, keepdims=True), pick1, cnts)
    total = carry_scr[0:1, :] + jnp.sum(onehot, axis=0, keepdims=True)
    carry_scr[...] = jnp.broadcast_to(total, carry_scr.shape)
    cnt_ref[...] = jnp.broadcast_to(total, cnt_ref.shape).astype(jnp.int32)

    for j, r in enumerate(subs):
        ri = jnp.where(lane_i == 0, e0[j], jnp.where(lane_i == 1, e1[j], 0.0))
        ri = jnp.where(lane_i == 2, rank0[j], jnp.where(lane_i == 3, rank1[j], ri))
        ri_ref[:, r] = jnp.transpose(ri)[0:SUBLANES, :].astype(jnp.int32)
        rw_ref[r, :] = jnp.where(lane_i == 0, w0[j], jnp.where(lane_i == 1, w1[j], 0.0))


def _outproj(ga, gb, x2, w_out, norm_w, wr_hi, wr_lo):
    n = x2.shape[0]
    tm = OUTPROJ_TM
    kern = functools.partial(_outproj_kernel, tm=tm)
    row_blk = lambda w: pl.BlockSpec((tm, w), lambda i: (i, 0))
    const = lambda shape: pl.BlockSpec(shape, lambda i: (0, 0))
    return pl.pallas_call(
        kern,
        grid=(n // tm,),
        in_specs=[
            row_blk(D_MODEL), row_blk(D_MODEL), row_blk(D_MODEL),
            const((D_MODEL, D_MODEL)), const((1, D_MODEL)),
            const((D_MODEL, LANES)), const((D_MODEL, LANES)),
        ],
        out_specs=[row_blk(D_MODEL),
                   pl.BlockSpec((tm * ROW_TILE, LANES), lambda i: (i, 0)),
                   pl.BlockSpec((SUBLANES, tm), lambda i: (0, i)),
                   row_blk(LANES),
                   const((SUBLANES, LANES))],
        out_shape=[
            jax.ShapeDtypeStruct((n, D_MODEL), F32),
            jax.ShapeDtypeStruct((n * ROW_TILE, LANES), U32),
            jax.ShapeDtypeStruct((SUBLANES, n), jnp.int32),
            jax.ShapeDtypeStruct((n, LANES), F32),
            jax.ShapeDtypeStruct((SUBLANES, LANES), jnp.int32),
        ],
        scratch_shapes=[pltpu.VMEM((SUBLANES, LANES), F32)],
        compiler_params=pltpu.CompilerParams(
            dimension_semantics=("arbitrary",), vmem_limit_bytes=VMEM_LIMIT),
        name="outproj",
    )(ga, gb, x2, w_out, norm_w, wr_hi, wr_lo)


def _row_copy(src_ref, src_row, dst_ref, dst_row, sem):
    src = src_ref.at[pl.ds(pl.multiple_of(src_row * ROW_TILE, ROW_TILE), ROW_TILE)]
    dst = dst_ref.at[pl.ds(pl.multiple_of(dst_row * ROW_TILE, ROW_TILE), ROW_TILE)]
    return pltpu.make_async_copy(src, dst, sem)


def _rows_copy(src_ref, dst_ref, dst_row, n_rows, sem):
    src = src_ref.at[pl.ds(0, n_rows * ROW_TILE)]
    dst = dst_ref.at[pl.ds(pl.multiple_of(dst_row * ROW_TILE, ROW_TILE), n_rows * ROW_TILE)]
    return pltpu.make_async_copy(src, dst, sem)


def _zero_fill(zero_scr, xs_ref, lo, hi, sem, wait):
    def go(copy):
        if wait:
            copy.wait()
        else:
            copy.start()

    length = hi - lo
    n_full = lax.shift_right_logical(length, ZERO_ROWS.bit_length() - 1)

    def full(j, carry):
        go(_rows_copy(zero_scr, xs_ref, lo + j * ZERO_ROWS, ZERO_ROWS, sem))
        return carry

    lax.fori_loop(0, n_full, full, 0)
    pos = lo + n_full * ZERO_ROWS
    piece = ZERO_ROWS // 2
    while piece >= 1:
        has = jnp.bitwise_and(length, piece) != 0

        @pl.when(has)
        def _(pos=pos, piece=piece):
            go(_rows_copy(zero_scr, xs_ref, pos, piece, sem))

        pos = pos + jnp.where(has, piece, 0)
        piece //= 2


def _scatter_kernel(seg_ref, d0_ref, d1_ref, h2_ref, xs_ref, zero_scr, sem, zsem, *, tile):
    i = pl.program_id(0)

    def issue(t, carry):
        _row_copy(h2_ref, t, xs_ref, d0_ref[t], sem).start(priority=0)
        _row_copy(h2_ref, t, xs_ref, d1_ref[t], sem).start(priority=1)
        return carry

    lax.fori_loop(0, tile, issue, 0, unroll=DMA_UNROLL)

    @pl.when(i == 0)
    def _():
        zero_scr[...] = jnp.zeros_like(zero_scr)
        for wait in (False, True):
            def per_segment(e, carry, wait=wait):
                _zero_fill(zero_scr, xs_ref, seg_ref[0, e], seg_ref[1, e], zsem, wait)
                return carry

            lax.fori_loop(0, N_EXPERTS + 1, per_segment, 0)

    def drain(t, carry):
        _row_copy(h2_ref, 0, xs_ref, 0, sem).wait()
        _row_copy(h2_ref, 0, xs_ref, 0, sem).wait()
        return carry

    lax.fori_loop(0, tile, drain, 0, unroll=DMA_UNROLL)


def _scatter(seg, dest0, dest1, h2t, n_rows):
    n = dest0.shape[0]
    tile = SCATTER_T
    kern = functools.partial(_scatter_kernel, tile=tile)
    return pl.pallas_call(
        kern,
        grid=(n // tile,),
        in_specs=[
            pl.BlockSpec(memory_space=pltpu.SMEM),
            pl.BlockSpec((tile,), lambda i: (i,), memory_space=pltpu.SMEM),
            pl.BlockSpec((tile,), lambda i: (i,), memory_space=pltpu.SMEM),
            pl.BlockSpec((tile * ROW_TILE, LANES), lambda i: (i, 0)),
        ],
        out_specs=pl.BlockSpec(memory_space=pl.ANY),
        out_shape=jax.ShapeDtypeStruct((n_rows * ROW_TILE, LANES), U32),
        scratch_shapes=[pltpu.VMEM((ZERO_ROWS * ROW_TILE, LANES), U32),
                        pltpu.SemaphoreType.DMA, pltpu.SemaphoreType.DMA],
        compiler_params=pltpu.CompilerParams(
            dimension_semantics=("arbitrary",), vmem_limit_bytes=VMEM_LIMIT),
        name="scatter",
    )(seg, dest0, dest1, h2t)


def _expert_kernel(be_ref, nv_ref, xs_ref, wg_ref, wu_ref, wd_ref, y_ref):
    i = pl.program_id(0)

    @pl.when(i < nv_ref[0])
    def _():
        x = _load_row_tiles(xs_ref).astype(BF16)
        g = _dot(x, wg_ref[0].astype(BF16))
        u = _dot(x, wu_ref[0].astype(BF16))
        hid = (_silu(g) * u).astype(BF16)
        _store_row_tiles(y_ref, _dot(hid, wd_ref[0].astype(BF16)))

    @pl.when(i >= nv_ref[0])
    def _():
        y_ref[...] = jnp.zeros_like(y_ref)


def _experts(block_expert, n_valid, xs, w_gate, w_up, w_down):
    blk = MOE_BLOCK
    n_rows = xs.shape[0] // ROW_TILE
    grid_spec = pltpu.PrefetchScalarGridSpec(
        num_scalar_prefetch=2,
        grid=(n_rows // blk,),
        in_specs=[
            pl.BlockSpec((blk * ROW_TILE, LANES), lambda i, be, nv: (jnp.minimum(i, nv[0] - 1), 0)),
            pl.BlockSpec((1, D_MODEL, D_EXPERT), lambda i, be, nv: (be[i], 0, 0)),
            pl.BlockSpec((1, D_MODEL, D_EXPERT), lambda i, be, nv: (be[i], 0, 0)),
            pl.BlockSpec((1, D_EXPERT, D_MODEL), lambda i, be, nv: (be[i], 0, 0)),
        ],
        out_specs=pl.BlockSpec((blk * ROW_TILE, LANES), lambda i, be, nv: (i, 0)),
    )
    return pl.pallas_call(
        _expert_kernel,
        grid_spec=grid_spec,
        out_shape=jax.ShapeDtypeStruct((n_rows * ROW_TILE, LANES), U32),
        compiler_params=pltpu.CompilerParams(
            dimension_semantics=("arbitrary",), vmem_limit_bytes=VMEM_LIMIT),
        name="experts",
    )(block_expert, n_valid, xs, w_gate, w_up, w_down)


def _combine_kernel(d0_ref, d1_ref, x1_ref, rw_ref, nw_ref, y_ref, out_ref, ya_scr, yb_scr, sems, *, tile):
    half = tile // 2

    def issue(h):
        def body(t, carry):
            _row_copy(y_ref, d0_ref[t], ya_scr, t, sems.at[h]).start()
            _row_copy(y_ref, d1_ref[t], yb_scr, t, sems.at[h]).start()
            return carry
        lax.fori_loop(h * half, (h + 1) * half, body, 0, unroll=DMA_UNROLL)

    def drain(h):
        def body(t, carry):
            _row_copy(y_ref, 0, ya_scr, 0, sems.at[h]).wait()
            _row_copy(y_ref, 0, yb_scr, 0, sems.at[h]).wait()
            return carry
        lax.fori_loop(0, half, body, 0, unroll=DMA_UNROLL)

    def combine(h):
        rows = pl.ds(h * half, half)
        tiles = pl.ds(h * half * ROW_TILE, half * ROW_TILE)
        rw = rw_ref[rows, :]
        moe = rw[:, 0:1] * _load_row_tiles(ya_scr.at[tiles]) + rw[:, 1:2] * _load_row_tiles(yb_scr.at[tiles])
        x2 = x1_ref[rows, :] + moe
        out_ref[rows, :] = x2 * lax.rsqrt(jnp.mean(x2 * x2, axis=-1, keepdims=True) + EPS) * nw_ref[...]

    issue(0)
    issue(1)
    drain(0)
    combine(0)
    drain(1)
    combine(1)


def _combine(dest0, dest1, x1, rw, norm_w, y):
    n = x1.shape[0]
    tile = COMBINE_T
    kern = functools.partial(_combine_kernel, tile=tile)
    return pl.pallas_call(
        kern,
        grid=(n // tile,),
        in_specs=[
            pl.BlockSpec((tile,), lambda i: (i,), memory_space=pltpu.SMEM),
            pl.BlockSpec((tile,), lambda i: (i,), memory_space=pltpu.SMEM),
            pl.BlockSpec((tile, D_MODEL), lambda i: (i, 0)),
            pl.BlockSpec((tile, LANES), lambda i: (i, 0)),
            pl.BlockSpec((1, D_MODEL), lambda i: (0, 0)),
            pl.BlockSpec(memory_space=pl.ANY),
        ],
        out_specs=pl.BlockSpec((tile, D_MODEL), lambda i: (i, 0)),
        out_shape=jax.ShapeDtypeStruct((n, D_MODEL), F32),
        scratch_shapes=[
            pltpu.VMEM((tile * ROW_TILE, LANES), U32), pltpu.VMEM((tile * ROW_TILE, LANES), U32),
            pltpu.SemaphoreType.DMA((2,)),
        ],
        compiler_params=pltpu.CompilerParams(
            dimension_semantics=("arbitrary",), vmem_limit_bytes=VMEM_LIMIT),
        name="combine",
    )(dest0, dest1, x1, rw, norm_w, y)


def _pad_cols(w, width):
    return jnp.pad(w, ((0, 0), (0, width - w.shape[1])))


def _token_mixer_and_moe(x, norm1_w, w_in, w2_f, b_f, w2_b, b_b, gla_norm_w, conv_w, a_log_f, dt_bias_f,
                         a_log_b, dt_bias_b, gdn_norm_w, w_out, norm2_w, w_group, w_router, w_gate, w_up,
                         w_down, out_norm_w):
    batch, seq, d = x.shape
    n = batch * seq
    x2 = x.reshape(n, d)

    w_main = jnp.concatenate([w_in[:, :3072], w_in[:, 3104:7200], w_in[:, 7232:]], axis=1).astype(BF16)
    w_small = _pad_cols(jnp.concatenate([w_in[:, 3072:3104], w_in[:, 7200:7232]], axis=1), LANES)
    ws_hi, ws_lo = _split2(w_small)
    main, small = _inproj(x2, norm1_w.reshape(1, d), w_main, ws_hi, ws_lo)

    w2f_pad = jnp.zeros((LANES, GLA_HEADS * GLA_DK), F32).at[0:GLA_GATE_RANK].set(w2_f)
    w2b_pad = jnp.zeros((LANES, GLA_HEADS * GLA_DK), F32).at[GLA_GATE_RANK:2 * GLA_GATE_RANK].set(w2_b)
    ga = _gla(main, small, w2f_pad, w2b_pad, b_f.reshape(1, -1), b_b.reshape(1, -1),
              gla_norm_w.reshape(1, -1), batch, seq)

    gates = jnp.zeros((SUBLANES, LANES), F32)
    gates = gates.at[0, SMALL_AF:SMALL_BF].set(jnp.concatenate([a_log_f, a_log_b]))
    gates = gates.at[1, SMALL_AF:SMALL_BF].set(jnp.concatenate([dt_bias_f, dt_bias_b]))
    gb = _gdn(main, small, gates, conv_w, gdn_norm_w.reshape(1, -1), batch, seq)

    w_route = _pad_cols(jnp.concatenate([w_router, w_group], axis=1), LANES)
    wr_hi, wr_lo = _split2(w_route)
    x1, h2t, rt, rw, counts = _outproj(ga, gb, x2, w_out.astype(BF16), norm2_w.reshape(1, d), wr_hi, wr_lo)

    blk = MOE_BLOCK
    cnt = counts[0, :N_EXPERTS]
    padded = (cnt + blk - 1) // blk * blk
    ends = jnp.cumsum(padded)
    pstart = (ends - padded).astype(jnp.int32)
    n_blocks = -(-(2 * n + N_EXPERTS * (blk - 1)) // blk)
    n_rows = n_blocks * blk
    block_row = jnp.arange(n_blocks, dtype=jnp.int32) * blk
    block_expert = jnp.minimum(
        jnp.sum((ends[None, :] <= block_row[:, None]).astype(jnp.int32), axis=1), N_EXPERTS - 1)
    n_valid = (ends[-1:] // blk).astype(jnp.int32)
    seg = jnp.stack([jnp.append(pstart + cnt, ends[-1]), jnp.append(ends, n_rows)]).astype(jnp.int32)

    experts = jnp.arange(N_EXPERTS, dtype=jnp.int32)
    seg_start = jnp.sum(jnp.where(rt[0:2, :, None] == experts, pstart, 0), axis=-1)
    dest = seg_start + rt[2:4]
    dest0, dest1 = dest[0], dest[1]

    xs = _scatter(seg, dest0, dest1, h2t, n_rows)
    y = _experts(block_expert, n_valid, xs, w_gate, w_up, w_down)
    out = _combine(dest0, dest1, x1, rw, out_norm_w.reshape(1, d), y)
    return out.reshape(batch, seq, d)


def kernel(x, norm1_w, w_in, gla_gate_w2_fwd, gla_gate_b_fwd, gla_gate_w2_bwd, gla_gate_b_bwd, gla_norm_w,
           gdn_conv_w, gdn_a_log_fwd, gdn_dt_bias_fwd, gdn_a_log_bwd, gdn_dt_bias_bwd, gdn_norm_w, w_out,
           norm2_w, moe_w_group, moe_w_router, moe_w_gate, moe_w_up, moe_w_down, norm_f_w):
    assert norm1_w.shape[0] == 1, "single-layer block"
    return _token_mixer_and_moe(
        x, norm1_w[0], w_in[0], gla_gate_w2_fwd[0], gla_gate_b_fwd[0], gla_gate_w2_bwd[0], gla_gate_b_bwd[0],
        gla_norm_w[0], gdn_conv_w[0], gdn_a_log_fwd[0], gdn_dt_bias_fwd[0], gdn_a_log_bwd[0],
        gdn_dt_bias_bwd[0], gdn_norm_w[0], w_out[0], norm2_w[0], moe_w_group[0], moe_w_router[0],
        moe_w_gate[0], moe_w_up[0], moe_w_down[0], norm_f_w)
```

```python
import functools

import jax
import jax.numpy as jnp
import numpy as np
from jax import lax
from jax.experimental import pallas as pl
from jax.experimental.pallas import tpu as pltpu

F32 = jnp.float32
BF16 = jnp.bfloat16
U32 = jnp.uint32

D_MODEL = 1024
GLA_HEADS = 4
GLA_DK = 128
GLA_DV = 256
GLA_GATE_RANK = 16
GLA_GATE_TAU = 16.0
GLA_CHUNK = 64
GLA_GROUP = 8
GDN_HEADS = 8
GDN_DK = 128
GDN_DV = 128
GDN_CONV = 5
GDN_CHUNK = 128
GDN_PREP_GROUP = 8
N_GROUPS = 4
EXPERTS_PER_GROUP = 8
N_EXPERTS = N_GROUPS * EXPERTS_PER_GROUP
D_EXPERT = 256
EPS = 1e-6

LANES = 128
SUBLANES = 8
VMEM_LIMIT = 48 * 1024 * 1024

COL_GQ, COL_GK, COL_GV, COL_GR = 0, 512, 1024, 2048
COL_DQ, COL_DK, COL_DV, COL_DZ = 3072, 4096, 5120, 6144
COL_MA, COL_MB = 7168, 8192
D_MAIN = 9216
SMALL_AF, SMALL_AB, SMALL_BF, SMALL_BB = 32, 40, 48, 56
ROUTE_GROUP_LANE = 32

MOE_BLOCK = 512
ROW_TILE = D_MODEL // 2 // LANES
HIGH_HALF = np.uint32(0xFFFF0000)
ZERO_ROWS = 128
DMA_UNROLL = 8
INPROJ_TM, INPROJ_TN = 1024, 1024
OUTPROJ_TM = 512
OUTPROJ_SUB = 128
SCATTER_T = 512
COMBINE_T = 512
CONV_ROWS = 256
NEG_INF = float("-inf")


def _dot(a, b):
    return jnp.dot(a, b, preferred_element_type=F32)


def _dot_nt(a, b):
    return lax.dot_general(a, b, (((1,), (1,)), ((), ())), preferred_element_type=F32)


def _dot_tn(a, b):
    return lax.dot_general(a, b, (((0,), (0,)), ((), ())), preferred_element_type=F32)


def _split2(x):
    hi = x.astype(BF16)
    lo = (x - hi.astype(F32)).astype(BF16)
    return hi, lo


def _split3(x):
    hi = x.astype(BF16)
    r = x - hi.astype(F32)
    mid = r.astype(BF16)
    lo = (r - mid.astype(F32)).astype(BF16)
    return hi, mid, lo


def _dot_exact_rhs(x, m_bf16):
    hi, mid, lo = _split3(x)
    return _dot(hi, m_bf16) + _dot(mid, m_bf16) + _dot(lo, m_bf16)


def _dot_exact_lhs(m_bf16, x):
    hi, mid, lo = _split3(x)
    return _dot(m_bf16, hi) + _dot(m_bf16, mid) + _dot(m_bf16, lo)


def _dot_lhs2(m_bf16, x):
    hi, lo = _split2(x)
    return _dot(m_bf16, hi) + _dot(m_bf16, lo)


def _dot_lhs2_wide(m2_bf16, x):
    return _dot(m2_bf16, jnp.concatenate(_split2(x), axis=0))


def _cumsum_rows(x, reverse):
    rows = x.shape[0]
    row = _iota2(x.shape, 0)
    shift = 1
    while shift < rows:
        if reverse:
            x = x + jnp.where(row < rows - shift, pltpu.roll(x, rows - shift, axis=0), 0.0)
        else:
            x = x + jnp.where(row >= shift, pltpu.roll(x, shift, axis=0), 0.0)
        shift *= 2
    return x


def _dot3(a, b):
    ah, al = _split2(a)
    bh, bl = _split2(b)
    return _dot(ah, bh) + _dot(al, bh) + _dot(ah, bl)


def _store_row_tiles(ref, x):
    rows = x.shape[0]
    half = D_MODEL // 2
    hi = lax.bitcast_convert_type(x[:, :half].astype(BF16).astype(F32), U32)
    lo = lax.bitcast_convert_type(x[:, half:].astype(BF16).astype(F32), U32)
    packed = jnp.bitwise_or(jnp.bitwise_and(hi, HIGH_HALF), jnp.right_shift(lo, 16))
    for j in range(ROW_TILE):
        ref[pl.ds(j, rows, stride=ROW_TILE), :] = packed[:, j * LANES:(j + 1) * LANES]


def _load_row_tiles(ref):
    rows = ref.shape[0] // ROW_TILE
    packed = jnp.concatenate([ref[pl.ds(j, rows, stride=ROW_TILE), :] for j in range(ROW_TILE)], axis=1)
    hi = lax.bitcast_convert_type(jnp.bitwise_and(packed, HIGH_HALF), F32)
    lo = lax.bitcast_convert_type(jnp.left_shift(packed, 16), F32)
    return jnp.concatenate([hi, lo], axis=1)


def _each(fn, *lists):
    return [fn(*args) for args in zip(*lists)]


def _sigmoid(x):
    return 1.0 / (1.0 + jnp.exp(-x))


def _silu(x):
    return x * _sigmoid(x)


def _softplus(x):
    return jnp.maximum(x, 0.0) + jnp.log(1.0 + jnp.exp(-jnp.abs(x)))


def _log_sigmoid(x):
    return jnp.minimum(x, 0.0) - jnp.log(1.0 + jnp.exp(-jnp.abs(x)))


def _iota2(shape, dim):
    return lax.broadcasted_iota(jnp.int32, shape, dim)


def _inproj_kernel(x_ref, nw_ref, w_ref, wsh_ref, wsl_ref, main_ref, small_ref, h_scr):
    @pl.when(pl.program_id(1) == 0)
    def _():
        x = x_ref[...]
        h = x * lax.rsqrt(jnp.mean(x * x, axis=-1, keepdims=True) + EPS) * nw_ref[...]
        hh, hl = _split2(h)
        h_scr[...] = hh
        small_ref[...] = _dot(hh, wsh_ref[...]) + _dot(hl, wsh_ref[...]) + _dot(hh, wsl_ref[...])

    main_ref[...] = _dot(h_scr[...], w_ref[...]).astype(BF16)


def _inproj(x2, norm_w, w_main, ws_hi, ws_lo):
    n = x2.shape[0]
    tm, tn = INPROJ_TM, INPROJ_TN
    return pl.pallas_call(
        _inproj_kernel,
        grid=(n // tm, D_MAIN // tn),
        in_specs=[
            pl.BlockSpec((tm, D_MODEL), lambda i, j: (i, 0)),
            pl.BlockSpec((1, D_MODEL), lambda i, j: (0, 0)),
            pl.BlockSpec((D_MODEL, tn), lambda i, j: (0, j)),
            pl.BlockSpec((D_MODEL, LANES), lambda i, j: (0, 0)),
            pl.BlockSpec((D_MODEL, LANES), lambda i, j: (0, 0)),
        ],
        out_specs=[
            pl.BlockSpec((tm, tn), lambda i, j: (i, j)),
            pl.BlockSpec((tm, LANES), lambda i, j: (i, 0)),
        ],
        out_shape=[
            jax.ShapeDtypeStruct((n, D_MAIN), BF16),
            jax.ShapeDtypeStruct((n, LANES), F32),
        ],
        scratch_shapes=[pltpu.VMEM((tm, D_MODEL), BF16)],
        compiler_params=pltpu.CompilerParams(
            dimension_semantics=("arbitrary", "arbitrary"), vmem_limit_bytes=VMEM_LIMIT),
        name="inproj",
    )(x2, norm_w, w_main, ws_hi, ws_lo)


def _gla_kernel(q_ref, k_ref, v_ref, gr_ref, ma_ref, small_ref, w2f_ref, w2b_ref, bf_ref, bb_ref,
                nw_ref, out_ref, laf_scr, lab_scr, o_scr, stf_scr, stb_scr, *, seq, chunk):
    c = chunk
    n = seq // c
    scale = GLA_DK ** -0.5

    sm = small_ref[...].astype(BF16)
    laf_scr[...] = _log_sigmoid(_dot(sm, w2f_ref[...].astype(BF16)) + bf_ref[...]) * (1.0 / GLA_GATE_TAU)
    lab_scr[...] = _log_sigmoid(_dot(sm, w2b_ref[...].astype(BF16)) + bb_ref[...]) * (1.0 / GLA_GATE_TAU)
    stf_scr[...] = jnp.zeros_like(stf_scr)
    stb_scr[...] = jnp.zeros_like(stb_scr)

    row = _iota2((c, c), 0)
    col = _iota2((c, c), 1)
    low = row >= col
    upp = row <= col
    low_m = jnp.concatenate([jnp.where(low, 1.0, 0.0).astype(BF16)] * 2, axis=1)
    upp_m = jnp.concatenate([jnp.where(upp, 1.0, 0.0).astype(BF16)] * 2, axis=1)

    g = GLA_GROUP

    def finish(rows, o):
        y = o * lax.rsqrt(jnp.mean(o * o, axis=-1, keepdims=True) + EPS) * nw_ref[...]
        y = y * _silu(gr_ref[rows, :].astype(F32))
        y = y * _sigmoid(ma_ref[rows, :].astype(F32))
        out_ref[rows, :] = y.astype(BF16)

    def group(gi, second_touch):
        ids = [gi * g + j for j in range(g)] + [n - 1 - gi * g - j for j in range(g)]
        rows = [pl.ds(pl.multiple_of(i * c, c), c) for i in ids]
        la = [laf_scr[r, :] for r in rows[:g]] + [lab_scr[r, :] for r in rows[g:]]
        csum = [low_m] * g + [upp_m] * g
        mask = [low] * g + [upp] * g
        tot_row = [c - 1] * g + [0] * g
        qf = [q_ref[r, :].astype(F32) * scale for r in rows]
        kf = [k_ref[r, :].astype(F32) for r in rows]
        vc = [v_ref[r, :] for r in rows]

        cum = _each(_dot_lhs2_wide, csum, la)
        tot = _each(lambda x, r: x[r:r + 1, :], cum, tot_row)
        q_dec = _each(lambda q, x: (q * jnp.exp(x)).astype(BF16), qf, cum)
        k_inv = _each(lambda k, x: (k * jnp.exp(-x)).astype(BF16), kf, cum)
        k_tail = _each(lambda k, t, x: (k * jnp.exp(t - x)).astype(BF16), kf, tot, cum)
        s = _each(lambda m, q, k: jnp.where(m, _dot_nt(q, k), 0.0).astype(BF16), mask, q_dec, k_inv)
        o = _each(_dot, s, vc)
        kv = _each(_dot_tn, vc, k_tail)
        dec = _each(jnp.exp, tot)

        for st_scr, probs in ((stf_scr, range(g)), (stb_scr, range(g, 2 * g))):
            st = st_scr[...]
            for p in probs:
                o[p] = o[p] + _dot_nt(q_dec[p], st.astype(BF16))
                st = dec[p] * st + kv[p]
            st_scr[...] = st

        for r, o_p in zip(rows, o):
            if second_touch:
                finish(r, o_scr[r, :] + o_p)
            else:
                o_scr[r, :] = o_p

    def first_half(gi, carry):
        group(gi, False)
        return carry

    def second_half(gi, carry):
        group(gi, True)
        return carry

    n_groups = n // g
    lax.fori_loop(0, n_groups // 2, first_half, 0)
    lax.fori_loop(n_groups // 2, n_groups, second_half, 0)


def _gla(main, small, w2f_pad, w2b_pad, b_f, b_b, norm_w, batch, seq):
    n = batch * seq
    h = GLA_HEADS
    kern = functools.partial(_gla_kernel, seq=seq, chunk=GLA_CHUNK)
    qk_blk = lambda off: pl.BlockSpec((seq, GLA_DK), lambda b, hh, off=off: (b, off // GLA_DK + hh))
    v_blk = lambda off: pl.BlockSpec((seq, GLA_DV), lambda b, hh, off=off: (b, off // GLA_DV + hh))
    return pl.pallas_call(
        kern,
        grid=(batch, h),
        in_specs=[
            qk_blk(COL_GQ), qk_blk(COL_GK), v_blk(COL_GV), v_blk(COL_GR), v_blk(COL_MA),
            pl.BlockSpec((seq, LANES), lambda b, hh: (b, 0)),
            pl.BlockSpec((LANES, GLA_DK), lambda b, hh: (0, hh)),
            pl.BlockSpec((LANES, GLA_DK), lambda b, hh: (0, hh)),
            pl.BlockSpec((1, GLA_DK), lambda b, hh: (0, hh)),
            pl.BlockSpec((1, GLA_DK), lambda b, hh: (0, hh)),
            pl.BlockSpec((1, GLA_DV), lambda b, hh: (0, 0)),
        ],
        out_specs=pl.BlockSpec((seq, GLA_DV), lambda b, hh: (b, hh)),
        out_shape=jax.ShapeDtypeStruct((n, D_MODEL), BF16),
        scratch_shapes=[
            pltpu.VMEM((seq, GLA_DK), F32), pltpu.VMEM((seq, GLA_DK), F32),
            pltpu.VMEM((seq, GLA_DV), F32),
            pltpu.VMEM((GLA_DV, GLA_DK), F32), pltpu.VMEM((GLA_DV, GLA_DK), F32),
        ],
        compiler_params=pltpu.CompilerParams(
            dimension_semantics=("arbitrary", "arbitrary"), vmem_limit_bytes=VMEM_LIMIT),
        name="gla",
    )(main, main, main, main, main, small, w2f_pad, w2b_pad, b_f, b_b, norm_w)


TRI_BLOCK = 16


def _mm(a, b):
    return _dot(a.astype(BF16), b.astype(BF16))


def _nilpotent_inverse(a_list, eye, index, tick):
    t_list = _each(lambda a: eye - a, a_list)
    p_list = a_list
    power = 2
    while power < index:
        p_list = _each(lambda p: _mm(p, p), p_list)
        tick()
        t_list = _each(lambda t, p: t + _mm(t, p), t_list, p_list)
        tick()
        power *= 2
    return t_list


def _tri_inverse(a_list, eye, diag_blocks, chunk, tick):
    ad_list = _each(lambda a: jnp.where(diag_blocks, a, 0.0), a_list)
    ao_list = _each(lambda a: jnp.where(diag_blocks, 0.0, a), a_list)
    d_list = _nilpotent_inverse(ad_list, eye, TRI_BLOCK, tick)
    n_list = _each(_mm, d_list, ao_list)
    tick()
    t_list = _nilpotent_inverse(n_list, eye, chunk // TRI_BLOCK, tick)
    out = _each(_mm, t_list, d_list)
    tick()
    return out


def _gdn_kernel(gate_ref, q_ref, k_ref, v_ref, z_ref, mb_ref, small_ref, cwq_ref, cwk_ref, cwv_ref,
                nw_ref, out_ref, pad_scr, qs_scr, ks_scr, vs_scr, gf_scr, gb_scr, btf_scr, btb_scr,
                o_scr, nmat_scr, bmat_scr, qp_scr, cd_scr, sf_scr, sb_scr, *, seq, chunk, n_heads_total):
    c = chunk
    n = seq // c
    step = pl.program_id(0)
    hh = lax.rem(jnp.minimum(step, n_heads_total - 1), GDN_HEADS)
    cur = lax.rem(step, 2)
    prev = 1 - cur
    scale = GDN_DK ** -0.5

    @pl.when(step == 0)
    def _():
        nmat_scr[...] = jnp.zeros_like(nmat_scr)
        bmat_scr[...] = jnp.zeros_like(bmat_scr)
        qp_scr[...] = jnp.zeros_like(qp_scr)
        cd_scr[...] = jnp.zeros_like(cd_scr)
        o_scr[...] = jnp.zeros_like(o_scr)

    zeros8 = jnp.zeros((SUBLANES, LANES), F32)
    pad_scr[0:SUBLANES, :] = zeros8
    pad_scr[seq + SUBLANES:seq + 2 * SUBLANES, :] = zeros8
    half = GDN_CONV // 2

    def conv_into(src_ref, cw_ref, dst_ref, normalise, mult):
        pad_scr[SUBLANES:seq + SUBLANES, :] = src_ref[...].astype(F32)
        w = cw_ref[...]

        def body(i, carry):
            r0 = pl.multiple_of(i * CONV_ROWS, CONV_ROWS)
            acc = jnp.zeros((CONV_ROWS, LANES), F32)
            for j in range(GDN_CONV):
                tap = pad_scr[pl.ds(r0 + (SUBLANES - half + j), CONV_ROWS), :]
                acc = acc + tap * w[j:j + 1, :]
            y = _silu(acc)
            if normalise:
                y = y * lax.rsqrt(jnp.sum(y * y, axis=-1, keepdims=True) + EPS) * mult
            dst_ref[pl.ds(r0, CONV_ROWS), :] = y.astype(BF16)
            return carry

        lax.fori_loop(0, seq // CONV_ROWS, body, 0, unroll=4)

    conv_into(q_ref, cwq_ref, qs_scr, True, scale)
    conv_into(k_ref, cwk_ref, ks_scr, True, 1.0)
    conv_into(v_ref, cwv_ref, vs_scr, False, 1.0)

    sm = small_ref[...]
    lane = _iota2(sm.shape, 1)
    log_decay = -jnp.exp(gate_ref[0:1, :]) * _softplus(sm + gate_ref[1:2, :])
    gate_vals = jnp.where(lane < SMALL_BF, log_decay, _sigmoid(sm))
    gate_hl = jnp.concatenate(_split2(gate_vals), axis=1)
    sel_lane = jnp.bitwise_and(_iota2((2 * LANES, 4 * LANES), 0), LANES - 1)
    sel_gate = jnp.right_shift(_iota2((2 * LANES, 4 * LANES), 1), LANES.bit_length() - 1)
    sel = jnp.where(sel_lane == SMALL_AF + GDN_HEADS * sel_gate + hh, 1.0, 0.0).astype(BF16)
    spread = _dot(gate_hl, sel)
    gf_scr[...] = spread[:, 0 * LANES:1 * LANES]
    gb_scr[...] = spread[:, 1 * LANES:2 * LANES]
    btf_scr[...] = spread[:, 2 * LANES:3 * LANES]
    btb_scr[...] = spread[:, 3 * LANES:4 * LANES]
    sf_scr[...] = jnp.zeros_like(sf_scr)
    sb_scr[...] = jnp.zeros_like(sb_scr)

    row = _iota2((c, c), 0)
    col = _iota2((c, c), 1)
    eye = jnp.where(row == col, 1.0, 0.0).astype(F32)
    low, slow = row >= col, row > col
    upp, supp = row <= col, row < col
    assert c == LANES, "the decay matrix is formed from a [c, 128] lane-broadcast column and its transpose"

    tri_shift = TRI_BLOCK.bit_length() - 1
    diag_blocks = jnp.right_shift(row, tri_shift) == jnp.right_shift(col, tri_shift)

    cur_slot, prev_slot = cur * (2 * n), prev * (2 * n)
    cur_row, prev_row = cur * seq, prev * seq

    def scan_step(i):
        j = n - 1 - i
        slots = [prev_slot + i, prev_slot + n + j]
        rows = [pl.ds(pl.multiple_of(prev_row + i * c, c), c), pl.ds(pl.multiple_of(prev_row + j * c, c), c)]
        states = [sf_scr, sb_scr]
        s = [ref[...] for ref in states]
        s_b = _each(lambda x: x.astype(BF16), s)
        ns = _each(lambda sl, x: _dot(nmat_scr[sl], x), slots, s_b)
        qs = _each(lambda sl, x: _dot(qp_scr[sl], x), slots, s_b)
        for ref, sl, s_, ns_ in zip(states, slots, s, ns):
            ref[...] = cd_scr[sl][0:1, :] * s_ + (bmat_scr[sl] - ns_)
        for r, q in zip(rows, qs):
            o_scr[r, :] += q

    n_groups = n // GDN_PREP_GROUP
    scans_per_group = n // n_groups

    def prep_group(gi, carry):
        pending = [gi * scans_per_group + j for j in range(scans_per_group)]

        def tick():
            if pending:
                scan_step(pending.pop(0))

        chunk_ids = [gi * GDN_PREP_GROUP + j for j in range(GDN_PREP_GROUP)]
        rows = [pl.ds(pl.multiple_of(i * c, c), c) for i in chunk_ids]
        qc = [qs_scr[r, :] for r in rows]
        kc = [ks_scr[r, :] for r in rows]
        vc = [vs_scr[r, :] for r in rows]
        gl = [ref[r, :] for r in rows for ref in (gf_scr, gb_scr)]
        bt = [ref[r, :] for r in rows for ref in (btf_scr, btb_scr)]
        reverse = [False, True] * GDN_PREP_GROUP
        incl = [low, upp] * GDN_PREP_GROUP
        strict = [slow, supp] * GDN_PREP_GROUP
        tot_row = [c - 1, 0] * GDN_PREP_GROUP

        def both(per_chunk):
            return [x for x in per_chunk for _ in range(2)]

        kk = both(_each(_dot_nt, kc, kc))
        qk = both(_each(_dot_nt, qc, kc))
        tick()
        qf = both(_each(lambda x: x.astype(F32), qc))
        kf = both(_each(lambda x: x.astype(F32), kc))
        vf = both(_each(lambda x: x.astype(F32), vc))

        gc = _each(_cumsum_rows, gl, reverse)
        tot = _each(lambda g, r: g[r:r + 1, :], gc, tot_row)
        e = _each(lambda g, inc: jnp.exp(jnp.where(inc, g - jnp.transpose(g), 0.0)), gc, incl)
        a = _each(lambda kk_, b, e_, st: kk_ * b[:, :c] * jnp.where(st, e_, 0.0), kk, bt, e, strict)
        t_inv = _tri_inverse(a, eye, diag_blocks, c, tick)
        egc = _each(jnp.exp, gc)
        wu = _each(lambda t, k, v, b, eg: _mm(t, jnp.concatenate([k * b * eg, v * b], axis=1)).astype(BF16),
                   t_inv, kf, vf, bt, egc)
        tick()
        attn = _each(lambda qk_, e_, inc: (qk_ * jnp.where(inc, e_, 0.0)).astype(BF16), qk, e, incl)
        k_tail = _each(lambda k, t, g: (k * jnp.exp(t - g)).astype(BF16), kf, tot, gc)
        kwu = _each(_dot_tn, k_tail, wu)
        tick()
        awu = _each(_dot, attn, wu)
        while pending:
            tick()

        for p in range(2 * GDN_PREP_GROUP):
            slot = cur_slot + chunk_ids[p // 2] + (p % 2) * n
            nmat_scr[slot] = kwu[p][:, :GDN_DK].astype(BF16)
            bmat_scr[slot] = kwu[p][:, GDN_DK:]
            qp_scr[slot] = (qf[p] * egc[p] - awu[p][:, :GDN_DK]).astype(BF16)
            cd_scr[slot] = jnp.broadcast_to(jnp.exp(tot[p]), (SUBLANES, LANES))
        for j, i in enumerate(chunk_ids):
            r = pl.ds(pl.multiple_of(cur_row + i * c, c), c)
            o_scr[r, :] = awu[2 * j][:, GDN_DK:] + awu[2 * j + 1][:, GDN_DK:]
        return carry

    lax.fori_loop(0, n_groups, prep_group, 0)

    def finish(i, carry):
        r0 = pl.multiple_of(i * CONV_ROWS, CONV_ROWS)
        o = o_scr[pl.ds(pl.multiple_of(prev_row + r0, CONV_ROWS), CONV_ROWS), :]
        y = o * lax.rsqrt(jnp.mean(o * o, axis=-1, keepdims=True) + EPS) * nw_ref[...]
        y = y * _silu(z_ref[pl.ds(r0, CONV_ROWS), :].astype(F32))
        y = y * _sigmoid(mb_ref[pl.ds(r0, CONV_ROWS), :].astype(F32))
        out_ref[pl.ds(r0, CONV_ROWS), :] = y.astype(BF16)
        return carry

    lax.fori_loop(0, seq // CONV_ROWS, finish, 0, unroll=4)


def _gdn(main, small, gates, conv_w, norm_w, batch, seq):
    n = batch * seq
    total = batch * GDN_HEADS
    kern = functools.partial(_gdn_kernel, seq=seq, chunk=GDN_CHUNK, n_heads_total=total)
    n_chunks = seq // GDN_CHUNK

    def head_of(step):
        idx = jnp.minimum(step, total - 1)
        return idx // GDN_HEADS, idx % GDN_HEADS

    def prev_head_of(step):
        idx = jnp.maximum(step - 1, 0)
        return idx // GDN_HEADS, idx % GDN_HEADS

    def blk(off, which):
        def index(s):
            b, hh = which(s)
            return b, off // LANES + hh
        return pl.BlockSpec((seq, LANES), index)

    def cw(part):
        return pl.BlockSpec((GDN_CONV, LANES), lambda s: (0, part * GDN_HEADS + head_of(s)[1]))

    seq_f32 = lambda: pltpu.VMEM((seq, LANES), F32)
    seq_bf16 = lambda: pltpu.VMEM((seq, LANES), BF16)
    return pl.pallas_call(
        kern,
        grid=(total + 1,),
        in_specs=[
            pl.BlockSpec((SUBLANES, LANES), lambda s: (0, 0)),
            blk(COL_DQ, head_of), blk(COL_DK, head_of), blk(COL_DV, head_of),
            blk(COL_DZ, prev_head_of), blk(COL_MB, prev_head_of),
            pl.BlockSpec((seq, LANES), lambda s: (head_of(s)[0], 0)),
            cw(0), cw(1), cw(2),
            pl.BlockSpec((1, GDN_DV), lambda s: (0, 0)),
        ],
        out_specs=pl.BlockSpec((seq, GDN_DV), lambda s: prev_head_of(s)),
        out_shape=jax.ShapeDtypeStruct((n, D_MODEL), BF16),
        scratch_shapes=[
            pltpu.VMEM((seq + 2 * SUBLANES, LANES), F32),
            seq_bf16(), seq_bf16(), seq_bf16(),
            seq_f32(), seq_f32(), seq_f32(), seq_f32(),
            pltpu.VMEM((2 * seq, LANES), F32),
            pltpu.VMEM((4 * n_chunks, GDN_DK, GDN_DK), BF16),
            pltpu.VMEM((4 * n_chunks, GDN_DK, GDN_DV), F32),
            pltpu.VMEM((4 * n_chunks, GDN_CHUNK, GDN_DK), BF16),
            pltpu.VMEM((4 * n_chunks, SUBLANES, LANES), F32),
            pltpu.VMEM((GDN_DK, GDN_DV), F32), pltpu.VMEM((GDN_DK, GDN_DV), F32),
        ],
        compiler_params=pltpu.CompilerParams(
            dimension_semantics=("arbitrary",), vmem_limit_bytes=VMEM_LIMIT),
        name="gdn",
    )(gates, main, main, main, main, main, small, conv_w, conv_w, conv_w, norm_w)


def _outproj_kernel(ga_ref, gb_ref, x_ref, wo_ref, nw_ref, wrh_ref, wrl_ref,
                    x1_ref, h2_ref, ri_ref, rw_ref, cnt_ref, carry_scr, *, tm):
    @pl.when(pl.program_id(0) == 0)
    def _():
        carry_scr[...] = jnp.zeros_like(carry_scr)

    sub = OUTPROJ_SUB
    subs = [pl.ds(j * sub, sub) for j in range(tm // sub)]
    lane_i = _iota2((sub, LANES), 1)
    lane = lane_i.astype(F32)

    mixed = [(ga_ref[r, :].astype(F32) + gb_ref[r, :].astype(F32)).astype(BF16) for r in subs]
    x1 = _each(lambda r, m: x_ref[r, :] + _dot(m, wo_ref[...]), subs, mixed)
    for r, v in zip(subs, x1):
        x1_ref[r, :] = v
    h2 = _each(lambda v: v * lax.rsqrt(jnp.mean(v * v, axis=-1, keepdims=True) + EPS) * nw_ref[...], x1)
    for j, v in enumerate(h2):
        _store_row_tiles(h2_ref.at[pl.ds(j * sub * ROW_TILE, sub * ROW_TILE)], v)
    hl = _each(_split2, h2)
    lg = _each(lambda p: _dot(p[0], wrh_ref[...]) + _dot(p[1], wrh_ref[...]) + _dot(p[0], wrl_ref[...]), hl)

    def row_max(vals):
        return _each(lambda v: jnp.max(v, axis=-1, keepdims=True), vals)

    def first_lane_of(vals, maxima):
        return _each(lambda v, m: jnp.min(jnp.where(v == m, lane, float(LANES)), axis=-1, keepdims=True),
                     vals, maxima)

    is_g = (lane_i >= ROUTE_GROUP_LANE) & (lane_i < ROUTE_GROUP_LANE + N_GROUPS)
    gl = _each(lambda v: jnp.where(is_g, v, NEG_INF), lg)
    gmax = row_max(gl)
    gidx = _each(lambda i: i - float(ROUTE_GROUP_LANE), first_lane_of(gl, gmax))
    gsum = _each(lambda v, m: jnp.sum(jnp.where(is_g, jnp.exp(v - m), 0.0), axis=-1, keepdims=True), lg, gmax)
    lane_group = jnp.right_shift(lane_i, EXPERTS_PER_GROUP.bit_length() - 1).astype(F32)
    el = _each(lambda v, g: jnp.where((lane_i < N_EXPERTS) & (lane_group == g), v, NEG_INF), lg, gidx)
    m1 = row_max(el)
    e0 = first_lane_of(el, m1)
    el2 = _each(lambda v, i: jnp.where(lane == i, NEG_INF, v), el, e0)
    m2 = row_max(el2)
    e1 = first_lane_of(el2, m2)
    ratio = _each(lambda a, b: jnp.exp(b - a), m1, m2)
    w0 = _each(lambda s_, r: 1.0 / (s_ * (1.0 + r)), gsum, ratio)
    w1 = _each(lambda w, r: w * r, w0, ratio)

    pick0 = _each(lambda i: lane == i, e0)
    pick1 = _each(lambda i: lane == i, e1)
    onehot = jnp.concatenate(_each(lambda p, q: jnp.where(p | q, 1.0, 0.0), pick0, pick1), axis=0)
    trow = _iota2((tm, tm), 0)
    tcol = _iota2((tm, tm), 1)
    before = jnp.where(trow > tcol, 1.0, 0.0).astype(BF16)
    cnt = _dot(before, onehot.astype(BF16)) + carry_scr[0:1, :]
    cnts = [cnt[j * sub:(j + 1) * sub, :] for j in range(tm // sub)]
    rank0 = _each(lambda p, c_: jnp.sum(jnp.where(p, c_, 0.0), axis=-1, keepdims=True), pick0, cnts)
    rank1 = _each(lambda p, c_: jnp.sum(jnp.where(p, c_, 0.0), axis=-1, keepdims=True), pick1, cnts)
    total = carry_scr[0:1, :] + jnp.sum(onehot, axis=0, keepdims=True)
    carry_scr[...] = jnp.broadcast_to(total, carry_scr.shape)
    cnt_ref[...] = jnp.broadcast_to(total, cnt_ref.shape).astype(jnp.int32)

    for j, r in enumerate(subs):
        ri = jnp.where(lane_i == 0, e0[j], jnp.where(lane_i == 1, e1[j], 0.0))
        ri = jnp.where(lane_i == 2, rank0[j], jnp.where(lane_i == 3, rank1[j], ri))
        ri_ref[:, r] = jnp.transpose(ri)[0:SUBLANES, :].astype(jnp.int32)
        rw_ref[r, :] = jnp.where(lane_i == 0, w0[j], jnp.where(lane_i == 1, w1[j], 0.0))


def _outproj(ga, gb, x2, w_out, norm_w, wr_hi, wr_lo):
    n = x2.shape[0]
    tm = OUTPROJ_TM
    kern = functools.partial(_outproj_kernel, tm=tm)
    row_blk = lambda w: pl.BlockSpec((tm, w), lambda i: (i, 0))
    const = lambda shape: pl.BlockSpec(shape, lambda i: (0, 0))
    return pl.pallas_call(
        kern,
        grid=(n // tm,),
        in_specs=[
            row_blk(D_MODEL), row_blk(D_MODEL), row_blk(D_MODEL),
            const((D_MODEL, D_MODEL)), const((1, D_MODEL)),
            const((D_MODEL, LANES)), const((D_MODEL, LANES)),
        ],
        out_specs=[row_blk(D_MODEL),
                   pl.BlockSpec((tm * ROW_TILE, LANES), lambda i: (i, 0)),
                   pl.BlockSpec((SUBLANES, tm), lambda i: (0, i)),
                   row_blk(LANES),
                   const((SUBLANES, LANES))],
        out_shape=[
            jax.ShapeDtypeStruct((n, D_MODEL), F32),
            jax.ShapeDtypeStruct((n * ROW_TILE, LANES), U32),
            jax.ShapeDtypeStruct((SUBLANES, n), jnp.int32),
            jax.ShapeDtypeStruct((n, LANES), F32),
            jax.ShapeDtypeStruct((SUBLANES, LANES), jnp.int32),
        ],
        scratch_shapes=[pltpu.VMEM((SUBLANES, LANES), F32)],
        compiler_params=pltpu.CompilerParams(
            dimension_semantics=("arbitrary",), vmem_limit_bytes=VMEM_LIMIT),
        name="outproj",
    )(ga, gb, x2, w_out, norm_w, wr_hi, wr_lo)


def _row_copy(src_ref, src_row, dst_ref, dst_row, sem):
    src = src_ref.at[pl.ds(pl.multiple_of(src_row * ROW_TILE, ROW_TILE), ROW_TILE)]
    dst = dst_ref.at[pl.ds(pl.multiple_of(dst_row * ROW_TILE, ROW_TILE), ROW_TILE)]
    return pltpu.make_async_copy(src, dst, sem)


def _rows_copy(src_ref, dst_ref, dst_row, n_rows, sem):
    src = src_ref.at[pl.ds(0, n_rows * ROW_TILE)]
    dst = dst_ref.at[pl.ds(pl.multiple_of(dst_row * ROW_TILE, ROW_TILE), n_rows * ROW_TILE)]
    return pltpu.make_async_copy(src, dst, sem)


def _zero_fill(zero_scr, xs_ref, lo, hi, sem, wait):
    def go(copy):
        if wait:
            copy.wait()
        else:
            copy.start()

    length = hi - lo
    n_full = lax.shift_right_logical(length, ZERO_ROWS.bit_length() - 1)

    def full(j, carry):
        go(_rows_copy(zero_scr, xs_ref, lo + j * ZERO_ROWS, ZERO_ROWS, sem))
        return carry

    lax.fori_loop(0, n_full, full, 0)
    pos = lo + n_full * ZERO_ROWS
    piece = ZERO_ROWS // 2
    while piece >= 1:
        has = jnp.bitwise_and(length, piece) != 0

        @pl.when(has)
        def _(pos=pos, piece=piece):
            go(_rows_copy(zero_scr, xs_ref, pos, piece, sem))

        pos = pos + jnp.where(has, piece, 0)
        piece //= 2


def _scatter_kernel(seg_ref, d0_ref, d1_ref, h2_ref, xs_ref, zero_scr, sem, zsem, *, tile):
    i = pl.program_id(0)

    def issue(t, carry):
        _row_copy(h2_ref, t, xs_ref, d0_ref[t], sem).start(priority=0)
        _row_copy(h2_ref, t, xs_ref, d1_ref[t], sem).start(priority=1)
        return carry

    lax.fori_loop(0, tile, issue, 0, unroll=DMA_UNROLL)

    @pl.when(i == 0)
    def _():
        zero_scr[...] = jnp.zeros_like(zero_scr)
        for wait in (False, True):
            def per_segment(e, carry, wait=wait):
                _zero_fill(zero_scr, xs_ref, seg_ref[0, e], seg_ref[1, e], zsem, wait)
                return carry

            lax.fori_loop(0, N_EXPERTS + 1, per_segment, 0)

    def drain(t, carry):
        _row_copy(h2_ref, 0, xs_ref, 0, sem).wait()
        _row_copy(h2_ref, 0, xs_ref, 0, sem).wait()
        return carry

    lax.fori_loop(0, tile, drain, 0, unroll=DMA_UNROLL)


def _scatter(seg, dest0, dest1, h2t, n_rows):
    n = dest0.shape[0]
    tile = SCATTER_T
    kern = functools.partial(_scatter_kernel, tile=tile)
    return pl.pallas_call(
        kern,
        grid=(n // tile,),
        in_specs=[
            pl.BlockSpec(memory_space=pltpu.SMEM),
            pl.BlockSpec((tile,), lambda i: (i,), memory_space=pltpu.SMEM),
            pl.BlockSpec((tile,), lambda i: (i,), memory_space=pltpu.SMEM),
            pl.BlockSpec((tile * ROW_TILE, LANES), lambda i: (i, 0)),
        ],
        out_specs=pl.BlockSpec(memory_space=pl.ANY),
        out_shape=jax.ShapeDtypeStruct((n_rows * ROW_TILE, LANES), U32),
        scratch_shapes=[pltpu.VMEM((ZERO_ROWS * ROW_TILE, LANES), U32),
                        pltpu.SemaphoreType.DMA, pltpu.SemaphoreType.DMA],
        compiler_params=pltpu.CompilerParams(
            dimension_semantics=("arbitrary",), vmem_limit_bytes=VMEM_LIMIT),
        name="scatter",
    )(seg, dest0, dest1, h2t)


def _expert_kernel(be_ref, nv_ref, xs_ref, wg_ref, wu_ref, wd_ref, y_ref):
    i = pl.program_id(0)

    @pl.when(i < nv_ref[0])
    def _():
        x = _load_row_tiles(xs_ref).astype(BF16)
        g = _dot(x, wg_ref[0].astype(BF16))
        u = _dot(x, wu_ref[0].astype(BF16))
        hid = (_silu(g) * u).astype(BF16)
        _store_row_tiles(y_ref, _dot(hid, wd_ref[0].astype(BF16)))

    @pl.when(i >= nv_ref[0])
    def _():
        y_ref[...] = jnp.zeros_like(y_ref)


def _experts(block_expert, n_valid, xs, w_gate, w_up, w_down):
    blk = MOE_BLOCK
    n_rows = xs.shape[0] // ROW_TILE
    grid_spec = pltpu.PrefetchScalarGridSpec(
        num_scalar_prefetch=2,
        grid=(n_rows // blk,),
        in_specs=[
            pl.BlockSpec((blk * ROW_TILE, LANES), lambda i, be, nv: (jnp.minimum(i, nv[0] - 1), 0)),
            pl.BlockSpec((1, D_MODEL, D_EXPERT), lambda i, be, nv: (be[i], 0, 0)),
            pl.BlockSpec((1, D_MODEL, D_EXPERT), lambda i, be, nv: (be[i], 0, 0)),
            pl.BlockSpec((1, D_EXPERT, D_MODEL), lambda i, be, nv: (be[i], 0, 0)),
        ],
        out_specs=pl.BlockSpec((blk * ROW_TILE, LANES), lambda i, be, nv: (i, 0)),
    )
    return pl.pallas_call(
        _expert_kernel,
        grid_spec=grid_spec,
        out_shape=jax.ShapeDtypeStruct((n_rows * ROW_TILE, LANES), U32),
        compiler_params=pltpu.CompilerParams(
            dimension_semantics=("arbitrary",), vmem_limit_bytes=VMEM_LIMIT),
        name="experts",
    )(block_expert, n_valid, xs, w_gate, w_up, w_down)


def _combine_kernel(d0_ref, d1_ref, x1_ref, rw_ref, nw_ref, y_ref, out_ref, ya_scr, yb_scr, sems, *, tile):
    half = tile // 2

    def issue(h):
        def body(t, carry):
            _row_copy(y_ref, d0_ref[t], ya_scr, t, sems.at[h]).start(priority=0)
            _row_copy(y_ref, d1_ref[t], yb_scr, t, sems.at[h]).start(priority=1)
            return carry
        lax.fori_loop(h * half, (h + 1) * half, body, 0, unroll=DMA_UNROLL)

    def drain(h):
        def body(t, carry):
            _row_copy(y_ref, 0, ya_scr, 0, sems.at[h]).wait()
            _row_copy(y_ref, 0, yb_scr, 0, sems.at[h]).wait()
            return carry
        lax.fori_loop(0, half, body, 0, unroll=DMA_UNROLL)

    def combine(h):
        rows = pl.ds(h * half, half)
        tiles = pl.ds(h * half * ROW_TILE, half * ROW_TILE)
        rw = rw_ref[rows, :]
        moe = rw[:, 0:1] * _load_row_tiles(ya_scr.at[tiles]) + rw[:, 1:2] * _load_row_tiles(yb_scr.at[tiles])
        x2 = x1_ref[rows, :] + moe
        out_ref[rows, :] = x2 * lax.rsqrt(jnp.mean(x2 * x2, axis=-1, keepdims=True) + EPS) * nw_ref[...]

    issue(0)
    issue(1)
    drain(0)
    combine(0)
    drain(1)
    combine(1)


def _combine(dest0, dest1, x1, rw, norm_w, y):
    n = x1.shape[0]
    tile = COMBINE_T
    kern = functools.partial(_combine_kernel, tile=tile)
    return pl.pallas_call(
        kern,
        grid=(n // tile,),
        in_specs=[
            pl.BlockSpec((tile,), lambda i: (i,), memory_space=pltpu.SMEM),
            pl.BlockSpec((tile,), lambda i: (i,), memory_space=pltpu.SMEM),
            pl.BlockSpec((tile, D_MODEL), lambda i: (i, 0)),
            pl.BlockSpec((tile, LANES), lambda i: (i, 0)),
            pl.BlockSpec((1, D_MODEL), lambda i: (0, 0)),
            pl.BlockSpec(memory_space=pl.ANY),
        ],
        out_specs=pl.BlockSpec((tile, D_MODEL), lambda i: (i, 0)),
        out_shape=jax.ShapeDtypeStruct((n, D_MODEL), F32),
        scratch_shapes=[
            pltpu.VMEM((tile * ROW_TILE, LANES), U32), pltpu.VMEM((tile * ROW_TILE, LANES), U32),
            pltpu.SemaphoreType.DMA((2,)),
        ],
        compiler_params=pltpu.CompilerParams(
            dimension_semantics=("arbitrary",), vmem_limit_bytes=VMEM_LIMIT),
        name="combine",
    )(dest0, dest1, x1, rw, norm_w, y)


def _pad_cols(w, width):
    return jnp.pad(w, ((0, 0), (0, width - w.shape[1])))


def _token_mixer_and_moe(x, norm1_w, w_in, w2_f, b_f, w2_b, b_b, gla_norm_w, conv_w, a_log_f, dt_bias_f,
                         a_log_b, dt_bias_b, gdn_norm_w, w_out, norm2_w, w_group, w_router, w_gate, w_up,
                         w_down, out_norm_w):
    batch, seq, d = x.shape
    n = batch * seq
    x2 = x.reshape(n, d)

    w_main = jnp.concatenate([w_in[:, :3072], w_in[:, 3104:7200], w_in[:, 7232:]], axis=1).astype(BF16)
    w_small = _pad_cols(jnp.concatenate([w_in[:, 3072:3104], w_in[:, 7200:7232]], axis=1), LANES)
    ws_hi, ws_lo = _split2(w_small)
    main, small = _inproj(x2, norm1_w.reshape(1, d), w_main, ws_hi, ws_lo)

    w2f_pad = jnp.zeros((LANES, GLA_HEADS * GLA_DK), F32).at[0:GLA_GATE_RANK].set(w2_f)
    w2b_pad = jnp.zeros((LANES, GLA_HEADS * GLA_DK), F32).at[GLA_GATE_RANK:2 * GLA_GATE_RANK].set(w2_b)
    ga = _gla(main, small, w2f_pad, w2b_pad, b_f.reshape(1, -1), b_b.reshape(1, -1),
              gla_norm_w.reshape(1, -1), batch, seq)

    gates = jnp.zeros((SUBLANES, LANES), F32)
    gates = gates.at[0, SMALL_AF:SMALL_BF].set(jnp.concatenate([a_log_f, a_log_b]))
    gates = gates.at[1, SMALL_AF:SMALL_BF].set(jnp.concatenate([dt_bias_f, dt_bias_b]))
    gb = _gdn(main, small, gates, conv_w, gdn_norm_w.reshape(1, -1), batch, seq)

    w_route = _pad_cols(jnp.concatenate([w_router, w_group], axis=1), LANES)
    wr_hi, wr_lo = _split2(w_route)
    x1, h2t, rt, rw, counts = _outproj(ga, gb, x2, w_out.astype(BF16), norm2_w.reshape(1, d), wr_hi, wr_lo)

    blk = MOE_BLOCK
    cnt = counts[0, :N_EXPERTS]
    padded = (cnt + blk - 1) // blk * blk
    ends = jnp.cumsum(padded)
    pstart = (ends - padded).astype(jnp.int32)
    n_blocks = -(-(2 * n + N_EXPERTS * (blk - 1)) // blk)
    n_rows = n_blocks * blk
    block_row = jnp.arange(n_blocks, dtype=jnp.int32) * blk
    block_expert = jnp.minimum(
        jnp.sum((ends[None, :] <= block_row[:, None]).astype(jnp.int32), axis=1), N_EXPERTS - 1)
    n_valid = (ends[-1:] // blk).astype(jnp.int32)
    seg = jnp.stack([jnp.append(pstart + cnt, ends[-1]), jnp.append(ends, n_rows)]).astype(jnp.int32)

    experts = jnp.arange(N_EXPERTS, dtype=jnp.int32)
    seg_start = jnp.sum(jnp.where(rt[0:2, :, None] == experts, pstart, 0), axis=-1)
    dest = seg_start + rt[2:4]
    dest0, dest1 = dest[0], dest[1]

    xs = _scatter(seg, dest0, dest1, h2t, n_rows)
    y = _experts(block_expert, n_valid, xs, w_gate, w_up, w_down)
    out = _combine(dest0, dest1, x1, rw, out_norm_w.reshape(1, d), y)
    return out.reshape(batch, seq, d)


def kernel(x, norm1_w, w_in, gla_gate_w2_fwd, gla_gate_b_fwd, gla_gate_w2_bwd, gla_gate_b_bwd, gla_norm_w,
           gdn_conv_w, gdn_a_log_fwd, gdn_dt_bias_fwd, gdn_a_log_bwd, gdn_dt_bias_bwd, gdn_norm_w, w_out,
           norm2_w, moe_w_group, moe_w_router, moe_w_gate, moe_w_up, moe_w_down, norm_f_w):
    assert norm1_w.shape[0] == 1, "single-layer block"
    return _token_mixer_and_moe(
        x, norm1_w[0], w_in[0], gla_gate_w2_fwd[0], gla_gate_b_fwd[0], gla_gate_w2_bwd[0], gla_gate_b_bwd[0],
        gla_norm_w[0], gdn_conv_w[0], gdn_a_log_fwd[0], gdn_dt_bias_fwd[0], gdn_a_log_bwd[0],
        gdn_dt_bias_bwd[0], gdn_norm_w[0], w_out[0], norm2_w[0], moe_w_group[0], moe_w_router[0],
        moe_w_gate[0], moe_w_up[0], moe_w_down[0], norm_f_w)
```

```python
import functools

import jax
import jax.numpy as jnp
import numpy as np
from jax import lax
from jax.experimental import pallas as pl
from jax.experimental.pallas import tpu as pltpu

F32 = jnp.float32
BF16 = jnp.bfloat16
U32 = jnp.uint32

D_MODEL = 1024
GLA_HEADS = 4
GLA_DK = 128
GLA_DV = 256
GLA_GATE_RANK = 16
GLA_GATE_TAU = 16.0
GLA_CHUNK = 64
GLA_GROUP = 8
GDN_HEADS = 8
GDN_DK = 128
GDN_DV = 128
GDN_CONV = 5
GDN_CHUNK = 128
GDN_PREP_GROUP = 8
N_GROUPS = 4
EXPERTS_PER_GROUP = 8
N_EXPERTS = N_GROUPS * EXPERTS_PER_GROUP
D_EXPERT = 256
EPS = 1e-6

LANES = 128
SUBLANES = 8
VMEM_LIMIT = 48 * 1024 * 1024

COL_GQ, COL_GK, COL_GV, COL_GR = 0, 512, 1024, 2048
COL_DQ, COL_DK, COL_DV, COL_DZ = 3072, 4096, 5120, 6144
COL_MA, COL_MB = 7168, 8192
D_MAIN = 9216
SMALL_AF, SMALL_AB, SMALL_BF, SMALL_BB = 32, 40, 48, 56
ROUTE_GROUP_LANE = 32

MOE_BLOCK = 512
ROW_TILE = D_MODEL // 2 // LANES
HIGH_HALF = np.uint32(0xFFFF0000)
ZERO_ROWS = 128
DMA_UNROLL = 8
INPROJ_TM, INPROJ_TN = 1024, 1024
OUTPROJ_TM = 512
OUTPROJ_SUB = 128
SCATTER_T = 1024
COMBINE_T = 1024
CONV_ROWS = 256
NEG_INF = float("-inf")


def _dot(a, b):
    return jnp.dot(a, b, preferred_element_type=F32)


def _dot_nt(a, b):
    return lax.dot_general(a, b, (((1,), (1,)), ((), ())), preferred_element_type=F32)


def _dot_tn(a, b):
    return lax.dot_general(a, b, (((0,), (0,)), ((), ())), preferred_element_type=F32)


def _split2(x):
    hi = x.astype(BF16)
    lo = (x - hi.astype(F32)).astype(BF16)
    return hi, lo


def _split3(x):
    hi = x.astype(BF16)
    r = x - hi.astype(F32)
    mid = r.astype(BF16)
    lo = (r - mid.astype(F32)).astype(BF16)
    return hi, mid, lo


def _dot_exact_rhs(x, m_bf16):
    hi, mid, lo = _split3(x)
    return _dot(hi, m_bf16) + _dot(mid, m_bf16) + _dot(lo, m_bf16)


def _dot_exact_lhs(m_bf16, x):
    hi, mid, lo = _split3(x)
    return _dot(m_bf16, hi) + _dot(m_bf16, mid) + _dot(m_bf16, lo)


def _dot_lhs2(m_bf16, x):
    hi, lo = _split2(x)
    return _dot(m_bf16, hi) + _dot(m_bf16, lo)


def _dot_lhs2_wide(m2_bf16, x):
    return _dot(m2_bf16, jnp.concatenate(_split2(x), axis=0))


def _cumsum_rows(x, reverse):
    rows = x.shape[0]
    row = _iota2(x.shape, 0)
    shift = 1
    while shift < rows:
        if reverse:
            x = x + jnp.where(row < rows - shift, pltpu.roll(x, rows - shift, axis=0), 0.0)
        else:
            x = x + jnp.where(row >= shift, pltpu.roll(x, shift, axis=0), 0.0)
        shift *= 2
    return x


def _dot3(a, b):
    ah, al = _split2(a)
    bh, bl = _split2(b)
    return _dot(ah, bh) + _dot(al, bh) + _dot(ah, bl)


def _store_row_tiles(ref, x):
    rows = x.shape[0]
    half = D_MODEL // 2
    hi = lax.bitcast_convert_type(x[:, :half].astype(BF16).astype(F32), U32)
    lo = lax.bitcast_convert_type(x[:, half:].astype(BF16).astype(F32), U32)
    packed = jnp.bitwise_or(jnp.bitwise_and(hi, HIGH_HALF), jnp.right_shift(lo, 16))
    for j in range(ROW_TILE):
        ref[pl.ds(j, rows, stride=ROW_TILE), :] = packed[:, j * LANES:(j + 1) * LANES]


def _load_row_tiles(ref):
    rows = ref.shape[0] // ROW_TILE
    packed = jnp.concatenate([ref[pl.ds(j, rows, stride=ROW_TILE), :] for j in range(ROW_TILE)], axis=1)
    hi = lax.bitcast_convert_type(jnp.bitwise_and(packed, HIGH_HALF), F32)
    lo = lax.bitcast_convert_type(jnp.left_shift(packed, 16), F32)
    return jnp.concatenate([hi, lo], axis=1)


def _each(fn, *lists):
    return [fn(*args) for args in zip(*lists)]


def _sigmoid(x):
    return 1.0 / (1.0 + jnp.exp(-x))


def _silu(x):
    return x * _sigmoid(x)


def _softplus(x):
    return jnp.maximum(x, 0.0) + jnp.log(1.0 + jnp.exp(-jnp.abs(x)))


def _log_sigmoid(x):
    return jnp.minimum(x, 0.0) - jnp.log(1.0 + jnp.exp(-jnp.abs(x)))


def _iota2(shape, dim):
    return lax.broadcasted_iota(jnp.int32, shape, dim)


def _inproj_kernel(x_ref, nw_ref, w_ref, wsh_ref, wsl_ref, main_ref, small_ref, h_scr):
    @pl.when(pl.program_id(1) == 0)
    def _():
        x = x_ref[...]
        h = x * lax.rsqrt(jnp.mean(x * x, axis=-1, keepdims=True) + EPS) * nw_ref[...]
        hh, hl = _split2(h)
        h_scr[...] = hh
        small_ref[...] = _dot(hh, wsh_ref[...]) + _dot(hl, wsh_ref[...]) + _dot(hh, wsl_ref[...])

    main_ref[...] = _dot(h_scr[...], w_ref[...]).astype(BF16)


def _inproj(x2, norm_w, w_main, ws_hi, ws_lo):
    n = x2.shape[0]
    tm, tn = INPROJ_TM, INPROJ_TN
    return pl.pallas_call(
        _inproj_kernel,
        grid=(n // tm, D_MAIN // tn),
        in_specs=[
            pl.BlockSpec((tm, D_MODEL), lambda i, j: (i, 0)),
            pl.BlockSpec((1, D_MODEL), lambda i, j: (0, 0)),
            pl.BlockSpec((D_MODEL, tn), lambda i, j: (0, j)),
            pl.BlockSpec((D_MODEL, LANES), lambda i, j: (0, 0)),
            pl.BlockSpec((D_MODEL, LANES), lambda i, j: (0, 0)),
        ],
        out_specs=[
            pl.BlockSpec((tm, tn), lambda i, j: (i, j)),
            pl.BlockSpec((tm, LANES), lambda i, j: (i, 0)),
        ],
        out_shape=[
            jax.ShapeDtypeStruct((n, D_MAIN), BF16),
            jax.ShapeDtypeStruct((n, LANES), F32),
        ],
        scratch_shapes=[pltpu.VMEM((tm, D_MODEL), BF16)],
        compiler_params=pltpu.CompilerParams(
            dimension_semantics=("arbitrary", "arbitrary"), vmem_limit_bytes=VMEM_LIMIT),
        name="inproj",
    )(x2, norm_w, w_main, ws_hi, ws_lo)


def _gla_kernel(q_ref, k_ref, v_ref, gr_ref, ma_ref, small_ref, w2f_ref, w2b_ref, bf_ref, bb_ref,
                nw_ref, out_ref, laf_scr, lab_scr, o_scr, stf_scr, stb_scr, *, seq, chunk):
    c = chunk
    n = seq // c
    scale = GLA_DK ** -0.5

    sm = small_ref[...].astype(BF16)
    laf_scr[...] = _log_sigmoid(_dot(sm, w2f_ref[...].astype(BF16)) + bf_ref[...]) * (1.0 / GLA_GATE_TAU)
    lab_scr[...] = _log_sigmoid(_dot(sm, w2b_ref[...].astype(BF16)) + bb_ref[...]) * (1.0 / GLA_GATE_TAU)
    stf_scr[...] = jnp.zeros_like(stf_scr)
    stb_scr[...] = jnp.zeros_like(stb_scr)

    row = _iota2((c, c), 0)
    col = _iota2((c, c), 1)
    low = row >= col
    upp = row <= col
    low_m = jnp.concatenate([jnp.where(low, 1.0, 0.0).astype(BF16)] * 2, axis=1)
    upp_m = jnp.concatenate([jnp.where(upp, 1.0, 0.0).astype(BF16)] * 2, axis=1)

    g = GLA_GROUP

    def finish(rows, o):
        y = o * lax.rsqrt(jnp.mean(o * o, axis=-1, keepdims=True) + EPS) * nw_ref[...]
        y = y * _silu(gr_ref[rows, :].astype(F32))
        y = y * _sigmoid(ma_ref[rows, :].astype(F32))
        out_ref[rows, :] = y.astype(BF16)

    def group(gi, second_touch):
        ids = [gi * g + j for j in range(g)] + [n - 1 - gi * g - j for j in range(g)]
        rows = [pl.ds(pl.multiple_of(i * c, c), c) for i in ids]
        la = [laf_scr[r, :] for r in rows[:g]] + [lab_scr[r, :] for r in rows[g:]]
        csum = [low_m] * g + [upp_m] * g
        mask = [low] * g + [upp] * g
        tot_row = [c - 1] * g + [0] * g
        qf = [q_ref[r, :].astype(F32) * scale for r in rows]
        kf = [k_ref[r, :].astype(F32) for r in rows]
        vc = [v_ref[r, :] for r in rows]

        cum = _each(_dot_lhs2_wide, csum, la)
        tot = _each(lambda x, r: x[r:r + 1, :], cum, tot_row)
        q_dec = _each(lambda q, x: (q * jnp.exp(x)).astype(BF16), qf, cum)
        k_inv = _each(lambda k, x: (k * jnp.exp(-x)).astype(BF16), kf, cum)
        k_tail = _each(lambda k, t, x: (k * jnp.exp(t - x)).astype(BF16), kf, tot, cum)
        s = _each(lambda m, q, k: jnp.where(m, _dot_nt(q, k), 0.0).astype(BF16), mask, q_dec, k_inv)
        o = _each(_dot, s, vc)
        kv = _each(_dot_tn, vc, k_tail)
        dec = _each(jnp.exp, tot)

        for st_scr, probs in ((stf_scr, range(g)), (stb_scr, range(g, 2 * g))):
            st = st_scr[...]
            for p in probs:
                o[p] = o[p] + _dot_nt(q_dec[p], st.astype(BF16))
                st = dec[p] * st + kv[p]
            st_scr[...] = st

        for r, o_p in zip(rows, o):
            if second_touch:
                finish(r, o_scr[r, :] + o_p)
            else:
                o_scr[r, :] = o_p

    def first_half(gi, carry):
        group(gi, False)
        return carry

    def second_half(gi, carry):
        group(gi, True)
        return carry

    n_groups = n // g
    lax.fori_loop(0, n_groups // 2, first_half, 0)
    lax.fori_loop(n_groups // 2, n_groups, second_half, 0)


def _gla(main, small, w2f_pad, w2b_pad, b_f, b_b, norm_w, batch, seq):
    n = batch * seq
    h = GLA_HEADS
    kern = functools.partial(_gla_kernel, seq=seq, chunk=GLA_CHUNK)
    qk_blk = lambda off: pl.BlockSpec((seq, GLA_DK), lambda b, hh, off=off: (b, off // GLA_DK + hh))
    v_blk = lambda off: pl.BlockSpec((seq, GLA_DV), lambda b, hh, off=off: (b, off // GLA_DV + hh))
    return pl.pallas_call(
        kern,
        grid=(batch, h),
        in_specs=[
            qk_blk(COL_GQ), qk_blk(COL_GK), v_blk(COL_GV), v_blk(COL_GR), v_blk(COL_MA),
            pl.BlockSpec((seq, LANES), lambda b, hh: (b, 0)),
            pl.BlockSpec((LANES, GLA_DK), lambda b, hh: (0, hh)),
            pl.BlockSpec((LANES, GLA_DK), lambda b, hh: (0, hh)),
            pl.BlockSpec((1, GLA_DK), lambda b, hh: (0, hh)),
            pl.BlockSpec((1, GLA_DK), lambda b, hh: (0, hh)),
            pl.BlockSpec((1, GLA_DV), lambda b, hh: (0, 0)),
        ],
        out_specs=pl.BlockSpec((seq, GLA_DV), lambda b, hh: (b, hh)),
        out_shape=jax.ShapeDtypeStruct((n, D_MODEL), BF16),
        scratch_shapes=[
            pltpu.VMEM((seq, GLA_DK), F32), pltpu.VMEM((seq, GLA_DK), F32),
            pltpu.VMEM((seq, GLA_DV), F32),
            pltpu.VMEM((GLA_DV, GLA_DK), F32), pltpu.VMEM((GLA_DV, GLA_DK), F32),
        ],
        compiler_params=pltpu.CompilerParams(
            dimension_semantics=("arbitrary", "arbitrary"), vmem_limit_bytes=VMEM_LIMIT),
        name="gla",
    )(main, main, main, main, main, small, w2f_pad, w2b_pad, b_f, b_b, norm_w)


TRI_BLOCK = 16


def _mm(a, b):
    return _dot(a.astype(BF16), b.astype(BF16))


def _nilpotent_inverse(a_list, eye, index, tick):
    t_list = _each(lambda a: eye - a, a_list)
    p_list = a_list
    power = 2
    while power < index:
        p_list = _each(lambda p: _mm(p, p), p_list)
        tick()
        t_list = _each(lambda t, p: t + _mm(t, p), t_list, p_list)
        tick()
        power *= 2
    return t_list


def _tri_inverse(a_list, eye, diag_blocks, chunk, tick):
    ad_list = _each(lambda a: jnp.where(diag_blocks, a, 0.0), a_list)
    ao_list = _each(lambda a: jnp.where(diag_blocks, 0.0, a), a_list)
    d_list = _nilpotent_inverse(ad_list, eye, TRI_BLOCK, tick)
    n_list = _each(_mm, d_list, ao_list)
    tick()
    t_list = _nilpotent_inverse(n_list, eye, chunk // TRI_BLOCK, tick)
    out = _each(_mm, t_list, d_list)
    tick()
    return out


def _gdn_kernel(gate_ref, q_ref, k_ref, v_ref, z_ref, mb_ref, small_ref, cwq_ref, cwk_ref, cwv_ref,
                nw_ref, out_ref, pad_scr, qs_scr, ks_scr, vs_scr, gf_scr, gb_scr, btf_scr, btb_scr,
                o_scr, nmat_scr, bmat_scr, qp_scr, cd_scr, sf_scr, sb_scr, *, seq, chunk, n_heads_total):
    c = chunk
    n = seq // c
    step = pl.program_id(0)
    hh = lax.rem(jnp.minimum(step, n_heads_total - 1), GDN_HEADS)
    cur = lax.rem(step, 2)
    prev = 1 - cur
    scale = GDN_DK ** -0.5

    @pl.when(step == 0)
    def _():
        nmat_scr[...] = jnp.zeros_like(nmat_scr)
        bmat_scr[...] = jnp.zeros_like(bmat_scr)
        qp_scr[...] = jnp.zeros_like(qp_scr)
        cd_scr[...] = jnp.zeros_like(cd_scr)
        o_scr[...] = jnp.zeros_like(o_scr)

    zeros8 = jnp.zeros((SUBLANES, LANES), F32)
    pad_scr[0:SUBLANES, :] = zeros8
    pad_scr[seq + SUBLANES:seq + 2 * SUBLANES, :] = zeros8
    half = GDN_CONV // 2

    def conv_into(src_ref, cw_ref, dst_ref, normalise, mult):
        pad_scr[SUBLANES:seq + SUBLANES, :] = src_ref[...].astype(F32)
        w = cw_ref[...]

        def body(i, carry):
            r0 = pl.multiple_of(i * CONV_ROWS, CONV_ROWS)
            acc = jnp.zeros((CONV_ROWS, LANES), F32)
            for j in range(GDN_CONV):
                tap = pad_scr[pl.ds(r0 + (SUBLANES - half + j), CONV_ROWS), :]
                acc = acc + tap * w[j:j + 1, :]
            y = _silu(acc)
            if normalise:
                y = y * lax.rsqrt(jnp.sum(y * y, axis=-1, keepdims=True) + EPS) * mult
            dst_ref[pl.ds(r0, CONV_ROWS), :] = y.astype(BF16)
            return carry

        lax.fori_loop(0, seq // CONV_ROWS, body, 0, unroll=4)

    conv_into(q_ref, cwq_ref, qs_scr, True, scale)
    conv_into(k_ref, cwk_ref, ks_scr, True, 1.0)
    conv_into(v_ref, cwv_ref, vs_scr, False, 1.0)

    sm = small_ref[...]
    lane = _iota2(sm.shape, 1)
    log_decay = -jnp.exp(gate_ref[0:1, :]) * _softplus(sm + gate_ref[1:2, :])
    gate_vals = jnp.where(lane < SMALL_BF, log_decay, _sigmoid(sm))
    gate_hl = jnp.concatenate(_split2(gate_vals), axis=1)
    sel_lane = jnp.bitwise_and(_iota2((2 * LANES, 4 * LANES), 0), LANES - 1)
    sel_gate = jnp.right_shift(_iota2((2 * LANES, 4 * LANES), 1), LANES.bit_length() - 1)
    sel = jnp.where(sel_lane == SMALL_AF + GDN_HEADS * sel_gate + hh, 1.0, 0.0).astype(BF16)
    spread = _dot(gate_hl, sel)
    gf_scr[...] = spread[:, 0 * LANES:1 * LANES]
    gb_scr[...] = spread[:, 1 * LANES:2 * LANES]
    btf_scr[...] = spread[:, 2 * LANES:3 * LANES]
    btb_scr[...] = spread[:, 3 * LANES:4 * LANES]
    sf_scr[...] = jnp.zeros_like(sf_scr)
    sb_scr[...] = jnp.zeros_like(sb_scr)

    row = _iota2((c, c), 0)
    col = _iota2((c, c), 1)
    eye = jnp.where(row == col, 1.0, 0.0).astype(F32)
    low, slow = row >= col, row > col
    upp, supp = row <= col, row < col
    assert c == LANES, "the decay matrix is formed from a [c, 128] lane-broadcast column and its transpose"

    tri_shift = TRI_BLOCK.bit_length() - 1
    diag_blocks = jnp.right_shift(row, tri_shift) == jnp.right_shift(col, tri_shift)

    cur_slot, prev_slot = cur * (2 * n), prev * (2 * n)
    cur_row, prev_row = cur * seq, prev * seq

    def scan_step(i):
        j = n - 1 - i
        slots = [prev_slot + i, prev_slot + n + j]
        rows = [pl.ds(pl.multiple_of(prev_row + i * c, c), c), pl.ds(pl.multiple_of(prev_row + j * c, c), c)]
        states = [sf_scr, sb_scr]
        s = [ref[...] for ref in states]
        s_b = _each(lambda x: x.astype(BF16), s)
        ns = _each(lambda sl, x: _dot(nmat_scr[sl], x), slots, s_b)
        qs = _each(lambda sl, x: _dot(qp_scr[sl], x), slots, s_b)
        for ref, sl, s_, ns_ in zip(states, slots, s, ns):
            ref[...] = cd_scr[sl][0:1, :] * s_ + (bmat_scr[sl] - ns_)
        for r, q in zip(rows, qs):
            o_scr[r, :] += q

    n_groups = n // GDN_PREP_GROUP
    scans_per_group = n // n_groups

    def prep_group(gi, carry):
        pending = [gi * scans_per_group + j for j in range(scans_per_group)]

        def tick():
            if pending:
                scan_step(pending.pop(0))

        chunk_ids = [gi * GDN_PREP_GROUP + j for j in range(GDN_PREP_GROUP)]
        rows = [pl.ds(pl.multiple_of(i * c, c), c) for i in chunk_ids]
        qc = [qs_scr[r, :] for r in rows]
        kc = [ks_scr[r, :] for r in rows]
        vc = [vs_scr[r, :] for r in rows]
        gl = [ref[r, :] for r in rows for ref in (gf_scr, gb_scr)]
        bt = [ref[r, :] for r in rows for ref in (btf_scr, btb_scr)]
        reverse = [False, True] * GDN_PREP_GROUP
        incl = [low, upp] * GDN_PREP_GROUP
        strict = [slow, supp] * GDN_PREP_GROUP
        tot_row = [c - 1, 0] * GDN_PREP_GROUP

        def both(per_chunk):
            return [x for x in per_chunk for _ in range(2)]

        kk = both(_each(_dot_nt, kc, kc))
        qk = both(_each(_dot_nt, qc, kc))
        tick()
        qf = both(_each(lambda x: x.astype(F32), qc))
        kf = both(_each(lambda x: x.astype(F32), kc))
        vf = both(_each(lambda x: x.astype(F32), vc))

        gc = _each(_cumsum_rows, gl, reverse)
        tot = _each(lambda g, r: g[r:r + 1, :], gc, tot_row)
        e = _each(lambda g, inc: jnp.exp(jnp.where(inc, g - jnp.transpose(g), 0.0)), gc, incl)
        a = _each(lambda kk_, b, e_, st: kk_ * b[:, :c] * jnp.where(st, e_, 0.0), kk, bt, e, strict)
        t_inv = _tri_inverse(a, eye, diag_blocks, c, tick)
        egc = _each(jnp.exp, gc)
        wu = _each(lambda t, k, v, b, eg: _mm(t, jnp.concatenate([k * b * eg, v * b], axis=1)).astype(BF16),
                   t_inv, kf, vf, bt, egc)
        tick()
        attn = _each(lambda qk_, e_, inc: (qk_ * jnp.where(inc, e_, 0.0)).astype(BF16), qk, e, incl)
        k_tail = _each(lambda k, t, g: (k * jnp.exp(t - g)).astype(BF16), kf, tot, gc)
        kwu = _each(_dot_tn, k_tail, wu)
        tick()
        awu = _each(_dot, attn, wu)
        while pending:
            tick()

        for p in range(2 * GDN_PREP_GROUP):
            slot = cur_slot + chunk_ids[p // 2] + (p % 2) * n
            nmat_scr[slot] = kwu[p][:, :GDN_DK].astype(BF16)
            bmat_scr[slot] = kwu[p][:, GDN_DK:]
            qp_scr[slot] = (qf[p] * egc[p] - awu[p][:, :GDN_DK]).astype(BF16)
            cd_scr[slot] = jnp.broadcast_to(jnp.exp(tot[p]), (SUBLANES, LANES))
        for j, i in enumerate(chunk_ids):
            r = pl.ds(pl.multiple_of(cur_row + i * c, c), c)
            o_scr[r, :] = awu[2 * j][:, GDN_DK:] + awu[2 * j + 1][:, GDN_DK:]
        return carry

    lax.fori_loop(0, n_groups, prep_group, 0)

    def finish(i, carry):
        r0 = pl.multiple_of(i * CONV_ROWS, CONV_ROWS)
        o = o_scr[pl.ds(pl.multiple_of(prev_row + r0, CONV_ROWS), CONV_ROWS), :]
        y = o * lax.rsqrt(jnp.mean(o * o, axis=-1, keepdims=True) + EPS) * nw_ref[...]
        y = y * _silu(z_ref[pl.ds(r0, CONV_ROWS), :].astype(F32))
        y = y * _sigmoid(mb_ref[pl.ds(r0, CONV_ROWS), :].astype(F32))
        out_ref[pl.ds(r0, CONV_ROWS), :] = y.astype(BF16)
        return carry

    lax.fori_loop(0, seq // CONV_ROWS, finish, 0, unroll=4)


def _gdn(main, small, gates, conv_w, norm_w, batch, seq):
    n = batch * seq
    total = batch * GDN_HEADS
    kern = functools.partial(_gdn_kernel, seq=seq, chunk=GDN_CHUNK, n_heads_total=total)
    n_chunks = seq // GDN_CHUNK

    def head_of(step):
        idx = jnp.minimum(step, total - 1)
        return idx // GDN_HEADS, idx % GDN_HEADS

    def prev_head_of(step):
        idx = jnp.maximum(step - 1, 0)
        return idx // GDN_HEADS, idx % GDN_HEADS

    def blk(off, which):
        def index(s):
            b, hh = which(s)
            return b, off // LANES + hh
        return pl.BlockSpec((seq, LANES), index)

    def cw(part):
        return pl.BlockSpec((GDN_CONV, LANES), lambda s: (0, part * GDN_HEADS + head_of(s)[1]))

    seq_f32 = lambda: pltpu.VMEM((seq, LANES), F32)
    seq_bf16 = lambda: pltpu.VMEM((seq, LANES), BF16)
    return pl.pallas_call(
        kern,
        grid=(total + 1,),
        in_specs=[
            pl.BlockSpec((SUBLANES, LANES), lambda s: (0, 0)),
            blk(COL_DQ, head_of), blk(COL_DK, head_of), blk(COL_DV, head_of),
            blk(COL_DZ, prev_head_of), blk(COL_MB, prev_head_of),
            pl.BlockSpec((seq, LANES), lambda s: (head_of(s)[0], 0)),
            cw(0), cw(1), cw(2),
            pl.BlockSpec((1, GDN_DV), lambda s: (0, 0)),
        ],
        out_specs=pl.BlockSpec((seq, GDN_DV), lambda s: prev_head_of(s)),
        out_shape=jax.ShapeDtypeStruct((n, D_MODEL), BF16),
        scratch_shapes=[
            pltpu.VMEM((seq + 2 * SUBLANES, LANES), F32),
            seq_bf16(), seq_bf16(), seq_bf16(),
            seq_f32(), seq_f32(), seq_f32(), seq_f32(),
            pltpu.VMEM((2 * seq, LANES), F32),
            pltpu.VMEM((4 * n_chunks, GDN_DK, GDN_DK), BF16),
            pltpu.VMEM((4 * n_chunks, GDN_DK, GDN_DV), F32),
            pltpu.VMEM((4 * n_chunks, GDN_CHUNK, GDN_DK), BF16),
            pltpu.VMEM((4 * n_chunks, SUBLANES, LANES), F32),
            pltpu.VMEM((GDN_DK, GDN_DV), F32), pltpu.VMEM((GDN_DK, GDN_DV), F32),
        ],
        compiler_params=pltpu.CompilerParams(
            dimension_semantics=("arbitrary",), vmem_limit_bytes=VMEM_LIMIT),
        name="gdn",
    )(gates, main, main, main, main, main, small, conv_w, conv_w, conv_w, norm_w)


def _outproj_kernel(ga_ref, gb_ref, x_ref, wo_ref, nw_ref, wrh_ref, wrl_ref,
                    x1_ref, h2_ref, ri_ref, rw_ref, cnt_ref, carry_scr, *, tm):
    @pl.when(pl.program_id(0) == 0)
    def _():
        carry_scr[...] = jnp.zeros_like(carry_scr)

    sub = OUTPROJ_SUB
    subs = [pl.ds(j * sub, sub) for j in range(tm // sub)]
    lane_i = _iota2((sub, LANES), 1)
    lane = lane_i.astype(F32)

    mixed = [(ga_ref[r, :].astype(F32) + gb_ref[r, :].astype(F32)).astype(BF16) for r in subs]
    x1 = _each(lambda r, m: x_ref[r, :] + _dot(m, wo_ref[...]), subs, mixed)
    for r, v in zip(subs, x1):
        x1_ref[r, :] = v
    h2 = _each(lambda v: v * lax.rsqrt(jnp.mean(v * v, axis=-1, keepdims=True) + EPS) * nw_ref[...], x1)
    for j, v in enumerate(h2):
        _store_row_tiles(h2_ref.at[pl.ds(j * sub * ROW_TILE, sub * ROW_TILE)], v)
    hl = _each(_split2, h2)
    lg = _each(lambda p: _dot(p[0], wrh_ref[...]) + _dot(p[1], wrh_ref[...]) + _dot(p[0], wrl_ref[...]), hl)

    def row_max(vals):
        return _each(lambda v: jnp.max(v, axis=-1, keepdims=True), vals)

    def first_lane_of(vals, maxima):
        return _each(lambda v, m: jnp.min(jnp.where(v == m, lane, float(LANES)), axis=-1, keepdims=True),
                     vals, maxima)

    is_g = (lane_i >= ROUTE_GROUP_LANE) & (lane_i < ROUTE_GROUP_LANE + N_GROUPS)
    gl = _each(lambda v: jnp.where(is_g, v, NEG_INF), lg)
    gmax = row_max(gl)
    gidx = _each(lambda i: i - float(ROUTE_GROUP_LANE), first_lane_of(gl, gmax))
    gsum = _each(lambda v, m: jnp.sum(jnp.where(is_g, jnp.exp(v - m), 0.0), axis=-1, keepdims=True), lg, gmax)
    lane_group = jnp.right_shift(lane_i, EXPERTS_PER_GROUP.bit_length() - 1).astype(F32)
    el = _each(lambda v, g: jnp.where((lane_i < N_EXPERTS) & (lane_group == g), v, NEG_INF), lg, gidx)
    m1 = row_max(el)
    e0 = first_lane_of(el, m1)
    el2 = _each(lambda v, i: jnp.where(lane == i, NEG_INF, v), el, e0)
    m2 = row_max(el2)
    e1 = first_lane_of(el2, m2)
    ratio = _each(lambda a, b: jnp.exp(b - a), m1, m2)
    w0 = _each(lambda s_, r: 1.0 / (s_ * (1.0 + r)), gsum, ratio)
    w1 = _each(lambda w, r: w * r, w0, ratio)

    pick0 = _each(lambda i: lane == i, e0)
    pick1 = _each(lambda i: lane == i, e1)
    onehot = jnp.concatenate(_each(lambda p, q: jnp.where(p | q, 1.0, 0.0), pick0, pick1), axis=0)
    trow = _iota2((tm, tm), 0)
    tcol = _iota2((tm, tm), 1)
    before = jnp.where(trow > tcol, 1.0, 0.0).astype(BF16)
    cnt = _dot(before, onehot.astype(BF16)) + carry_scr[0:1, :]
    cnts = [cnt[j * sub:(j + 1) * sub, :] for j in range(tm // sub)]
    rank0 = _each(lambda p, c_: jnp.sum(jnp.where(p, c_, 0.0), axis=-1, keepdims=True), pick0, cnts)
    rank1 = _each(lambda p, c_: jnp.sum(jnp.where(p, c_, 0.0), axis=-1, keepdims=True), pick1, cnts)
    total = carry_scr[0:1, :] + jnp.sum(onehot, axis=0, keepdims=True)
    carry_scr[...] = jnp.broadcast_to(total, carry_scr.shape)
    cnt_ref[...] = jnp.broadcast_to(total, cnt_ref.shape).astype(jnp.int32)

    for j, r in enumerate(subs):
        ri = jnp.where(lane_i == 0, e0[j], jnp.where(lane_i == 1, e1[j], 0.0))
        ri = jnp.where(lane_i == 2, rank0[j], jnp.where(lane_i == 3, rank1[j], ri))
        ri_ref[:, r] = jnp.transpose(ri)[0:SUBLANES, :].astype(jnp.int32)
        rw_ref[r, :] = jnp.where(lane_i == 0, w0[j], jnp.where(lane_i == 1, w1[j], 0.0))


def _outproj(ga, gb, x2, w_out, norm_w, wr_hi, wr_lo):
    n = x2.shape[0]
    tm = OUTPROJ_TM
    kern = functools.partial(_outproj_kernel, tm=tm)
    row_blk = lambda w: pl.BlockSpec((tm, w), lambda i: (i, 0))
    const = lambda shape: pl.BlockSpec(shape, lambda i: (0, 0))
    return pl.pallas_call(
        kern,
        grid=(n // tm,),
        in_specs=[
            row_blk(D_MODEL), row_blk(D_MODEL), row_blk(D_MODEL),
            const((D_MODEL, D_MODEL)), const((1, D_MODEL)),
            const((D_MODEL, LANES)), const((D_MODEL, LANES)),
        ],
        out_specs=[row_blk(D_MODEL),
                   pl.BlockSpec((tm * ROW_TILE, LANES), lambda i: (i, 0)),
                   pl.BlockSpec((SUBLANES, tm), lambda i: (0, i)),
                   row_blk(LANES),
                   const((SUBLANES, LANES))],
        out_shape=[
            jax.ShapeDtypeStruct((n, D_MODEL), F32),
            jax.ShapeDtypeStruct((n * ROW_TILE, LANES), U32),
            jax.ShapeDtypeStruct((SUBLANES, n), jnp.int32),
            jax.ShapeDtypeStruct((n, LANES), F32),
            jax.ShapeDtypeStruct((SUBLANES, LANES), jnp.int32),
        ],
        scratch_shapes=[pltpu.VMEM((SUBLANES, LANES), F32)],
        compiler_params=pltpu.CompilerParams(
            dimension_semantics=("arbitrary",), vmem_limit_bytes=VMEM_LIMIT),
        name="outproj",
    )(ga, gb, x2, w_out, norm_w, wr_hi, wr_lo)


def _row_copy(src_ref, src_row, dst_ref, dst_row, sem):
    src = src_ref.at[pl.ds(pl.multiple_of(src_row * ROW_TILE, ROW_TILE), ROW_TILE)]
    dst = dst_ref.at[pl.ds(pl.multiple_of(dst_row * ROW_TILE, ROW_TILE), ROW_TILE)]
    return pltpu.make_async_copy(src, dst, sem)


def _rows_copy(src_ref, dst_ref, dst_row, n_rows, sem):
    src = src_ref.at[pl.ds(0, n_rows * ROW_TILE)]
    dst = dst_ref.at[pl.ds(pl.multiple_of(dst_row * ROW_TILE, ROW_TILE), n_rows * ROW_TILE)]
    return pltpu.make_async_copy(src, dst, sem)


def _zero_fill(zero_scr, xs_ref, lo, hi, sem, wait):
    def go(copy):
        if wait:
            copy.wait()
        else:
            copy.start()

    length = hi - lo
    n_full = lax.shift_right_logical(length, ZERO_ROWS.bit_length() - 1)

    def full(j, carry):
        go(_rows_copy(zero_scr, xs_ref, lo + j * ZERO_ROWS, ZERO_ROWS, sem))
        return carry

    lax.fori_loop(0, n_full, full, 0)
    pos = lo + n_full * ZERO_ROWS
    piece = ZERO_ROWS // 2
    while piece >= 1:
        has = jnp.bitwise_and(length, piece) != 0

        @pl.when(has)
        def _(pos=pos, piece=piece):
            go(_rows_copy(zero_scr, xs_ref, pos, piece, sem))

        pos = pos + jnp.where(has, piece, 0)
        piece //= 2


def _scatter_kernel(seg_ref, d0_ref, d1_ref, h2_ref, xs_ref, zero_scr, sem, zsem, *, tile):
    i = pl.program_id(0)

    def issue(t, carry):
        _row_copy(h2_ref, t, xs_ref, d0_ref[t], sem).start(priority=0)
        _row_copy(h2_ref, t, xs_ref, d1_ref[t], sem).start(priority=1)
        return carry

    lax.fori_loop(0, tile, issue, 0, unroll=DMA_UNROLL)

    @pl.when(i == 0)
    def _():
        zero_scr[...] = jnp.zeros_like(zero_scr)
        for wait in (False, True):
            def per_segment(e, carry, wait=wait):
                _zero_fill(zero_scr, xs_ref, seg_ref[0, e], seg_ref[1, e], zsem, wait)
                return carry

            lax.fori_loop(0, N_EXPERTS + 1, per_segment, 0)

    def drain(t, carry):
        _row_copy(h2_ref, 0, xs_ref, 0, sem).wait()
        _row_copy(h2_ref, 0, xs_ref, 0, sem).wait()
        return carry

    lax.fori_loop(0, tile, drain, 0, unroll=DMA_UNROLL)


def _scatter(seg, dest0, dest1, h2t, n_rows):
    n = dest0.shape[0]
    tile = SCATTER_T
    kern = functools.partial(_scatter_kernel, tile=tile)
    return pl.pallas_call(
        kern,
        grid=(n // tile,),
        in_specs=[
            pl.BlockSpec(memory_space=pltpu.SMEM),
            pl.BlockSpec((tile,), lambda i: (i,), memory_space=pltpu.SMEM),
            pl.BlockSpec((tile,), lambda i: (i,), memory_space=pltpu.SMEM),
            pl.BlockSpec((tile * ROW_TILE, LANES), lambda i: (i, 0)),
        ],
        out_specs=pl.BlockSpec(memory_space=pl.ANY),
        out_shape=jax.ShapeDtypeStruct((n_rows * ROW_TILE, LANES), U32),
        scratch_shapes=[pltpu.VMEM((ZERO_ROWS * ROW_TILE, LANES), U32),
                        pltpu.SemaphoreType.DMA, pltpu.SemaphoreType.DMA],
        compiler_params=pltpu.CompilerParams(
            dimension_semantics=("arbitrary",), vmem_limit_bytes=VMEM_LIMIT),
        name="scatter",
    )(seg, dest0, dest1, h2t)


def _expert_kernel(be_ref, nv_ref, xs_ref, wg_ref, wu_ref, wd_ref, y_ref):
    i = pl.program_id(0)

    @pl.when(i < nv_ref[0])
    def _():
        x = _load_row_tiles(xs_ref).astype(BF16)
        g = _dot(x, wg_ref[0].astype(BF16))
        u = _dot(x, wu_ref[0].astype(BF16))
        hid = (_silu(g) * u).astype(BF16)
        _store_row_tiles(y_ref, _dot(hid, wd_ref[0].astype(BF16)))

    @pl.when(i >= nv_ref[0])
    def _():
        y_ref[...] = jnp.zeros_like(y_ref)


def _experts(block_expert, n_valid, xs, w_gate, w_up, w_down):
    blk = MOE_BLOCK
    n_rows = xs.shape[0] // ROW_TILE
    grid_spec = pltpu.PrefetchScalarGridSpec(
        num_scalar_prefetch=2,
        grid=(n_rows // blk,),
        in_specs=[
            pl.BlockSpec((blk * ROW_TILE, LANES), lambda i, be, nv: (jnp.minimum(i, nv[0] - 1), 0)),
            pl.BlockSpec((1, D_MODEL, D_EXPERT), lambda i, be, nv: (be[i], 0, 0)),
            pl.BlockSpec((1, D_MODEL, D_EXPERT), lambda i, be, nv: (be[i], 0, 0)),
            pl.BlockSpec((1, D_EXPERT, D_MODEL), lambda i, be, nv: (be[i], 0, 0)),
        ],
        out_specs=pl.BlockSpec((blk * ROW_TILE, LANES), lambda i, be, nv: (i, 0)),
    )
    return pl.pallas_call(
        _expert_kernel,
        grid_spec=grid_spec,
        out_shape=jax.ShapeDtypeStruct((n_rows * ROW_TILE, LANES), U32),
        compiler_params=pltpu.CompilerParams(
            dimension_semantics=("arbitrary",), vmem_limit_bytes=VMEM_LIMIT),
        name="experts",
    )(block_expert, n_valid, xs, w_gate, w_up, w_down)


def _combine_kernel(d0_ref, d1_ref, x1_ref, rw_ref, nw_ref, y_ref, out_ref, ya_scr, yb_scr, sems, *, tile):
    half = tile // 2

    def issue(h):
        def body(t, carry):
            _row_copy(y_ref, d0_ref[t], ya_scr, t, sems.at[h]).start(priority=0)
            _row_copy(y_ref, d1_ref[t], yb_scr, t, sems.at[h]).start(priority=1)
            return carry
        lax.fori_loop(h * half, (h + 1) * half, body, 0, unroll=DMA_UNROLL)

    def drain(h):
        def body(t, carry):
            _row_copy(y_ref, 0, ya_scr, 0, sems.at[h]).wait()
            _row_copy(y_ref, 0, yb_scr, 0, sems.at[h]).wait()
            return carry
        lax.fori_loop(0, half, body, 0, unroll=DMA_UNROLL)

    def combine(h):
        rows = pl.ds(h * half, half)
        tiles = pl.ds(h * half * ROW_TILE, half * ROW_TILE)
        rw = rw_ref[rows, :]
        moe = rw[:, 0:1] * _load_row_tiles(ya_scr.at[tiles]) + rw[:, 1:2] * _load_row_tiles(yb_scr.at[tiles])
        x2 = x1_ref[rows, :] + moe
        out_ref[rows, :] = x2 * lax.rsqrt(jnp.mean(x2 * x2, axis=-1, keepdims=True) + EPS) * nw_ref[...]

    issue(0)
    issue(1)
    drain(0)
    combine(0)
    drain(1)
    combine(1)


def _combine(dest0, dest1, x1, rw, norm_w, y):
    n = x1.shape[0]
    tile = COMBINE_T
    kern = functools.partial(_combine_kernel, tile=tile)
    return pl.pallas_call(
        kern,
        grid=(n // tile,),
        in_specs=[
            pl.BlockSpec((tile,), lambda i: (i,), memory_space=pltpu.SMEM),
            pl.BlockSpec((tile,), lambda i: (i,), memory_space=pltpu.SMEM),
            pl.BlockSpec((tile, D_MODEL), lambda i: (i, 0)),
            pl.BlockSpec((tile, LANES), lambda i: (i, 0)),
            pl.BlockSpec((1, D_MODEL), lambda i: (0, 0)),
            pl.BlockSpec(memory_space=pl.ANY),
        ],
        out_specs=pl.BlockSpec((tile, D_MODEL), lambda i: (i, 0)),
        out_shape=jax.ShapeDtypeStruct((n, D_MODEL), F32),
        scratch_shapes=[
            pltpu.VMEM((tile * ROW_TILE, LANES), U32), pltpu.VMEM((tile * ROW_TILE, LANES), U32),
            pltpu.SemaphoreType.DMA((2,)),
        ],
        compiler_params=pltpu.CompilerParams(
            dimension_semantics=("arbitrary",), vmem_limit_bytes=VMEM_LIMIT),
        name="combine",
    )(dest0, dest1, x1, rw, norm_w, y)


def _pad_cols(w, width):
    return jnp.pad(w, ((0, 0), (0, width - w.shape[1])))


def _token_mixer_and_moe(x, norm1_w, w_in, w2_f, b_f, w2_b, b_b, gla_norm_w, conv_w, a_log_f, dt_bias_f,
                         a_log_b, dt_bias_b, gdn_norm_w, w_out, norm2_w, w_group, w_router, w_gate, w_up,
                         w_down, out_norm_w):
    batch, seq, d = x.shape
    n = batch * seq
    x2 = x.reshape(n, d)

    w_main = jnp.concatenate([w_in[:, :3072], w_in[:, 3104:7200], w_in[:, 7232:]], axis=1).astype(BF16)
    w_small = _pad_cols(jnp.concatenate([w_in[:, 3072:3104], w_in[:, 7200:7232]], axis=1), LANES)
    ws_hi, ws_lo = _split2(w_small)
    main, small = _inproj(x2, norm1_w.reshape(1, d), w_main, ws_hi, ws_lo)

    w2f_pad = jnp.zeros((LANES, GLA_HEADS * GLA_DK), F32).at[0:GLA_GATE_RANK].set(w2_f)
    w2b_pad = jnp.zeros((LANES, GLA_HEADS * GLA_DK), F32).at[GLA_GATE_RANK:2 * GLA_GATE_RANK].set(w2_b)
    ga = _gla(main, small, w2f_pad, w2b_pad, b_f.reshape(1, -1), b_b.reshape(1, -1),
              gla_norm_w.reshape(1, -1), batch, seq)

    gates = jnp.zeros((SUBLANES, LANES), F32)
    gates = gates.at[0, SMALL_AF:SMALL_BF].set(jnp.concatenate([a_log_f, a_log_b]))
    gates = gates.at[1, SMALL_AF:SMALL_BF].set(jnp.concatenate([dt_bias_f, dt_bias_b]))
    gb = _gdn(main, small, gates, conv_w, gdn_norm_w.reshape(1, -1), batch, seq)

    w_route = _pad_cols(jnp.concatenate([w_router, w_group], axis=1), LANES)
    wr_hi, wr_lo = _split2(w_route)
    x1, h2t, rt, rw, counts = _outproj(ga, gb, x2, w_out.astype(BF16), norm2_w.reshape(1, d), wr_hi, wr_lo)

    blk = MOE_BLOCK
    cnt = counts[0, :N_EXPERTS]
    padded = (cnt + blk - 1) // blk * blk
    ends = jnp.cumsum(padded)
    pstart = (ends - padded).astype(jnp.int32)
    n_blocks = -(-(2 * n + N_EXPERTS * (blk - 1)) // blk)
    n_rows = n_blocks * blk
    block_row = jnp.arange(n_blocks, dtype=jnp.int32) * blk
    block_expert = jnp.minimum(
        jnp.sum((ends[None, :] <= block_row[:, None]).astype(jnp.int32), axis=1), N_EXPERTS - 1)
    n_valid = (ends[-1:] // blk).astype(jnp.int32)
    seg = jnp.stack([jnp.append(pstart + cnt, ends[-1]), jnp.append(ends, n_rows)]).astype(jnp.int32)

    experts = jnp.arange(N_EXPERTS, dtype=jnp.int32)
    seg_start = jnp.sum(jnp.where(rt[0:2, :, None] == experts, pstart, 0), axis=-1)
    dest = seg_start + rt[2:4]
    dest0, dest1 = dest[0], dest[1]

    xs = _scatter(seg, dest0, dest1, h2t, n_rows)
    y = _experts(block_expert, n_valid, xs, w_gate, w_up, w_down)
    out = _combine(dest0, dest1, x1, rw, out_norm_w.reshape(1, d), y)
    return out.reshape(batch, seq, d)


def kernel(x, norm1_w, w_in, gla_gate_w2_fwd, gla_gate_b_fwd, gla_gate_w2_bwd, gla_gate_b_bwd, gla_norm_w,
           gdn_conv_w, gdn_a_log_fwd, gdn_dt_bias_fwd, gdn_a_log_bwd, gdn_dt_bias_bwd, gdn_norm_w, w_out,
           norm2_w, moe_w_group, moe_w_router, moe_w_gate, moe_w_up, moe_w_down, norm_f_w):
    assert norm1_w.shape[0] == 1, "single-layer block"
    return _token_mixer_and_moe(
        x, norm1_w[0], w_in[0], gla_gate_w2_fwd[0], gla_gate_b_fwd[0], gla_gate_w2_bwd[0], gla_gate_b_bwd[0],
        gla_norm_w[0], gdn_conv_w[0], gdn_a_log_fwd[0], gdn_dt_bias_fwd[0], gdn_a_log_bwd[0],
        gdn_dt_bias_bwd[0], gdn_norm_w[0], w_out[0], norm2_w[0], moe_w_group[0], moe_w_router[0],
        moe_w_gate[0], moe_w_up[0], moe_w_down[0], norm_f_w)
```
